```python
import jax, jax.numpy as jnp
from jax import lax
import numpy as np

D_MODEL = 1024
BATCH = 8
SEQ = 2048
DEPTH = 1
DEC_BATCH = 128
DEC_SEQ = 1
PAST_LEN = 16384
PAGE_SIZE = 128

HEAD_DIM = 64
N_HEADS = D_MODEL // HEAD_DIM
D_RWKV = N_HEADS * HEAD_DIM
D_CONV = D_MODEL // 2
CONV_W = 31
LORA_DECAY = 64
LORA_ICLR = 64
RMS_EPS = 1e-6
GN_EPS = 64e-5
LN_EPS = 1e-5

N_SHIFT = 4 * D_RWKV + LORA_DECAY + LORA_ICLR
N_IN = N_SHIFT + 2 * D_CONV + D_CONV + 2 * D_MODEL

kernel_name = "rwkv7_conformer_gated_parallel_step"


def rms_norm(x, g):
    xf = x.astype(jnp.float32)
    y = xf * lax.rsqrt(jnp.mean(xf * xf, axis=-1, keepdims=True) + RMS_EPS)
    return (y * g.astype(jnp.float32)).astype(x.dtype)


def wkv7_scan(s0, r, w, k, v, a, b):
    def step(s, inp):
        rt, wt, kt, vt, at, bt = inp
        sa = jnp.einsum('bhij,bhj->bhi', s, at)
        s = s * wt[:, :, None, :] + sa[..., None] * bt[:, :, None, :] + vt[..., None] * kt[:, :, None, :]
        yt = jnp.einsum('bhij,bhj->bhi', s, rt)
        return s, yt
    xs = tuple(jnp.swapaxes(t, 0, 1) for t in (r, w, k, v, a, b))
    s_fin, ys = lax.scan(step, s0, xs)
    return jnp.swapaxes(ys, 0, 1), s_fin


def mixer_layer(x, shift_st, wkv_st, conv_st, norm_pre_g, w_in, mu_shift, decay_w0, decay_w2,
                iclr_a0, iclr_a2, k_k, k_a, r_k, gn_g, gn_b, conv_glu_b, conv_w, conv_b,
                ln_c_g, ln_c_b, w_branch_r, w_branch_c, w_out, norm_post_g):
    B, T, _ = x.shape
    f32 = jnp.float32
    h = rms_norm(x, norm_pre_g)
    proj = jnp.einsum('btd,dn->btn', h, w_in)
    ps = proj[..., :N_SHIFT]
    prev_row = jnp.einsum('bd,dn->bn', shift_st, w_in[:, :N_SHIFT])
    ps_prev = jnp.concatenate([prev_row[:, None, :], ps[:, :-1]], axis=1)
    ps = ps + (ps_prev - ps) * mu_shift
    o = 0
    r = ps[..., o:o + D_RWKV]; o += D_RWKV
    k = ps[..., o:o + D_RWKV]; o += D_RWKV
    v = ps[..., o:o + D_RWKV]; o += D_RWKV
    z_r = ps[..., o:o + D_RWKV]; o += D_RWKV
    wl = ps[..., o:o + LORA_DECAY]; o += LORA_DECAY
    al = ps[..., o:o + LORA_ICLR]
    o = N_SHIFT
    u_in = proj[..., o:o + 2 * D_CONV]; o += 2 * D_CONV
    z_c = proj[..., o:o + D_CONV]; o += D_CONV
    g_r = proj[..., o:o + D_MODEL]; o += D_MODEL
    g_c = proj[..., o:o + D_MODEL]

    w_raw = (decay_w0 + jnp.tanh(wl) @ decay_w2).astype(f32)
    w_log = -jax.nn.softplus(-w_raw) - 0.5
    decay = jnp.exp(-jnp.exp(w_log))
    iclr = jax.nn.sigmoid((iclr_a0 + al @ iclr_a2).astype(f32))
    rf, kf, vf = r.astype(f32), k.astype(f32), v.astype(f32)
    hs = lambda t: t.reshape(B, T, N_HEADS, HEAD_DIM)
    kk = hs(kf * k_k.astype(f32))
    kk = kk * lax.rsqrt(jnp.sum(kk * kk, axis=-1, keepdims=True) + 1e-12)
    kf = kf * (1.0 + (iclr - 1.0) * k_a.astype(f32))
    rh, kh, vh, ih = hs(rf), hs(kf), hs(vf), hs(iclr)
    y_w, wkv_new = wkv7_scan(wkv_st.astype(f32), rh, hs(decay), kh, vh, -kk, kk * ih)
    mu = jnp.mean(y_w, axis=-1, keepdims=True)
    var = jnp.mean(jnp.square(y_w - mu), axis=-1, keepdims=True)
    y_gn = ((y_w - mu) * lax.rsqrt(var + GN_EPS)).reshape(B, T, D_RWKV) * gn_g + gn_b
    bonus = jnp.sum(rh * kh * r_k.astype(f32), axis=-1, keepdims=True) * vh
    o_r = (y_gn + bonus.reshape(B, T, D_RWKV)).astype(x.dtype) * jax.nn.silu(z_r)
    branch_r = o_r @ w_branch_r

    u_in = u_in + conv_glu_b
    u = u_in[..., :D_CONV] * jax.nn.sigmoid(u_in[..., D_CONV:])
    buf = jnp.concatenate([conv_st.astype(u.dtype), u], axis=1)
    c = lax.conv_general_dilated(buf, conv_w[:, None, :].astype(u.dtype), window_strides=(1,),
                                 padding='VALID', dimension_numbers=('NWC', 'WIO', 'NWC'),
                                 feature_group_count=D_CONV) + conv_b
    cf = c.astype(f32)
    cm = jnp.mean(cf, axis=-1, keepdims=True)
    cv = jnp.mean(jnp.square(cf - cm), axis=-1, keepdims=True)
    cn = ((cf - cm) * lax.rsqrt(cv + LN_EPS) * ln_c_g + ln_c_b).astype(x.dtype)
    o_c = jax.nn.silu(cn) * jax.nn.silu(z_c)
    branch_c = o_c @ w_branch_c
    conv_new = buf[:, -(CONV_W - 1):]

    merged = jax.nn.sigmoid(g_r) * branch_r + jax.nn.sigmoid(g_c) * branch_c
    out = merged @ w_out
    y = x + rms_norm(out, norm_post_g)
    return y, h[:, -1], wkv_new.astype(wkv_st.dtype), conv_new


def setup_inputs(seed: int = 0) -> dict:
    key = jax.random.key(seed)
    ks = jax.random.split(key, 32)
    nrm = lambda i, shape, s: jax.random.normal(ks[i], shape, jnp.float32) * s
    L = DEPTH
    return {
        "x_prompt": nrm(0, (BATCH, SEQ, D_MODEL), 1.0),
        "x_sample": nrm(1, (DEC_BATCH, DEC_SEQ, D_MODEL), 1.0),
        "state_shift": nrm(2, (L, DEC_BATCH, D_MODEL), 1.0),
        "state_wkv": nrm(3, (L, DEC_BATCH, N_HEADS, HEAD_DIM, HEAD_DIM), 0.3),
        "state_conv": nrm(4, (L, DEC_BATCH, CONV_W - 1, D_CONV), 0.5),
        "norm_pre_g": 1.0 + nrm(5, (L, D_MODEL), 0.02),
        "w_in": nrm(6, (L, D_MODEL, N_IN), D_MODEL ** -0.5),
        "mu_shift": jax.random.uniform(ks[7], (L, N_SHIFT), jnp.float32),
        "decay_w0": jax.random.uniform(ks[8], (L, D_RWKV), jnp.float32, -6.0, -1.0),
        "decay_w2": nrm(9, (L, LORA_DECAY, D_RWKV), 0.5 * LORA_DECAY ** -0.5),
        "iclr_a0": nrm(10, (L, D_RWKV), 0.1),
        "iclr_a2": nrm(11, (L, LORA_ICLR, D_RWKV), LORA_ICLR ** -0.5),
        "k_k": 0.85 + nrm(12, (L, D_RWKV), 0.02),
        "k_a": 1.0 + nrm(13, (L, D_RWKV), 0.02),
        "r_k": nrm(14, (L, N_HEADS, HEAD_DIM), 0.1),
        "gn_g": 1.0 + nrm(15, (L, D_RWKV), 0.02),
        "gn_b": nrm(16, (L, D_RWKV), 0.02),
        "conv_glu_b": nrm(17, (L, 2 * D_CONV), 0.02),
        "conv_w": nrm(18, (L, CONV_W, D_CONV), CONV_W ** -0.5),
        "conv_b": nrm(19, (L, D_CONV), 0.02),
        "ln_c_g": 1.0 + nrm(20, (L, D_CONV), 0.02),
        "ln_c_b": nrm(21, (L, D_CONV), 0.02),
        "w_branch_r": nrm(22, (L, D_RWKV, D_MODEL), D_RWKV ** -0.5),
        "w_branch_c": nrm(23, (L, D_CONV, D_MODEL), D_CONV ** -0.5),
        "w_out": nrm(24, (L, D_MODEL, D_MODEL), D_MODEL ** -0.5),
        "norm_post_g": 1.0 + nrm(25, (L, D_MODEL), 0.02),
    }


def reference(x_prompt, x_sample, state_shift, state_wkv, state_conv, norm_pre_g, w_in, mu_shift,
              decay_w0, decay_w2, iclr_a0, iclr_a2, k_k, k_a, r_k, gn_g, gn_b, conv_glu_b, conv_w,
              conv_b, ln_c_g, ln_c_b, w_branch_r, w_branch_c, w_out, norm_post_g):
    dt = x_prompt.dtype
    xp, xs = x_prompt, x_sample
    sp_shift, sp_wkv, sp_conv = [], [], []
    ss_shift, ss_wkv, ss_conv = [], [], []
    for l in range(DEPTH):
        params = (norm_pre_g[l], w_in[l], mu_shift[l], decay_w0[l], decay_w2[l], iclr_a0[l],
                  iclr_a2[l], k_k[l], k_a[l], r_k[l], gn_g[l], gn_b[l], conv_glu_b[l], conv_w[l],
                  conv_b[l], ln_c_g[l], ln_c_b[l], w_branch_r[l], w_branch_c[l], w_out[l],
                  norm_post_g[l])
        p_shift0 = jnp.zeros((xp.shape[0], D_MODEL), dt)
        p_wkv0 = jnp.zeros((xp.shape[0], N_HEADS, HEAD_DIM, HEAD_DIM), state_wkv.dtype)
        p_conv0 = jnp.zeros((xp.shape[0], CONV_W - 1, D_CONV), state_conv.dtype)
        xp, a1, a2, a3 = mixer_layer(xp, p_shift0, p_wkv0, p_conv0, *params)
        sp_shift.append(a1); sp_wkv.append(a2); sp_conv.append(a3)
        xs, b1, b2, b3 = mixer_layer(xs, state_shift[l], state_wkv[l], state_conv[l], *params)
        ss_shift.append(b1); ss_wkv.append(b2); ss_conv.append(b3)
    new_shift_prompt = jnp.stack(sp_shift)
    new_wkv_prompt = jnp.stack(sp_wkv)
    new_conv_prompt = jnp.stack(sp_conv)
    new_shift_sample = jnp.stack(ss_shift)
    new_wkv_sample = jnp.stack(ss_wkv)
    new_conv_sample = jnp.stack(ss_conv)
    return (xp, xs, new_shift_prompt, new_wkv_prompt, new_conv_prompt,
            new_shift_sample, new_wkv_sample, new_conv_sample)
```

```python
import functools
import math

import jax
import jax.numpy as jnp
from jax import lax
from jax.experimental import pallas as pl
from jax.experimental.pallas import tpu as pltpu

F32 = jnp.float32
BF16 = jnp.bfloat16

D_MODEL = 1024
HEAD_DIM = 64
N_HEADS = D_MODEL // HEAD_DIM
D_RWKV = N_HEADS * HEAD_DIM
D_CONV = D_MODEL // 2
CONV_W = 31
LORA = 64
RMS_EPS = 1e-6
GN_EPS = 64e-5
LN_EPS = 1e-5
N_SHIFT = 4 * D_RWKV + 2 * LORA
N_IN = N_SHIFT + 2 * D_CONV + D_CONV + 2 * D_MODEL

SUBLANES = 8
PAIR = 2 * HEAD_DIM
N_PAIRS = N_HEADS // 2
MXU_DIM = 256
CHUNK = 64
CONV_PAD = 32
VMEM_LIMIT = 56 * 1024 * 1024

_O_R, _O_K, _O_V, _O_ZR = 0, D_RWKV, 2 * D_RWKV, 3 * D_RWKV
_O_LORA = 4 * D_RWKV
_O_U = N_SHIFT
_O_ZC = _O_U + 2 * D_CONV
_O_GR = _O_ZC + D_CONV
_O_GC = _O_GR + D_MODEL


def _dot(a, b):
    return jnp.dot(a, b, preferred_element_type=F32)


def _dot_nt(a, b):
    return lax.dot_general(a, b, (((1,), (1,)), ((), ())), preferred_element_type=F32)


def _sigmoid(x):
    return 1.0 / (1.0 + jnp.exp(-x))


def _split_hi_lo(x):
    hi = x.astype(BF16)
    lo = (x - hi.astype(F32)).astype(BF16)
    return hi, lo


def _head_sum(x, ones_bd):
    outs = []
    for c in range(x.shape[1] // MXU_DIM):
        hi, lo = _split_hi_lo(x[:, c * MXU_DIM:(c + 1) * MXU_DIM])
        outs.append(_dot(hi, ones_bd) + _dot(lo, ones_bd))
    return jnp.concatenate(outs, axis=1)


def _group_norm_gate(y, bonus, szr, gng, gnb, ones_bd):
    mean = _head_sum(y, ones_bd) * (1.0 / HEAD_DIM)
    d = y - mean
    var = _head_sum(d * d, ones_bd) * (1.0 / HEAD_DIM)
    y_gn = d * lax.rsqrt(var + GN_EPS) * gng + gnb
    return ((y_gn + bonus) * szr).astype(BF16)


def _prep_kernel(sample, *refs):
    if sample:
        (x_ref, prev_ref, gpre_ref, win_ref, mu_ref, w0_ref, a0_ref, w2c_ref, kk_ref, ka_ref, rk_ref,
         glub_ref, ones_ref,
         r_out, k_out, v_out, a_out, b_out, lw_out, szr_out, bon_out, u_out, szc_out, sgr_out, sgc_out,
         h_out) = refs
    else:
        (x_ref, gpre_ref, win_ref, mu_ref, w0_ref, a0_ref, w2c_ref, kk_ref, ka_ref, rk_ref,
         glub_ref, ones_ref,
         r_out, k_out, v_out, a_out, b_out, lw_out, szr_out, bon_out, u_out, szc_out, sgr_out, sgc_out,
         h_out, carry_ref) = refs

    x = x_ref[0]
    tm = x.shape[0]
    ms = jnp.mean(x * x, axis=-1, keepdims=True)
    h = x * lax.rsqrt(ms + RMS_EPS) * gpre_ref[...]
    hb = h.astype(BF16)

    if sample:
        h_out[0] = h
        pb = prev_ref[0].astype(BF16)
    else:
        h_out[0] = h[tm - SUBLANES:, :]

        @pl.when(pl.program_id(1) == 0)
        def _():
            carry_ref[...] = jnp.zeros_like(carry_ref)
        row0 = lax.broadcasted_iota(jnp.int32, (tm, 1), 0) == 0

    def shifted(off, width):
        p = _dot(hb, win_ref[:, off:off + width])
        if sample:
            pp = _dot(pb, win_ref[:, off:off + width])
        else:
            rolled = pltpu.roll(p, shift=1, axis=0)
            pp = jnp.where(row0, carry_ref[0:1, off:off + width], rolled)
            carry_ref[0:1, off:off + width] = p[tm - 1:tm, :]
        return p + (pp - p) * mu_ref[:, off:off + width]

    ones_bd = ones_ref[...]

    lo = shifted(_O_LORA, 2 * LORA)
    lane = lax.broadcasted_iota(jnp.int32, lo.shape, 1)
    lo = jnp.where(lane < LORA, jnp.tanh(lo), lo)
    lor = _dot(lo.astype(BF16), w2c_ref[...])
    w_raw = w0_ref[...] + lor[:, :D_RWKV]
    lw_out[0] = (-math.exp(-0.5)) * _sigmoid(w_raw)
    iclr = _sigmoid(a0_ref[...] + lor[:, D_RWKV:])

    k = shifted(_O_K, D_RWKV)
    kk = k * kk_ref[...]
    kk = kk * lax.rsqrt(_head_sum(kk * kk, ones_bd) + 1e-12)
    k2 = k * (1.0 + (iclr - 1.0) * ka_ref[...])
    k_out[0] = k2
    a_out[0] = -kk
    b_out[0] = kk * iclr

    r = shifted(_O_R, D_RWKV)
    r_out[0] = r
    bsum = _head_sum(r * k2 * rk_ref[...], ones_bd)
    v = shifted(_O_V, D_RWKV)
    v_out[0] = v
    bon_out[0] = bsum * v

    zr = shifted(_O_ZR, D_RWKV)
    szr_out[0] = zr * _sigmoid(zr)

    u_in = _dot(hb, win_ref[:, _O_U:_O_U + 2 * D_CONV]) + glub_ref[...]
    u_out[0] = u_in[:, :D_CONV] * _sigmoid(u_in[:, D_CONV:])
    zc = _dot(hb, win_ref[:, _O_ZC:_O_ZC + D_CONV])
    szc_out[0] = zc * _sigmoid(zc)
    sgr_out[0] = _sigmoid(_dot(hb, win_ref[:, _O_GR:_O_GR + D_MODEL]))
    sgc_out[0] = _sigmoid(_dot(hb, win_ref[:, _O_GC:_O_GC + D_MODEL]))


def _const_spec(shape):
    nd = len(shape)
    return pl.BlockSpec(shape, lambda *_: (0,) * nd, pipeline_mode=pl.Buffered(1))


def _prep(x, prev, p, tm):
    B, T, _ = x.shape
    sample = prev is not None
    tile = lambda w: pl.BlockSpec((1, tm, w), lambda b, j: (b, j, 0))
    in_specs = [tile(D_MODEL)] + ([tile(D_MODEL)] if sample else []) + [
        _const_spec((1, D_MODEL)), _const_spec((D_MODEL, N_IN)), _const_spec((1, N_SHIFT)),
        _const_spec((1, D_RWKV)), _const_spec((1, D_RWKV)), _const_spec((2 * LORA, 2 * D_RWKV)),
        _const_spec((1, D_RWKV)), _const_spec((1, D_RWKV)), _const_spec((1, D_RWKV)),
        _const_spec((1, 2 * D_CONV)), _const_spec((MXU_DIM, MXU_DIM))]
    widths = [D_RWKV] * 8 + [D_CONV, D_CONV, D_MODEL, D_MODEL]
    h_spec =(tile(D_MODEL) if sample else
              pl.BlockSpec((1, SUBLANES, D_MODEL), lambda b, j: (b, 0, 0)))
    out_specs = [tile(w) for w in widths] + [h_spec]
    out_shape = [jax.ShapeDtypeStruct((B, T, w), F32) for w in widths] + [
        jax.ShapeDtypeStruct((B, T if sample else SUBLANES, D_MODEL), F32)]
    args = [x] + ([prev] if sample else []) + [
        p["gpre"], p["w_in"], p["mu"], p["w0"], p["a0"], p["w2c"], p["k_k"], p["k_a"], p["r_k"],
        p["glu_b"], p["ones_bd"]]
    return pl.pallas_call(
        functools.partial(_prep_kernel, sample),
        grid=(B, T // tm),
        in_specs=in_specs,
        out_specs=out_specs,
        out_shape=out_shape,
        scratch_shapes=[] if sample else [pltpu.VMEM((SUBLANES, N_SHIFT), F32)],
        compiler_params=pltpu.CompilerParams(
            dimension_semantics=("arbitrary", "arbitrary"), vmem_limit_bytes=VMEM_LIMIT),
        name="prep_sample" if sample else "prep_prompt",
    )(*args)


def _block_diag(x, same_head):
    return jnp.where(same_head, jnp.concatenate([x, x], axis=0), jnp.zeros((), x.dtype))


def _wkv_chunk_kernel(r_ref, k_ref, v_ref, a_ref, b_ref, lw_ref, szr_ref, bon_ref, gng_ref, gnb_ref, ones_ref,
                      o_ref, s_out, s_ref):
    tm = r_ref.shape[1]
    j = pl.program_id(1)

    @pl.when(j == 0)
    def _():
        s_ref[...] = jnp.zeros_like(s_ref)

    row = lax.broadcasted_iota(jnp.int32, (CHUNK, PAIR), 0)
    col = lax.broadcasted_iota(jnp.int32, (CHUNK, PAIR), 1) % HEAD_DIM
    strict = row > col
    incl = row >= col
    eye_pair = (row == col).astype(F32)
    brow = lax.broadcasted_iota(jnp.int32, (PAIR, PAIR), 0)
    bcol = lax.broadcasted_iota(jnp.int32, (PAIR, PAIR), 1)
    same_head = (brow // HEAD_DIM) == (bcol // HEAD_DIM)
    eye_full = brow == bcol
    first_head = lax.broadcasted_iota(jnp.int32, (CHUNK, PAIR), 1) < HEAD_DIM
    trow = lax.broadcasted_iota(jnp.int32, (CHUNK, CHUNK), 0)
    tcol = lax.broadcasted_iota(jnp.int32, (CHUNK, CHUNK), 1)
    tri = (trow >= tcol).astype(BF16)
    ones_bd = ones_ref[...]
    bd = lambda x: _block_diag(x.astype(BF16), same_head)

    def chunk_body(c, carry):
        rows = pl.ds(pl.multiple_of(c * CHUNK, CHUNK), CHUNK)
        lw = lw_ref[0, rows, :]
        l1 = lw.astype(BF16)
        rem = lw - l1.astype(F32)
        l2 = rem.astype(BF16)
        l3 = (rem - l2.astype(F32)).astype(BF16)
        cl = _dot(tri, l1) + _dot(tri, l2) + _dot(tri, l3)
        e_neg = jnp.exp(-cl)
        w_end = jnp.exp(cl[CHUNK - 1:CHUNK, :])
        r_t = r_ref[0, rows, :] * jnp.exp(cl)
        a_t = a_ref[0, rows, :] * jnp.exp(cl - lw)
        b_t = b_ref[0, rows, :] * e_neg
        k_t = k_ref[0, rows, :] * e_neg
        b_e = b_t * w_end
        k_e = k_t * w_end
        v_all = v_ref[0, rows, :]

        ys = []
        for p in range(N_PAIRS):
            sl = slice(p * PAIR, (p + 1) * PAIR)
            at, rt, bt, kt, be, ke, vp = (t[:, sl] for t in (a_t, r_t, b_t, k_t, b_e, k_e, v_all))
            g = _dot_nt(jnp.concatenate([at, rt], axis=0).astype(BF16),
                        jnp.concatenate([bd(bt), bd(kt)], axis=0))
            l_ab = jnp.where(strict, g[:CHUNK, :PAIR], 0.0)
            l_ak = jnp.where(strict, g[:CHUNK, PAIR:], 0.0)
            m_rb = jnp.where(incl, g[CHUNK:, :PAIR], 0.0).astype(BF16)
            m_rk = jnp.where(incl, g[CHUNK:, PAIR:], 0.0).astype(BF16)
            bd_v = bd(vp)
            uv0 = _dot(l_ak.astype(BF16), bd_v)
            t_acc = eye_pair + l_ab
            lp = _dot(l_ab.astype(BF16), bd(l_ab))
            for _ in range(4):
                prod = _dot(lp.astype(BF16), jnp.concatenate([bd(lp), bd(t_acc)], axis=1))
                lp = prod[:, :PAIR]
                t_acc = t_acc + prod[:, PAIR:]
            t_acc = t_acc + _dot(lp.astype(BF16), bd(t_acc))
            au = _dot(t_acc.astype(BF16), jnp.concatenate([bd(at), bd(uv0)], axis=1))
            a_hat, u_v = au[:, :PAIR], au[:, PAIR:]
            ry = _dot(m_rb, jnp.concatenate([bd(a_hat), bd(u_v)], axis=1))
            r_hat = rt + ry[:, :PAIR]
            y_v = ry[:, PAIR:] + _dot(m_rk, bd_v)
            phi = _dot(a_hat.T.astype(BF16), be.astype(BF16))
            phi = jnp.where(same_head, phi, 0.0) + jnp.where(eye_full, w_end[:, sl], 0.0)
            f = _dot(jnp.concatenate([u_v, vp], axis=0).T.astype(BF16),
                     jnp.concatenate([be, ke], axis=0).astype(BF16))
            z = jnp.where(first_head, f[:HEAD_DIM, :], f[HEAD_DIM:, :])
            s0 = s_ref[p]
            ys.append(_dot_nt(r_hat.astype(BF16), bd(s0)) + y_v)
            s_ref[p] = _dot(s0.astype(BF16), phi.astype(BF16)) + z

        y = jnp.concatenate(ys, axis=1)
        o_ref[0, rows, :] = _group_norm_gate(y, bon_ref[0, rows, :], szr_ref[0, rows, :],
                                             gng_ref[...], gnb_ref[...], ones_bd)
        return carry

    lax.fori_loop(0, tm // CHUNK, chunk_body, 0)

    @pl.when(j == pl.num_programs(1) - 1)
    def _():
        for hd in range(N_HEADS):
            lo = (hd % 2) * HEAD_DIM
            s_out[0, hd] = s_ref[hd // 2][:, lo:lo + HEAD_DIM]


def _wkv_prompt(pre, p, tm):
    r, k, v, a, b, lw, szr, bon = pre
    B, T, _ = r.shape
    tile = pl.BlockSpec((1, tm, D_RWKV), lambda bb, j: (bb, j, 0))
    return pl.pallas_call(
        _wkv_chunk_kernel,
        grid=(B, T // tm),
        in_specs=[tile] * 8 + [_const_spec((1, D_RWKV)), _const_spec((1, D_RWKV)),
                               _const_spec((MXU_DIM, MXU_DIM))],
        out_specs=[tile, pl.BlockSpec((1, N_HEADS, HEAD_DIM, HEAD_DIM), lambda bb, j: (bb, 0, 0, 0))],
        out_shape=[jax.ShapeDtypeStruct((B, T, D_RWKV), BF16),
                   jax.ShapeDtypeStruct((B, N_HEADS, HEAD_DIM, HEAD_DIM), F32)],
        scratch_shapes=[pltpu.VMEM((N_PAIRS, HEAD_DIM, PAIR), F32)],
        compiler_params=pltpu.CompilerParams(
            dimension_semantics=("arbitrary", "arbitrary"), vmem_limit_bytes=VMEM_LIMIT),
        name="wkv_prompt",
    )(r, k, v, a, b, lw, szr, bon, p["gn_g"], p["gn_b"], p["ones_bd"])


def _wkv_step_kernel(r_ref, k_ref, v_ref, a_ref, b_ref, lw_ref, szr_ref, bon_ref, gng_ref, gnb_ref,
                     s_in, o_ref, s_out, y_ref):
    nb = s_in.shape[0]
    eye_b = (lax.broadcasted_iota(jnp.int32, (HEAD_DIM, HEAD_DIM), 0)
             == lax.broadcasted_iota(jnp.int32, (HEAD_DIM, HEAD_DIM), 1)).astype(BF16)
    zero4 = jnp.zeros((SUBLANES - 4, HEAD_DIM), BF16)
    zero6 = jnp.zeros((SUBLANES - 2, HEAD_DIM), BF16)

    def seq_body(i, carry):
        a_all, b_all, k_all, v_all, r_all = (t[i] for t in (a_ref, b_ref, k_ref, v_ref, r_ref))
        w_all = jnp.exp(lw_ref[i])
        for hd in range(N_HEADS):
            row = slice(hd, hd + 1)
            s0 = s_in[i, hd]
            a_hi, a_lo = _split_hi_lo(a_all[row])
            v_hi, v_lo = _split_hi_lo(v_all[row])
            q = jnp.concatenate([a_hi, a_lo, v_hi, v_lo, zero4], axis=0)
            s_hi, s_lo = _split_hi_lo(s0)
            cols = _dot_nt(jnp.concatenate([s_hi, s_lo, eye_b], axis=0), q)
            sa = cols[:HEAD_DIM, 0:1] + cols[:HEAD_DIM, 1:2] + cols[HEAD_DIM:2 * HEAD_DIM, 0:1]
            vcol = cols[2 * HEAD_DIM:, 2:3] + cols[2 * HEAD_DIM:, 3:4]
            s1 = s0 * w_all[row] + sa * b_all[row] + vcol * k_all[row]
            s_out[i, hd] = s1
            n_hi, n_lo = _split_hi_lo(s1)
            r_hi, r_lo = _split_hi_lo(r_all[row])
            rq = jnp.concatenate([r_hi, r_lo, zero6], axis=0)
            yy = _dot_nt(rq, n_hi)
            yl = _dot_nt(rq, n_lo)
            y_ref[row, :] = yy[0:1, :] + yy[1:2, :] + yl[0:1, :]
        y = y_ref[...]
        mean = jnp.mean(y, axis=-1, keepdims=True)
        d = y - mean
        var = jnp.mean(d * d, axis=-1, keepdims=True)
        y_gn = d * lax.rsqrt(var + GN_EPS) * gng_ref[...] + gnb_ref[...]
        o_ref[i] = ((y_gn + bon_ref[i]) * szr_ref[i]).astype(BF16)
        return carry

    lax.fori_loop(0, nb, seq_body, 0)


def _wkv_sample(pre, state, p, nb):
    ns = state.shape[0]
    pre = [t.reshape(ns, N_HEADS, HEAD_DIM) for t in pre]
    vec = pl.BlockSpec((nb, N_HEADS, HEAD_DIM), lambda i: (i, 0, 0))
    st = pl.BlockSpec((nb, N_HEADS, HEAD_DIM, HEAD_DIM), lambda i: (i, 0, 0, 0))
    o_r, s_new = pl.pallas_call(
        _wkv_step_kernel,
        grid=(ns // nb,),
        in_specs=[vec] * 8 + [_const_spec((N_HEADS, HEAD_DIM)), _const_spec((N_HEADS, HEAD_DIM)), st],
        out_specs=[vec, st],
        out_shape=[jax.ShapeDtypeStruct((ns, N_HEADS, HEAD_DIM), BF16),
                   jax.ShapeDtypeStruct(state.shape, F32)],
        scratch_shapes=[pltpu.VMEM((N_HEADS, HEAD_DIM), F32)],
        compiler_params=pltpu.CompilerParams(
            dimension_semantics=("arbitrary",), vmem_limit_bytes=VMEM_LIMIT),
        name="wkv_sample",
    )(*pre, p["gn_g"].reshape(N_HEADS, HEAD_DIM), p["gn_b"].reshape(N_HEADS, HEAD_DIM), state)
    return o_r.reshape(1, ns, D_RWKV), s_new


def _post_kernel(sample, *refs):
    if sample:
        (x_ref, or_ref, u_ref, szc_ref, sgr_ref, sgc_ref, cw_ref, cb_ref, lng_ref, lnb_ref,
         wbr_ref, wbc_ref, wout_ref, gpost_ref, st_ref, y_ref, st_out) = refs
    else:
        (x_ref, or_ref, u_ref, szc_ref, sgr_ref, sgc_ref, cw_ref, cb_ref, lng_ref, lnb_ref,
         wbr_ref, wbc_ref, wout_ref, gpost_ref, y_ref, ubuf) = refs

    u = u_ref[0]
    tm = u.shape[0]
    acc = jnp.broadcast_to(cb_ref[...], u.shape)
    if sample:
        for kk in range(CONV_W - 1):
            acc = acc + cw_ref[kk:kk + 1, :] * st_ref[:, kk, :]
        acc = acc + cw_ref[CONV_W - 1:CONV_W, :] * u
        st_out[:, 0:CONV_W - 2, :] = st_ref[:, 1:CONV_W - 1, :]
        st_out[:, CONV_W - 2, :] = u
    else:
        @pl.when(pl.program_id(1) == 0)
        def _():
            ubuf[0:CONV_PAD, :] = jnp.zeros((CONV_PAD, D_CONV), F32)
        ubuf[CONV_PAD:CONV_PAD + tm, :] = u
        base = CONV_PAD - (CONV_W - 1)
        for kk in range(CONV_W):
            acc = acc + cw_ref[kk:kk + 1, :] * ubuf[base + kk:base + kk + tm, :]
        ubuf[0:CONV_PAD, :] = ubuf[tm:tm + CONV_PAD, :]

    cm = jnp.mean(acc, axis=-1, keepdims=True)
    d = acc - cm
    cv = jnp.mean(d * d, axis=-1, keepdims=True)
    cn = d * lax.rsqrt(cv + LN_EPS) * lng_ref[...] + lnb_ref[...]
    o_c = (cn * _sigmoid(cn)) * szc_ref[0]
    branch_c = _dot(o_c.astype(BF16), wbc_ref[...])
    branch_r = _dot(or_ref[0], wbr_ref[...])
    merged = sgr_ref[0] * branch_r + sgc_ref[0] * branch_c
    out = _dot(merged.astype(BF16), wout_ref[...])
    ms = jnp.mean(out * out, axis=-1, keepdims=True)
    y_ref[0] = x_ref[0] + out * lax.rsqrt(ms + RMS_EPS) * gpost_ref[...]


def _post(x, o_r, u, szc, sgr, sgc, conv_state, p, tm):
    B, T, _ = x.shape
    sample = conv_state is not None
    tile = lambda w: pl.BlockSpec((1, tm, w), lambda b, j: (b, j, 0))
    in_specs = [tile(D_MODEL), tile(D_RWKV), tile(D_CONV), tile(D_CONV), tile(D_MODEL), tile(D_MODEL),
                _const_spec((CONV_W, D_CONV)), _const_spec((1, D_CONV)), _const_spec((1, D_CONV)),
                _const_spec((1, D_CONV)), _const_spec((D_RWKV, D_MODEL)), _const_spec((D_CONV, D_MODEL)),
                _const_spec((D_MODEL, D_MODEL)), _const_spec((1, D_MODEL))]
    args = [x, o_r, u, szc, sgr, sgc, p["conv_w"], p["conv_b"], p["ln_g"], p["ln_b"],
            p["w_br"], p["w_bc"], p["w_out"], p["gpost"]]
    out_specs = [tile(D_MODEL)]
    out_shape = [jax.ShapeDtypeStruct((B, T, D_MODEL), F32)]
    scratch = []
    if sample:
        st = pl.BlockSpec((tm, CONV_W - 1, D_CONV), lambda b, j: (j, 0, 0))
        in_specs.append(st)
        args.append(conv_state)
        out_specs.append(st)
        out_shape.append(jax.ShapeDtypeStruct(conv_state.shape, F32))
    else:
        scratch = [pltpu.VMEM((tm + CONV_PAD, D_CONV), F32)]
    return pl.pallas_call(
        functools.partial(_post_kernel, sample),
        grid=(B, T // tm),
        in_specs=in_specs,
        out_specs=out_specs,
        out_shape=out_shape,
        scratch_shapes=scratch,
        compiler_params=pltpu.CompilerParams(
            dimension_semantics=("arbitrary", "arbitrary"), vmem_limit_bytes=VMEM_LIMIT),
        name="post_sample" if sample else "post_prompt",
    )(*args)


def _layer_params(l, norm_pre_g, w_in, mu_shift, decay_w0, decay_w2, iclr_a0, iclr_a2, k_k, k_a, r_k, gn_g, gn_b,
                  conv_glu_b, conv_w, conv_b, ln_c_g, ln_c_b, w_branch_r, w_branch_c, w_out, norm_post_g):
    row = lambda t: t[l].reshape(1, -1).astype(F32)
    zeros = jnp.zeros((LORA, D_RWKV), F32)
    w2c = jnp.concatenate([jnp.concatenate([decay_w2[l], zeros], axis=1),
                           jnp.concatenate([zeros, iclr_a2[l]], axis=1)], axis=0)
    blk = jnp.arange(MXU_DIM) // HEAD_DIM
    return dict(
        gpre=row(norm_pre_g), w_in=w_in[l].astype(BF16), mu=row(mu_shift), w0=row(decay_w0), a0=row(iclr_a0),
        w2c=w2c.astype(BF16), k_k=row(k_k), k_a=row(k_a), r_k=row(r_k), glu_b=row(conv_glu_b),
        ones_bd=(blk[:, None] == blk[None, :]).astype(BF16),
        gn_g=row(gn_g), gn_b=row(gn_b), conv_w=conv_w[l].astype(F32), conv_b=row(conv_b),
        ln_g=row(ln_c_g), ln_b=row(ln_c_b), w_br=w_branch_r[l].astype(BF16), w_bc=w_branch_c[l].astype(BF16),
        w_out=w_out[l].astype(BF16), gpost=row(norm_post_g))


def _prompt_layer(x, p, tm_prep, tm_wkv, tm_post):
    T = x.shape[1]
    *pre, u, szc, sgr, sgc, h_last = _prep(x, None, p, tm_prep)
    o_r, wkv = _wkv_prompt(pre, p, tm_wkv)
    (y,) = _post(x, o_r, u, szc, sgr, sgc, None, p, tm_post)
    return y, h_last[:, SUBLANES - 1], wkv, u[:, T - (CONV_W - 1):]


def _sample_layer(x, shift, wkv, conv, p, nb_wkv, tm_post):
    ns = x.shape[0]
    xs = x.reshape(1, ns, D_MODEL)
    *pre, u, szc, sgr, sgc, h = _prep(xs, shift.reshape(1, ns, D_MODEL), p, ns)
    o_r, wkv_new = _wkv_sample(pre, wkv, p, nb_wkv)
    y, conv_new = _post(xs, o_r, u, szc, sgr, sgc, conv, p, tm_post)
    return y.reshape(ns, 1, D_MODEL), h.reshape(ns, D_MODEL), wkv_new, conv_new


def kernel(x_prompt, x_sample, state_shift, state_wkv, state_conv, norm_pre_g, w_in, mu_shift, decay_w0, decay_w2,
           iclr_a0, iclr_a2, k_k, k_a, r_k, gn_g, gn_b, conv_glu_b, conv_w, conv_b, ln_c_g, ln_c_b, w_branch_r,
           w_branch_c, w_out, norm_post_g):
    depth = w_in.shape[0]
    xp, xs = x_prompt, x_sample
    outs = [[] for _ in range(6)]
    for l in range(depth):
        p = _layer_params(l, norm_pre_g, w_in, mu_shift, decay_w0, decay_w2, iclr_a0, iclr_a2, k_k, k_a, r_k, gn_g,
                          gn_b, conv_glu_b, conv_w, conv_b, ln_c_g, ln_c_b, w_branch_r, w_branch_c, w_out,
                          norm_post_g)
        xp, a1, a2, a3 = _prompt_layer(xp, p, tm_prep=256, tm_wkv=256, tm_post=512)
        xs, b1, b2, b3 = _sample_layer(xs, state_shift[l], state_wkv[l], state_conv[l], p, nb_wkv=8, tm_post=32)
        for lst, val in zip(outs, (a1, a2, a3, b1, b2, b3)):
            lst.append(val)
    return (xp, xs) + tuple(jnp.stack(o) for o in outs)
```

```python
import functools
import math

import jax
import jax.numpy as jnp
from jax import lax
from jax.experimental import pallas as pl
from jax.experimental.pallas import tpu as pltpu

F32 = jnp.float32
BF16 = jnp.bfloat16

D_MODEL = 1024
HEAD_DIM = 64
N_HEADS = D_MODEL // HEAD_DIM
D_RWKV = N_HEADS * HEAD_DIM
D_CONV = D_MODEL // 2
CONV_W = 31
LORA = 64
RMS_EPS = 1e-6
GN_EPS = 64e-5
LN_EPS = 1e-5
N_SHIFT = 4 * D_RWKV + 2 * LORA
N_IN = N_SHIFT + 2 * D_CONV + D_CONV + 2 * D_MODEL

SUBLANES = 8
PAIR = 2 * HEAD_DIM
N_PAIRS = N_HEADS // 2
MXU_DIM = 256
CHUNK = 64
CONV_PAD = 32
VMEM_LIMIT = 56 * 1024 * 1024

_O_R, _O_K, _O_V, _O_ZR = 0, D_RWKV, 2 * D_RWKV, 3 * D_RWKV
_O_LORA = 4 * D_RWKV
_O_U = N_SHIFT
_O_ZC = _O_U + 2 * D_CONV
_O_GR = _O_ZC + D_CONV
_O_GC = _O_GR + D_MODEL


def _dot(a, b):
    return jnp.dot(a, b, preferred_element_type=F32)


def _dot_nt(a, b):
    return lax.dot_general(a, b, (((1,), (1,)), ((), ())), preferred_element_type=F32)


def _sigmoid(x):
    return 1.0 / (1.0 + jnp.exp(-x))


def _split_hi_lo(x):
    hi = x.astype(BF16)
    lo = (x - hi.astype(F32)).astype(BF16)
    return hi, lo


def _head_sum(x, ones_bd):
    outs = []
    for c in range(x.shape[1] // MXU_DIM):
        hi, lo = _split_hi_lo(x[:, c * MXU_DIM:(c + 1) * MXU_DIM])
        outs.append(_dot(hi, ones_bd) + _dot(lo, ones_bd))
    return jnp.concatenate(outs, axis=1)


def _group_norm_gate(y, bonus, szr, gng, gnb, ones_bd):
    mean = _head_sum(y, ones_bd) * (1.0 / HEAD_DIM)
    d = y - mean
    var = _head_sum(d * d, ones_bd) * (1.0 / HEAD_DIM)
    y_gn = d * lax.rsqrt(var + GN_EPS) * gng + gnb
    return ((y_gn + bonus) * szr).astype(BF16)


def _prep_kernel(sample, *refs):
    if sample:
        (x_ref, prev_ref, gpre_ref, win_ref, mu_ref, w0_ref, a0_ref, w2c_ref, kk_ref, ka_ref, rk_ref,
         glub_ref, ones_ref,
         r_out, k_out, v_out, a_out, b_out, lw_out, szr_out, bon_out, u_out, szc_out, sgr_out, sgc_out,
         h_out) = refs
    else:
        (x_ref, gpre_ref, win_ref, mu_ref, w0_ref, a0_ref, w2c_ref, kk_ref, ka_ref, rk_ref,
         glub_ref, ones_ref,
         r_out, k_out, v_out, a_out, b_out, lw_out, szr_out, bon_out, u_out, szc_out, sgr_out, sgc_out,
         h_out, carry_ref) = refs

    x = x_ref[0]
    tm = x.shape[0]
    ms = jnp.mean(x * x, axis=-1, keepdims=True)
    h = x * lax.rsqrt(ms + RMS_EPS) * gpre_ref[...]
    hb = h.astype(BF16)

    if sample:
        h_out[0] = h
        pb = prev_ref[0].astype(BF16)
    else:
        h_out[0] = h[tm - SUBLANES:, :]

        @pl.when(pl.program_id(1) == 0)
        def _():
            carry_ref[...] = jnp.zeros_like(carry_ref)
        row0 = lax.broadcasted_iota(jnp.int32, (tm, 1), 0) == 0

    def shifted(off, width):
        p = _dot(hb, win_ref[:, off:off + width])
        if sample:
            pp = _dot(pb, win_ref[:, off:off + width])
        else:
            rolled = pltpu.roll(p, shift=1, axis=0)
            pp = jnp.where(row0, carry_ref[0:1, off:off + width], rolled)
            carry_ref[0:1, off:off + width] = p[tm - 1:tm, :]
        return p + (pp - p) * mu_ref[:, off:off + width]

    ones_bd = ones_ref[...]

    lo = shifted(_O_LORA, 2 * LORA)
    lane = lax.broadcasted_iota(jnp.int32, lo.shape, 1)
    lo = jnp.where(lane < LORA, jnp.tanh(lo), lo)
    lor = _dot(lo.astype(BF16), w2c_ref[...])
    w_raw = w0_ref[...] + lor[:, :D_RWKV]
    lw_out[0] = (-math.exp(-0.5)) * _sigmoid(w_raw)
    iclr = _sigmoid(a0_ref[...] + lor[:, D_RWKV:])

    k = shifted(_O_K, D_RWKV)
    kk = k * kk_ref[...]
    kk = kk * lax.rsqrt(_head_sum(kk * kk, ones_bd) + 1e-12)
    k2 = k * (1.0 + (iclr - 1.0) * ka_ref[...])
    k_out[0] = k2
    a_out[0] = -kk
    b_out[0] = kk * iclr

    r = shifted(_O_R, D_RWKV)
    r_out[0] = r
    bsum = _head_sum(r * k2 * rk_ref[...], ones_bd)
    v = shifted(_O_V, D_RWKV)
    v_out[0] = v
    bon_out[0] = bsum * v

    zr = shifted(_O_ZR, D_RWKV)
    szr_out[0] = zr * _sigmoid(zr)

    u_in = _dot(hb, win_ref[:, _O_U:_O_U + 2 * D_CONV]) + glub_ref[...]
    u_out[0] = u_in[:, :D_CONV] * _sigmoid(u_in[:, D_CONV:])
    zc = _dot(hb, win_ref[:, _O_ZC:_O_ZC + D_CONV])
    szc_out[0] = zc * _sigmoid(zc)
    sgr_out[0] = _sigmoid(_dot(hb, win_ref[:, _O_GR:_O_GR + D_MODEL]))
    sgc_out[0] = _sigmoid(_dot(hb, win_ref[:, _O_GC:_O_GC + D_MODEL]))


def _const_spec(shape):
    nd = len(shape)
    return pl.BlockSpec(shape, lambda *_: (0,) * nd, pipeline_mode=pl.Buffered(1))


def _prep(x, prev, p, tm):
    B, T, _ = x.shape
    sample = prev is not None
    tile = lambda w: pl.BlockSpec((1, tm, w), lambda b, j: (b, j, 0))
    in_specs = [tile(D_MODEL)] + ([tile(D_MODEL)] if sample else []) + [
        _const_spec((1, D_MODEL)), _const_spec((D_MODEL, N_IN)), _const_spec((1, N_SHIFT)),
        _const_spec((1, D_RWKV)), _const_spec((1, D_RWKV)), _const_spec((2 * LORA, 2 * D_RWKV)),
        _const_spec((1, D_RWKV)), _const_spec((1, D_RWKV)), _const_spec((1, D_RWKV)),
        _const_spec((1, 2 * D_CONV)), _const_spec((MXU_DIM, MXU_DIM))]
    widths = [D_RWKV] * 8 + [D_CONV, D_CONV, D_MODEL, D_MODEL]
    h_spec =(tile(D_MODEL) if sample else
              pl.BlockSpec((1, SUBLANES, D_MODEL), lambda b, j: (b, 0, 0)))
    out_specs = [tile(w) for w in widths] + [h_spec]
    out_shape = [jax.ShapeDtypeStruct((B, T, w), F32) for w in widths] + [
        jax.ShapeDtypeStruct((B, T if sample else SUBLANES, D_MODEL), F32)]
    args = [x] + ([prev] if sample else []) + [
        p["gpre"], p["w_in"], p["mu"], p["w0"], p["a0"], p["w2c"], p["k_k"], p["k_a"], p["r_k"],
        p["glu_b"], p["ones_bd"]]
    return pl.pallas_call(
        functools.partial(_prep_kernel, sample),
        grid=(B, T // tm),
        in_specs=in_specs,
        out_specs=out_specs,
        out_shape=out_shape,
        scratch_shapes=[] if sample else [pltpu.VMEM((SUBLANES, N_SHIFT), F32)],
        compiler_params=pltpu.CompilerParams(
            dimension_semantics=("arbitrary", "arbitrary"), vmem_limit_bytes=VMEM_LIMIT),
        name="prep_sample" if sample else "prep_prompt",
    )(*args)


def _block_diag(x, same_head):
    return jnp.where(same_head, jnp.concatenate([x, x], axis=0), jnp.zeros((), x.dtype))


def _wkv_chunk_kernel(r_ref, k_ref, v_ref, a_ref, b_ref, lw_ref, szr_ref, bon_ref, gng_ref, gnb_ref, ones_ref,
                      o_ref, s_out, s_ref):
    tm = r_ref.shape[1]
    j = pl.program_id(1)

    @pl.when(j == 0)
    def _():
        s_ref[...] = jnp.zeros_like(s_ref)

    row = lax.broadcasted_iota(jnp.int32, (CHUNK, PAIR), 0)
    col = lax.broadcasted_iota(jnp.int32, (CHUNK, PAIR), 1) % HEAD_DIM
    strict = row > col
    incl = row >= col
    eye_pair = (row == col).astype(F32)
    brow = lax.broadcasted_iota(jnp.int32, (PAIR, PAIR), 0)
    bcol = lax.broadcasted_iota(jnp.int32, (PAIR, PAIR), 1)
    same_head = (brow // HEAD_DIM) == (bcol // HEAD_DIM)
    eye_full = brow == bcol
    first_head = lax.broadcasted_iota(jnp.int32, (CHUNK, PAIR), 1) < HEAD_DIM
    trow = lax.broadcasted_iota(jnp.int32, (CHUNK, CHUNK), 0)
    tcol = lax.broadcasted_iota(jnp.int32, (CHUNK, CHUNK), 1)
    tri = (trow >= tcol).astype(BF16)
    ones_bd = ones_ref[...]
    bd = lambda x: _block_diag(x.astype(BF16), same_head)

    def chunk_body(c, carry):
        rows = pl.ds(pl.multiple_of(c * CHUNK, CHUNK), CHUNK)
        lw = lw_ref[0, rows, :]
        l1 = lw.astype(BF16)
        rem = lw - l1.astype(F32)
        l2 = rem.astype(BF16)
        l3 = (rem - l2.astype(F32)).astype(BF16)
        cl = _dot(tri, l1) + _dot(tri, l2) + _dot(tri, l3)
        e_neg = jnp.exp(-cl)
        w_end = jnp.exp(cl[CHUNK - 1:CHUNK, :])
        r_t = r_ref[0, rows, :] * jnp.exp(cl)
        a_t = a_ref[0, rows, :] * jnp.exp(cl - lw)
        b_t = b_ref[0, rows, :] * e_neg
        k_t = k_ref[0, rows, :] * e_neg
        b_e = b_t * w_end
        k_e = k_t * w_end
        v_all = v_ref[0, rows, :]

        pairs = range(N_PAIRS)
        sls = [slice(p * PAIR, (p + 1) * PAIR) for p in pairs]
        at = [a_t[:, sl] for sl in sls]
        rt = [r_t[:, sl] for sl in sls]
        be = [b_e[:, sl] for sl in sls]
        ke = [k_e[:, sl] for sl in sls]
        vp = [v_all[:, sl] for sl in sls]
        g = [_dot_nt(jnp.concatenate([at[p], rt[p]], axis=0).astype(BF16),
                     jnp.concatenate([bd(b_t[:, sls[p]]), bd(k_t[:, sls[p]])], axis=0))
             for p in pairs]
        l_ab = [jnp.where(strict, g[p][:CHUNK, :PAIR], 0.0) for p in pairs]
        m_rb = [jnp.where(incl, g[p][CHUNK:, :PAIR], 0.0).astype(BF16) for p in pairs]
        m_rk = [jnp.where(incl, g[p][CHUNK:, PAIR:], 0.0).astype(BF16) for p in pairs]
        bd_v = [bd(vp[p]) for p in pairs]
        uv0 = [_dot(jnp.where(strict, g[p][:CHUNK, PAIR:], 0.0).astype(BF16), bd_v[p]) for p in pairs]
        t_acc = [eye_pair + l_ab[p] for p in pairs]
        lp = [_dot(l_ab[p].astype(BF16), bd(l_ab[p])) for p in pairs]
        for _ in range(4):
            prod = [_dot(lp[p].astype(BF16), jnp.concatenate([bd(lp[p]), bd(t_acc[p])], axis=1)) for p in pairs]
            lp = [prod[p][:, :PAIR] for p in pairs]
            t_acc = [t_acc[p] + prod[p][:, PAIR:] for p in pairs]
        t_acc = [t_acc[p] + _dot(lp[p].astype(BF16), bd(t_acc[p])) for p in pairs]
        au = [_dot(t_acc[p].astype(BF16), jnp.concatenate([bd(at[p]), bd(uv0[p])], axis=1)) for p in pairs]
        a_hat = [au[p][:, :PAIR] for p in pairs]
        u_v = [au[p][:, PAIR:] for p in pairs]
        ry = [_dot(m_rb[p], jnp.concatenate([bd(a_hat[p]), bd(u_v[p])], axis=1)) for p in pairs]
        yk = [_dot(m_rk[p], bd_v[p]) for p in pairs]
        phi = [_dot(a_hat[p].T.astype(BF16), be[p].astype(BF16)) for p in pairs]
        f = [_dot(jnp.concatenate([u_v[p], vp[p]], axis=0).T.astype(BF16),
                  jnp.concatenate([be[p], ke[p]], axis=0).astype(BF16)) for p in pairs]
        s0 = [s_ref[p] for p in pairs]
        ys = [_dot_nt((rt[p] + ry[p][:, :PAIR]).astype(BF16), bd(s0[p])) + ry[p][:, PAIR:] + yk[p]
              for p in pairs]
        for p in pairs:
            phi_p = jnp.where(same_head, phi[p], 0.0) + jnp.where(eye_full, w_end[:, sls[p]], 0.0)
            z = jnp.where(first_head, f[p][:HEAD_DIM, :], f[p][HEAD_DIM:, :])
            s_ref[p] = _dot(s0[p].astype(BF16), phi_p.astype(BF16)) + z

        y = jnp.concatenate(ys, axis=1)
        o_ref[0, rows, :] = _group_norm_gate(y, bon_ref[0, rows, :], szr_ref[0, rows, :],
                                             gng_ref[...], gnb_ref[...], ones_bd)
        return carry

    lax.fori_loop(0, tm // CHUNK, chunk_body, 0)

    @pl.when(j == pl.num_programs(1) - 1)
    def _():
        for hd in range(N_HEADS):
            lo = (hd % 2) * HEAD_DIM
            s_out[0, hd] = s_ref[hd // 2][:, lo:lo + HEAD_DIM]


def _wkv_prompt(pre, p, tm):
    r, k, v, a, b, lw, szr, bon = pre
    B, T, _ = r.shape
    tile = pl.BlockSpec((1, tm, D_RWKV), lambda bb, j: (bb, j, 0))
    return pl.pallas_call(
        _wkv_chunk_kernel,
        grid=(B, T // tm),
        in_specs=[tile] * 8 + [_const_spec((1, D_RWKV)), _const_spec((1, D_RWKV)),
                               _const_spec((MXU_DIM, MXU_DIM))],
        out_specs=[tile, pl.BlockSpec((1, N_HEADS, HEAD_DIM, HEAD_DIM), lambda bb, j: (bb, 0, 0, 0))],
        out_shape=[jax.ShapeDtypeStruct((B, T, D_RWKV), BF16),
                   jax.ShapeDtypeStruct((B, N_HEADS, HEAD_DIM, HEAD_DIM), F32)],
        scratch_shapes=[pltpu.VMEM((N_PAIRS, HEAD_DIM, PAIR), F32)],
        compiler_params=pltpu.CompilerParams(
            dimension_semantics=("arbitrary", "arbitrary"), vmem_limit_bytes=VMEM_LIMIT),
        name="wkv_prompt",
    )(r, k, v, a, b, lw, szr, bon, p["gn_g"], p["gn_b"], p["ones_bd"])


def _wkv_step_kernel(r_ref, k_ref, v_ref, a_ref, b_ref, lw_ref, szr_ref, bon_ref, gng_ref, gnb_ref,
                     s_in, o_ref, s_out, y_ref):
    nb = s_in.shape[0]
    eye_b = (lax.broadcasted_iota(jnp.int32, (HEAD_DIM, HEAD_DIM), 0)
             == lax.broadcasted_iota(jnp.int32, (HEAD_DIM, HEAD_DIM), 1)).astype(BF16)
    zero4 = jnp.zeros((SUBLANES - 4, HEAD_DIM), BF16)
    zero6 = jnp.zeros((SUBLANES - 2, HEAD_DIM), BF16)

    def seq_body(i, carry):
        a_all, b_all, k_all, v_all, r_all = (t[i] for t in (a_ref, b_ref, k_ref, v_ref, r_ref))
        w_all = jnp.exp(lw_ref[i])
        heads = range(N_HEADS)
        rows = [slice(hd, hd + 1) for hd in heads]
        s0 = [s_in[i, hd] for hd in heads]
        cols = []
        for hd in heads:
            a_hi, a_lo = _split_hi_lo(a_all[rows[hd]])
            v_hi, v_lo = _split_hi_lo(v_all[rows[hd]])
            q = jnp.concatenate([a_hi, a_lo, v_hi, v_lo, zero4], axis=0)
            s_hi, s_lo = _split_hi_lo(s0[hd])
            cols.append(_dot_nt(jnp.concatenate([s_hi, s_lo, eye_b], axis=0), q))
        s1 = []
        for hd in heads:
            c = cols[hd]
            sa = c[:HEAD_DIM, 0:1] + c[:HEAD_DIM, 1:2] + c[HEAD_DIM:2 * HEAD_DIM, 0:1]
            vcol = c[2 * HEAD_DIM:, 2:3] + c[2 * HEAD_DIM:, 3:4]
            s1.append(s0[hd] * w_all[rows[hd]] + sa * b_all[rows[hd]] + vcol * k_all[rows[hd]])
            s_out[i, hd] = s1[hd]
        yy, yl = [], []
        for hd in heads:
            n_hi, n_lo = _split_hi_lo(s1[hd])
            r_hi, r_lo = _split_hi_lo(r_all[rows[hd]])
            rq = jnp.concatenate([r_hi, r_lo, zero6], axis=0)
            yy.append(_dot_nt(rq, n_hi))
            yl.append(_dot_nt(rq, n_lo))
        for hd in heads:
            y_ref[rows[hd], :] = yy[hd][0:1, :] + yy[hd][1:2, :] + yl[hd][0:1, :]
        y = y_ref[...]
        mean = jnp.mean(y, axis=-1, keepdims=True)
        d = y - mean
        var = jnp.mean(d * d, axis=-1, keepdims=True)
        y_gn = d * lax.rsqrt(var + GN_EPS) * gng_ref[...] + gnb_ref[...]
        o_ref[i] = ((y_gn + bon_ref[i]) * szr_ref[i]).astype(BF16)
        return carry

    lax.fori_loop(0, nb, seq_body, 0)


def _wkv_sample(pre, state, p, nb):
    ns = state.shape[0]
    pre = [t.reshape(ns, N_HEADS, HEAD_DIM) for t in pre]
    vec = pl.BlockSpec((nb, N_HEADS, HEAD_DIM), lambda i: (i, 0, 0))
    st = pl.BlockSpec((nb, N_HEADS, HEAD_DIM, HEAD_DIM), lambda i: (i, 0, 0, 0))
    o_r, s_new = pl.pallas_call(
        _wkv_step_kernel,
        grid=(ns // nb,),
        in_specs=[vec] * 8 + [_const_spec((N_HEADS, HEAD_DIM)), _const_spec((N_HEADS, HEAD_DIM)), st],
        out_specs=[vec, st],
        out_shape=[jax.ShapeDtypeStruct((ns, N_HEADS, HEAD_DIM), BF16),
                   jax.ShapeDtypeStruct(state.shape, F32)],
        scratch_shapes=[pltpu.VMEM((N_HEADS, HEAD_DIM), F32)],
        compiler_params=pltpu.CompilerParams(
            dimension_semantics=("arbitrary",), vmem_limit_bytes=VMEM_LIMIT),
        name="wkv_sample",
    )(*pre, p["gn_g"].reshape(N_HEADS, HEAD_DIM), p["gn_b"].reshape(N_HEADS, HEAD_DIM), state)
    return o_r.reshape(1, ns, D_RWKV), s_new


def _post_kernel(sample, *refs):
    if sample:
        (x_ref, or_ref, u_ref, szc_ref, sgr_ref, sgc_ref, cw_ref, cb_ref, lng_ref, lnb_ref,
         wbr_ref, wbc_ref, wout_ref, gpost_ref, st_ref, y_ref, st_out) = refs
    else:
        (x_ref, or_ref, u_ref, szc_ref, sgr_ref, sgc_ref, cw_ref, cb_ref, lng_ref, lnb_ref,
         wbr_ref, wbc_ref, wout_ref, gpost_ref, y_ref, ubuf) = refs

    u = u_ref[0]
    tm = u.shape[0]
    acc = jnp.broadcast_to(cb_ref[...], u.shape)
    if sample:
        for kk in range(CONV_W - 1):
            acc = acc + cw_ref[kk:kk + 1, :] * st_ref[:, kk, :]
        acc = acc + cw_ref[CONV_W - 1:CONV_W, :] * u
        st_out[:, 0:CONV_W - 2, :] = st_ref[:, 1:CONV_W - 1, :]
        st_out[:, CONV_W - 2, :] = u
    else:
        @pl.when(pl.program_id(1) == 0)
        def _():
            ubuf[0:CONV_PAD, :] = jnp.zeros((CONV_PAD, D_CONV), F32)
        ubuf[CONV_PAD:CONV_PAD + tm, :] = u
        base = CONV_PAD - (CONV_W - 1)
        for kk in range(CONV_W):
            acc = acc + cw_ref[kk:kk + 1, :] * ubuf[base + kk:base + kk + tm, :]
        ubuf[0:CONV_PAD, :] = ubuf[tm:tm + CONV_PAD, :]

    cm = jnp.mean(acc, axis=-1, keepdims=True)
    d = acc - cm
    cv = jnp.mean(d * d, axis=-1, keepdims=True)
    cn = d * lax.rsqrt(cv + LN_EPS) * lng_ref[...] + lnb_ref[...]
    o_c = (cn * _sigmoid(cn)) * szc_ref[0]
    branch_c = _dot(o_c.astype(BF16), wbc_ref[...])
    branch_r = _dot(or_ref[0], wbr_ref[...])
    merged = sgr_ref[0] * branch_r + sgc_ref[0] * branch_c
    out = _dot(merged.astype(BF16), wout_ref[...])
    ms = jnp.mean(out * out, axis=-1, keepdims=True)
    y_ref[0] = x_ref[0] + out * lax.rsqrt(ms + RMS_EPS) * gpost_ref[...]


def _post(x, o_r, u, szc, sgr, sgc, conv_state, p, tm):
    B, T, _ = x.shape
    sample = conv_state is not None
    tile = lambda w: pl.BlockSpec((1, tm, w), lambda b, j: (b, j, 0))
    in_specs = [tile(D_MODEL), tile(D_RWKV), tile(D_CONV), tile(D_CONV), tile(D_MODEL), tile(D_MODEL),
                _const_spec((CONV_W, D_CONV)), _const_spec((1, D_CONV)), _const_spec((1, D_CONV)),
                _const_spec((1, D_CONV)), _const_spec((D_RWKV, D_MODEL)), _const_spec((D_CONV, D_MODEL)),
                _const_spec((D_MODEL, D_MODEL)), _const_spec((1, D_MODEL))]
    args = [x, o_r, u, szc, sgr, sgc, p["conv_w"], p["conv_b"], p["ln_g"], p["ln_b"],
            p["w_br"], p["w_bc"], p["w_out"], p["gpost"]]
    out_specs = [tile(D_MODEL)]
    out_shape = [jax.ShapeDtypeStruct((B, T, D_MODEL), F32)]
    scratch = []
    if sample:
        st = pl.BlockSpec((tm, CONV_W - 1, D_CONV), lambda b, j: (j, 0, 0))
        in_specs.append(st)
        args.append(conv_state)
        out_specs.append(st)
        out_shape.append(jax.ShapeDtypeStruct(conv_state.shape, F32))
    else:
        scratch = [pltpu.VMEM((tm + CONV_PAD, D_CONV), F32)]
    return pl.pallas_call(
        functools.partial(_post_kernel, sample),
        grid=(B, T // tm),
        in_specs=in_specs,
        out_specs=out_specs,
        out_shape=out_shape,
        scratch_shapes=scratch,
        compiler_params=pltpu.CompilerParams(
            dimension_semantics=("arbitrary", "arbitrary"), vmem_limit_bytes=VMEM_LIMIT),
        name="post_sample" if sample else "post_prompt",
    )(*args)


def _layer_params(l, norm_pre_g, w_in, mu_shift, decay_w0, decay_w2, iclr_a0, iclr_a2, k_k, k_a, r_k, gn_g, gn_b,
                  conv_glu_b, conv_w, conv_b, ln_c_g, ln_c_b, w_branch_r, w_branch_c, w_out, norm_post_g):
    row = lambda t: t[l].reshape(1, -1).astype(F32)
    zeros = jnp.zeros((LORA, D_RWKV), F32)
    w2c = jnp.concatenate([jnp.concatenate([decay_w2[l], zeros], axis=1),
                           jnp.concatenate([zeros, iclr_a2[l]], axis=1)], axis=0)
    blk = jnp.arange(MXU_DIM) // HEAD_DIM
    return dict(
        gpre=row(norm_pre_g), w_in=w_in[l].astype(BF16), mu=row(mu_shift), w0=row(decay_w0), a0=row(iclr_a0),
        w2c=w2c.astype(BF16), k_k=row(k_k), k_a=row(k_a), r_k=row(r_k), glu_b=row(conv_glu_b),
        ones_bd=(blk[:, None] == blk[None, :]).astype(BF16),
        gn_g=row(gn_g), gn_b=row(gn_b), conv_w=conv_w[l].astype(F32), conv_b=row(conv_b),
        ln_g=row(ln_c_g), ln_b=row(ln_c_b), w_br=w_branch_r[l].astype(BF16), w_bc=w_branch_c[l].astype(BF16),
        w_out=w_out[l].astype(BF16), gpost=row(norm_post_g))


def _prompt_layer(x, p, tm_prep, tm_wkv, tm_post):
    T = x.shape[1]
    *pre, u, szc, sgr, sgc, h_last = _prep(x, None, p, tm_prep)
    o_r, wkv = _wkv_prompt(pre, p, tm_wkv)
    (y,) = _post(x, o_r, u, szc, sgr, sgc, None, p, tm_post)
    return y, h_last[:, SUBLANES - 1], wkv, u[:, T - (CONV_W - 1):]


def _sample_layer(x, shift, wkv, conv, p, nb_wkv, tm_post):
    ns = x.shape[0]
    xs = x.reshape(1, ns, D_MODEL)
    *pre, u, szc, sgr, sgc, h = _prep(xs, shift.reshape(1, ns, D_MODEL), p, ns)
    o_r, wkv_new = _wkv_sample(pre, wkv, p, nb_wkv)
    y, conv_new = _post(xs, o_r, u, szc, sgr, sgc, conv, p, tm_post)
    return y.reshape(ns, 1, D_MODEL), h.reshape(ns, D_MODEL), wkv_new, conv_new


def kernel(x_prompt, x_sample, state_shift, state_wkv, state_conv, norm_pre_g, w_in, mu_shift, decay_w0, decay_w2,
           iclr_a0, iclr_a2, k_k, k_a, r_k, gn_g, gn_b, conv_glu_b, conv_w, conv_b, ln_c_g, ln_c_b, w_branch_r,
           w_branch_c, w_out, norm_post_g):
    depth = w_in.shape[0]
    xp, xs = x_prompt, x_sample
    outs = [[] for _ in range(6)]
    for l in range(depth):
        p = _layer_params(l, norm_pre_g, w_in, mu_shift, decay_w0, decay_w2, iclr_a0, iclr_a2, k_k, k_a, r_k, gn_g,
                          gn_b, conv_glu_b, conv_w, conv_b, ln_c_g, ln_c_b, w_branch_r, w_branch_c, w_out,
                          norm_post_g)
        xp, a1, a2, a3 = _prompt_layer(xp, p, tm_prep=256, tm_wkv=256, tm_post=512)
        xs, b1, b2, b3 = _sample_layer(xs, state_shift[l], state_wkv[l], state_conv[l], p, nb_wkv=8, tm_post=32)
        for lst, val in zip(outs, (a1, a2, a3, b1, b2, b3)):
            lst.append(val)
    return (xp, xs) + tuple(jnp.stack(o) for o in outs)
```

```python
import functools
import math

import jax
import jax.numpy as jnp
from jax import lax
from jax.experimental import pallas as pl
from jax.experimental.pallas import tpu as pltpu

F32 = jnp.float32
BF16 = jnp.bfloat16

D_MODEL = 1024
HEAD_DIM = 64
N_HEADS = D_MODEL // HEAD_DIM
D_RWKV = N_HEADS * HEAD_DIM
D_CONV = D_MODEL // 2
CONV_W = 31
LORA = 64
RMS_EPS = 1e-6
GN_EPS = 64e-5
LN_EPS = 1e-5
N_SHIFT = 4 * D_RWKV + 2 * LORA
N_IN = N_SHIFT + 2 * D_CONV + D_CONV + 2 * D_MODEL

SUBLANES = 8
PAIR = 2 * HEAD_DIM
N_PAIRS = N_HEADS // 2
MXU_DIM = 256
CHUNK = 64
CONV_PAD = 32
VMEM_LIMIT = 56 * 1024 * 1024

_O_R, _O_K, _O_V, _O_ZR = 0, D_RWKV, 2 * D_RWKV, 3 * D_RWKV
_O_LORA = 4 * D_RWKV
_O_U = N_SHIFT
_O_ZC = _O_U + 2 * D_CONV
_O_GR = _O_ZC + D_CONV
_O_GC = _O_GR + D_MODEL


def _dot(a, b):
    return jnp.dot(a, b, preferred_element_type=F32)


def _dot_nt(a, b):
    return lax.dot_general(a, b, (((1,), (1,)), ((), ())), preferred_element_type=F32)


def _sigmoid(x):
    return 1.0 / (1.0 + jnp.exp(-x))


def _split_hi_lo(x):
    hi = x.astype(BF16)
    lo = (x - hi.astype(F32)).astype(BF16)
    return hi, lo


def _head_sum(x, ones_bd):
    outs = []
    for c in range(x.shape[1] // MXU_DIM):
        hi, lo = _split_hi_lo(x[:, c * MXU_DIM:(c + 1) * MXU_DIM])
        outs.append(_dot(hi, ones_bd) + _dot(lo, ones_bd))
    return jnp.concatenate(outs, axis=1)


def _group_norm_gate(y, bonus, szr, gng, gnb, ones_bd):
    mean = _head_sum(y, ones_bd) * (1.0 / HEAD_DIM)
    d = y - mean
    var = _head_sum(d * d, ones_bd) * (1.0 / HEAD_DIM)
    y_gn = d * lax.rsqrt(var + GN_EPS) * gng + gnb
    return ((y_gn + bonus) * szr).astype(BF16)


def _prep_kernel(sample, *refs):
    if sample:
        (x_ref, prev_ref, gpre_ref, win_ref, mu_ref, w0_ref, a0_ref, w2c_ref, kk_ref, ka_ref, rk_ref,
         glub_ref, ones_ref,
         r_out, k_out, v_out, a_out, b_out, lw_out, szr_out, bon_out, u_out, szc_out, sgr_out, sgc_out,
         h_out) = refs
    else:
        (x_ref, gpre_ref, win_ref, mu_ref, w0_ref, a0_ref, w2c_ref, kk_ref, ka_ref, rk_ref,
         glub_ref, ones_ref,
         r_out, k_out, v_out, a_out, b_out, lw_out, szr_out, bon_out, u_out, szc_out, sgr_out, sgc_out,
         h_out, carry_ref) = refs

    x = x_ref[0]
    tm = x.shape[0]
    ms = jnp.mean(x * x, axis=-1, keepdims=True)
    h = x * lax.rsqrt(ms + RMS_EPS) * gpre_ref[...]
    hb = h.astype(BF16)

    if sample:
        h_out[0] = h
        pb = prev_ref[0].astype(BF16)
    else:
        h_out[0] = h[tm - SUBLANES:, :]

        @pl.when(pl.program_id(1) == 0)
        def _():
            carry_ref[...] = jnp.zeros_like(carry_ref)
        row0 = lax.broadcasted_iota(jnp.int32, (tm, 1), 0) == 0

    def shifted(off, width):
        p = _dot(hb, win_ref[:, off:off + width])
        if sample:
            pp = _dot(pb, win_ref[:, off:off + width])
        else:
            rolled = pltpu.roll(p, shift=1, axis=0)
            pp = jnp.where(row0, carry_ref[0:1, off:off + width], rolled)
            carry_ref[0:1, off:off + width] = p[tm - 1:tm, :]
        return p + (pp - p) * mu_ref[:, off:off + width]

    ones_bd = ones_ref[...]

    lo = shifted(_O_LORA, 2 * LORA)
    lane = lax.broadcasted_iota(jnp.int32, lo.shape, 1)
    lo = jnp.where(lane < LORA, jnp.tanh(lo), lo)
    lor = _dot(lo.astype(BF16), w2c_ref[...])
    w_raw = w0_ref[...] + lor[:, :D_RWKV]
    lw_out[0] = (-math.exp(-0.5)) * _sigmoid(w_raw)
    iclr = _sigmoid(a0_ref[...] + lor[:, D_RWKV:])

    k = shifted(_O_K, D_RWKV)
    kk = k * kk_ref[...]
    kk = kk * lax.rsqrt(_head_sum(kk * kk, ones_bd) + 1e-12)
    k2 = k * (1.0 + (iclr - 1.0) * ka_ref[...])
    k_out[0] = k2
    a_out[0] = -kk
    b_out[0] = kk * iclr

    r = shifted(_O_R, D_RWKV)
    r_out[0] = r
    bsum = _head_sum(r * k2 * rk_ref[...], ones_bd)
    v = shifted(_O_V, D_RWKV)
    v_out[0] = v
    bon_out[0] = bsum * v

    zr = shifted(_O_ZR, D_RWKV)
    szr_out[0] = zr * _sigmoid(zr)

    u_in = _dot(hb, win_ref[:, _O_U:_O_U + 2 * D_CONV]) + glub_ref[...]
    u_out[0] = u_in[:, :D_CONV] * _sigmoid(u_in[:, D_CONV:])
    zc = _dot(hb, win_ref[:, _O_ZC:_O_ZC + D_CONV])
    szc_out[0] = zc * _sigmoid(zc)
    sgr_out[0] = _sigmoid(_dot(hb, win_ref[:, _O_GR:_O_GR + D_MODEL]))
    sgc_out[0] = _sigmoid(_dot(hb, win_ref[:, _O_GC:_O_GC + D_MODEL]))


def _const_spec(shape):
    nd = len(shape)
    return pl.BlockSpec(shape, lambda *_: (0,) * nd, pipeline_mode=pl.Buffered(1))


def _prep(x, prev, p, tm):
    B, T, _ = x.shape
    sample = prev is not None
    tile = lambda w: pl.BlockSpec((1, tm, w), lambda b, j: (b, j, 0))
    in_specs = [tile(D_MODEL)] + ([tile(D_MODEL)] if sample else []) + [
        _const_spec((1, D_MODEL)), _const_spec((D_MODEL, N_IN)), _const_spec((1, N_SHIFT)),
        _const_spec((1, D_RWKV)), _const_spec((1, D_RWKV)), _const_spec((2 * LORA, 2 * D_RWKV)),
        _const_spec((1, D_RWKV)), _const_spec((1, D_RWKV)), _const_spec((1, D_RWKV)),
        _const_spec((1, 2 * D_CONV)), _const_spec((MXU_DIM, MXU_DIM))]
    widths = [D_RWKV] * 8 + [D_CONV, D_CONV, D_MODEL, D_MODEL]
    h_spec =(tile(D_MODEL) if sample else
              pl.BlockSpec((1, SUBLANES, D_MODEL), lambda b, j: (b, 0, 0)))
    out_specs = [tile(w) for w in widths] + [h_spec]
    out_shape = [jax.ShapeDtypeStruct((B, T, w), F32) for w in widths] + [
        jax.ShapeDtypeStruct((B, T if sample else SUBLANES, D_MODEL), F32)]
    args = [x] + ([prev] if sample else []) + [
        p["gpre"], p["w_in"], p["mu"], p["w0"], p["a0"], p["w2c"], p["k_k"], p["k_a"], p["r_k"],
        p["glu_b"], p["ones_bd"]]
    return pl.pallas_call(
        functools.partial(_prep_kernel, sample),
        grid=(B, T // tm),
        in_specs=in_specs,
        out_specs=out_specs,
        out_shape=out_shape,
        scratch_shapes=[] if sample else [pltpu.VMEM((SUBLANES, N_SHIFT), F32)],
        compiler_params=pltpu.CompilerParams(
            dimension_semantics=("arbitrary", "arbitrary"), vmem_limit_bytes=VMEM_LIMIT),
        name="prep_sample" if sample else "prep_prompt",
    )(*args)


def _block_diag(x, same_head):
    return jnp.where(same_head, jnp.concatenate([x, x], axis=0), jnp.zeros((), x.dtype))


def _wkv_chunk_kernel(r_ref, k_ref, v_ref, a_ref, b_ref, lw_ref, szr_ref, bon_ref, gng_ref, gnb_ref, ones_ref,
                      o_ref, s_out, s_ref):
    tm = r_ref.shape[1]
    j = pl.program_id(1)

    @pl.when(j == 0)
    def _():
        s_ref[...] = jnp.zeros_like(s_ref)

    row = lax.broadcasted_iota(jnp.int32, (CHUNK, PAIR), 0)
    col = lax.broadcasted_iota(jnp.int32, (CHUNK, PAIR), 1) % HEAD_DIM
    strict = row > col
    incl = row >= col
    eye_pair = (row == col).astype(F32)
    brow = lax.broadcasted_iota(jnp.int32, (PAIR, PAIR), 0)
    bcol = lax.broadcasted_iota(jnp.int32, (PAIR, PAIR), 1)
    same_head = (brow // HEAD_DIM) == (bcol // HEAD_DIM)
    eye_full = brow == bcol
    first_head = lax.broadcasted_iota(jnp.int32, (CHUNK, PAIR), 1) < HEAD_DIM
    trow = lax.broadcasted_iota(jnp.int32, (CHUNK, CHUNK), 0)
    tcol = lax.broadcasted_iota(jnp.int32, (CHUNK, CHUNK), 1)
    tri = (trow >= tcol).astype(BF16)
    ones_bd = ones_ref[...]
    bd = lambda x: _block_diag(x.astype(BF16), same_head)

    def chunk_body(c, carry):
        rows = pl.ds(pl.multiple_of(c * CHUNK, CHUNK), CHUNK)
        lw = lw_ref[0, rows, :]
        l1 = lw.astype(BF16)
        rem = lw - l1.astype(F32)
        l2 = rem.astype(BF16)
        l3 = (rem - l2.astype(F32)).astype(BF16)
        cl = _dot(tri, l1) + _dot(tri, l2) + _dot(tri, l3)
        e_neg = jnp.exp(-cl)
        w_end = jnp.exp(cl[CHUNK - 1:CHUNK, :])
        r_t = r_ref[0, rows, :] * jnp.exp(cl)
        a_t = a_ref[0, rows, :] * jnp.exp(cl - lw)
        b_t = b_ref[0, rows, :] * e_neg
        k_t = k_ref[0, rows, :] * e_neg
        b_e = b_t * w_end
        k_e = k_t * w_end
        v_all = v_ref[0, rows, :]

        pairs = range(N_PAIRS)
        sls = [slice(p * PAIR, (p + 1) * PAIR) for p in pairs]
        at = [a_t[:, sl] for sl in sls]
        rt = [r_t[:, sl] for sl in sls]
        be = [b_e[:, sl] for sl in sls]
        ke = [k_e[:, sl] for sl in sls]
        vp = [v_all[:, sl] for sl in sls]
        g = [_dot_nt(jnp.concatenate([at[p], rt[p]], axis=0).astype(BF16),
                     jnp.concatenate([bd(b_t[:, sls[p]]), bd(k_t[:, sls[p]])], axis=0))
             for p in pairs]
        l_ab = [jnp.where(strict, g[p][:CHUNK, :PAIR], 0.0) for p in pairs]
        m_rb = [jnp.where(incl, g[p][CHUNK:, :PAIR], 0.0).astype(BF16) for p in pairs]
        m_rk = [jnp.where(incl, g[p][CHUNK:, PAIR:], 0.0).astype(BF16) for p in pairs]
        bd_v = [bd(vp[p]) for p in pairs]
        uv0 = [_dot(jnp.where(strict, g[p][:CHUNK, PAIR:], 0.0).astype(BF16), bd_v[p]) for p in pairs]
        t_acc = [eye_pair + l_ab[p] for p in pairs]
        lp = [_dot(l_ab[p].astype(BF16), bd(l_ab[p])) for p in pairs]
        for _ in range(4):
            prod = [_dot(lp[p].astype(BF16), jnp.concatenate([bd(lp[p]), bd(t_acc[p])], axis=1)) for p in pairs]
            lp = [prod[p][:, :PAIR] for p in pairs]
            t_acc = [t_acc[p] + prod[p][:, PAIR:] for p in pairs]
        t_acc = [t_acc[p] + _dot(lp[p].astype(BF16), bd(t_acc[p])) for p in pairs]
        au = [_dot(t_acc[p].astype(BF16), jnp.concatenate([bd(at[p]), bd(uv0[p])], axis=1)) for p in pairs]
        a_hat = [au[p][:, :PAIR] for p in pairs]
        u_v = [au[p][:, PAIR:] for p in pairs]
        ry = [_dot(m_rb[p], jnp.concatenate([bd(a_hat[p]), bd(u_v[p])], axis=1)) for p in pairs]
        yk = [_dot(m_rk[p], bd_v[p]) for p in pairs]
        phi = [_dot(a_hat[p].T.astype(BF16), be[p].astype(BF16)) for p in pairs]
        f = [_dot(jnp.concatenate([u_v[p], vp[p]], axis=0).T.astype(BF16),
                  jnp.concatenate([be[p], ke[p]], axis=0).astype(BF16)) for p in pairs]
        s0 = [s_ref[p] for p in pairs]
        ys = [_dot_nt((rt[p] + ry[p][:, :PAIR]).astype(BF16), bd(s0[p])) + ry[p][:, PAIR:] + yk[p]
              for p in pairs]
        for p in pairs:
            phi_p = jnp.where(same_head, phi[p], 0.0) + jnp.where(eye_full, w_end[:, sls[p]], 0.0)
            z = jnp.where(first_head, f[p][:HEAD_DIM, :], f[p][HEAD_DIM:, :])
            s_ref[p] = _dot(s0[p].astype(BF16), phi_p.astype(BF16)) + z

        y = jnp.concatenate(ys, axis=1)
        o_ref[0, rows, :] = _group_norm_gate(y, bon_ref[0, rows, :], szr_ref[0, rows, :],
                                             gng_ref[...], gnb_ref[...], ones_bd)
        return carry

    lax.fori_loop(0, tm // CHUNK, chunk_body, 0)

    @pl.when(j == pl.num_programs(1) - 1)
    def _():
        for hd in range(N_HEADS):
            lo = (hd % 2) * HEAD_DIM
            s_out[0, hd] = s_ref[hd // 2][:, lo:lo + HEAD_DIM]


def _wkv_prompt(pre, p, tm):
    r, k, v, a, b, lw, szr, bon = pre
    B, T, _ = r.shape
    tile = pl.BlockSpec((1, tm, D_RWKV), lambda bb, j: (bb, j, 0))
    return pl.pallas_call(
        _wkv_chunk_kernel,
        grid=(B, T // tm),
        in_specs=[tile] * 8 + [_const_spec((1, D_RWKV)), _const_spec((1, D_RWKV)),
                               _const_spec((MXU_DIM, MXU_DIM))],
        out_specs=[tile, pl.BlockSpec((1, N_HEADS, HEAD_DIM, HEAD_DIM), lambda bb, j: (bb, 0, 0, 0))],
        out_shape=[jax.ShapeDtypeStruct((B, T, D_RWKV), BF16),
                   jax.ShapeDtypeStruct((B, N_HEADS, HEAD_DIM, HEAD_DIM), F32)],
        scratch_shapes=[pltpu.VMEM((N_PAIRS, HEAD_DIM, PAIR), F32)],
        compiler_params=pltpu.CompilerParams(
            dimension_semantics=("arbitrary", "arbitrary"), vmem_limit_bytes=VMEM_LIMIT),
        name="wkv_prompt",
    )(r, k, v, a, b, lw, szr, bon, p["gn_g"], p["gn_b"], p["ones_bd"])


def _wkv_step_kernel(r_ref, k_ref, v_ref, a_ref, b_ref, lw_ref, szr_ref, bon_ref, gng_ref, gnb_ref,
                     ones_ref, s_in, o_ref, s_out, vec_t, y_t):
    hd = pl.program_id(0)
    ns = s_in.shape[-1]

    @pl.when(hd == 0)
    def _():
        for n, ref in enumerate((a_ref, b_ref, k_ref, r_ref)):
            vec_t[n] = ref[0].T
        vec_t[4] = jnp.exp(lw_ref[0]).T
        vec_t[5] = v_ref[0].T

    base = pl.multiple_of(hd * HEAD_DIM, HEAD_DIM)
    hrows = pl.ds(base, HEAD_DIM)
    a_h, b_h, k_h, r_h, w_h = (vec_t[n, hrows, :] for n in range(5))
    sub = lax.broadcasted_iota(jnp.int32, (SUBLANES, ns), 0)

    def value_group(g, carry):
        off = pl.multiple_of(g * SUBLANES, SUBLANES)
        v_g = vec_t[5, pl.ds(base + off, SUBLANES), :]
        y_g = jnp.zeros((SUBLANES, ns), F32)
        for ii in range(SUBLANES):
            s0 = s_in[0, off + ii]
            sa = jnp.sum(s0 * a_h, axis=0, keepdims=True)
            s1 = s0 * w_h + sa * b_h + v_g[ii:ii + 1, :] * k_h
            s_out[0, off + ii] = s1
            y_g = jnp.where(sub == ii, jnp.sum(s1 * r_h, axis=0, keepdims=True), y_g)
        y_t[pl.ds(base + off, SUBLANES), :] = y_g
        return carry

    lax.fori_loop(0, HEAD_DIM // SUBLANES, value_group, 0)

    @pl.when(hd == pl.num_programs(0) - 1)
    def _():
        o_ref[0] = _group_norm_gate(y_t[...].T, bon_ref[0], szr_ref[0], gng_ref[...], gnb_ref[...], ones_ref[...])


def _wkv_sample(pre, state_t, p):
    ns = state_t.shape[-1]
    vec = pl.BlockSpec((1, ns, D_RWKV), lambda h: (0, 0, 0))
    st = pl.BlockSpec((1, HEAD_DIM, HEAD_DIM, ns), lambda h: (h, 0, 0, 0))
    return pl.pallas_call(
        _wkv_step_kernel,
        grid=(N_HEADS,),
        in_specs=[vec] * 8 + [_const_spec((1, D_RWKV)), _const_spec((1, D_RWKV)),
                              _const_spec((MXU_DIM, MXU_DIM)), st],
        out_specs=[vec, st],
        out_shape=[jax.ShapeDtypeStruct((1, ns, D_RWKV), BF16),
                   jax.ShapeDtypeStruct(state_t.shape, F32)],
        scratch_shapes=[pltpu.VMEM((6, D_RWKV, ns), F32), pltpu.VMEM((D_RWKV, ns), F32)],
        compiler_params=pltpu.CompilerParams(
            dimension_semantics=("arbitrary",), vmem_limit_bytes=VMEM_LIMIT),
        name="wkv_sample",
    )(*pre, p["gn_g"], p["gn_b"], p["ones_bd"], state_t)


def _post_kernel(sample, *refs):
    if sample:
        (x_ref, or_ref, u_ref, szc_ref, sgr_ref, sgc_ref, cw_ref, cb_ref, lng_ref, lnb_ref,
         wbr_ref, wbc_ref, wout_ref, gpost_ref, st_ref, y_ref, st_out) = refs
    else:
        (x_ref, or_ref, u_ref, szc_ref, sgr_ref, sgc_ref, cw_ref, cb_ref, lng_ref, lnb_ref,
         wbr_ref, wbc_ref, wout_ref, gpost_ref, y_ref, ubuf) = refs

    u = u_ref[0]
    tm = u.shape[0]
    acc = jnp.broadcast_to(cb_ref[...], u.shape)
    if sample:
        for kk in range(CONV_W - 1):
            acc = acc + cw_ref[kk:kk + 1, :] * st_ref[kk]
        acc = acc + cw_ref[CONV_W - 1:CONV_W, :] * u
        st_out[0:CONV_W - 2] = st_ref[1:CONV_W - 1]
        st_out[CONV_W - 2] = u
    else:
        @pl.when(pl.program_id(1) == 0)
        def _():
            ubuf[0:CONV_PAD, :] = jnp.zeros((CONV_PAD, D_CONV), F32)
        ubuf[CONV_PAD:CONV_PAD + tm, :] = u
        win = ubuf[...]
        rows = tm + CONV_PAD
        base = CONV_PAD - (CONV_W - 1)
        for s in range(SUBLANES):
            shifted = win if s == 0 else pltpu.roll(win, shift=rows - s, axis=0)
            for kk in range(CONV_W):
                if (base + kk) % SUBLANES == s:
                    q = base + kk - s
                    acc = acc + cw_ref[kk:kk + 1, :] * shifted[q:q + tm, :]
        ubuf[0:CONV_PAD, :] = win[tm:tm + CONV_PAD, :]

    cm = jnp.mean(acc, axis=-1, keepdims=True)
    d = acc - cm
    cv = jnp.mean(d * d, axis=-1, keepdims=True)
    cn = d * lax.rsqrt(cv + LN_EPS) * lng_ref[...] + lnb_ref[...]
    o_c = (cn * _sigmoid(cn)) * szc_ref[0]
    branch_c = _dot(o_c.astype(BF16), wbc_ref[...])
    branch_r = _dot(or_ref[0], wbr_ref[...])
    merged = sgr_ref[0] * branch_r + sgc_ref[0] * branch_c
    out = _dot(merged.astype(BF16), wout_ref[...])
    ms = jnp.mean(out * out, axis=-1, keepdims=True)
    y_ref[0] = x_ref[0] + out * lax.rsqrt(ms + RMS_EPS) * gpost_ref[...]


def _post(x, o_r, u, szc, sgr, sgc, conv_state, p, tm):
    B, T, _ = x.shape
    sample = conv_state is not None
    tile = lambda w: pl.BlockSpec((1, tm, w), lambda b, j: (b, j, 0))
    in_specs = [tile(D_MODEL), tile(D_RWKV), tile(D_CONV), tile(D_CONV), tile(D_MODEL), tile(D_MODEL),
                _const_spec((CONV_W, D_CONV)), _const_spec((1, D_CONV)), _const_spec((1, D_CONV)),
                _const_spec((1, D_CONV)), _const_spec((D_RWKV, D_MODEL)), _const_spec((D_CONV, D_MODEL)),
                _const_spec((D_MODEL, D_MODEL)), _const_spec((1, D_MODEL))]
    args = [x, o_r, u, szc, sgr, sgc, p["conv_w"], p["conv_b"], p["ln_g"], p["ln_b"],
            p["w_br"], p["w_bc"], p["w_out"], p["gpost"]]
    out_specs = [tile(D_MODEL)]
    out_shape = [jax.ShapeDtypeStruct((B, T, D_MODEL), F32)]
    scratch = []
    if sample:
        st = pl.BlockSpec((CONV_W - 1, tm, D_CONV), lambda b, j: (0, j, 0))
        in_specs.append(st)
        args.append(conv_state)
        out_specs.append(st)
        out_shape.append(jax.ShapeDtypeStruct(conv_state.shape, F32))
    else:
        scratch = [pltpu.VMEM((tm + CONV_PAD, D_CONV), F32)]
    return pl.pallas_call(
        functools.partial(_post_kernel, sample),
        grid=(B, T // tm),
        in_specs=in_specs,
        out_specs=out_specs,
        out_shape=out_shape,
        scratch_shapes=scratch,
        compiler_params=pltpu.CompilerParams(
            dimension_semantics=("arbitrary", "arbitrary"), vmem_limit_bytes=VMEM_LIMIT),
        name="post_sample" if sample else "post_prompt",
    )(*args)


def _layer_params(l, norm_pre_g, w_in, mu_shift, decay_w0, decay_w2, iclr_a0, iclr_a2, k_k, k_a, r_k, gn_g, gn_b,
                  conv_glu_b, conv_w, conv_b, ln_c_g, ln_c_b, w_branch_r, w_branch_c, w_out, norm_post_g):
    row = lambda t: t[l].reshape(1, -1).astype(F32)
    zeros = jnp.zeros((LORA, D_RWKV), F32)
    w2c = jnp.concatenate([jnp.concatenate([decay_w2[l], zeros], axis=1),
                           jnp.concatenate([zeros, iclr_a2[l]], axis=1)], axis=0)
    blk = jnp.arange(MXU_DIM) // HEAD_DIM
    return dict(
        gpre=row(norm_pre_g), w_in=w_in[l].astype(BF16), mu=row(mu_shift), w0=row(decay_w0), a0=row(iclr_a0),
        w2c=w2c.astype(BF16), k_k=row(k_k), k_a=row(k_a), r_k=row(r_k), glu_b=row(conv_glu_b),
        ones_bd=(blk[:, None] == blk[None, :]).astype(BF16),
        gn_g=row(gn_g), gn_b=row(gn_b), conv_w=conv_w[l].astype(F32), conv_b=row(conv_b),
        ln_g=row(ln_c_g), ln_b=row(ln_c_b), w_br=w_branch_r[l].astype(BF16), w_bc=w_branch_c[l].astype(BF16),
        w_out=w_out[l].astype(BF16), gpost=row(norm_post_g))


def _prompt_layer(x, p, tm_prep, tm_wkv, tm_post):
    T = x.shape[1]
    *pre, u, szc, sgr, sgc, h_last = _prep(x, None, p, tm_prep)
    o_r, wkv = _wkv_prompt(pre, p, tm_wkv)
    (y,) = _post(x, o_r, u, szc, sgr, sgc, None, p, tm_post)
    return y, h_last[:, SUBLANES - 1], wkv, u[:, T - (CONV_W - 1):]


def _sample_layer(x, shift, wkv, conv, p, tm_post):
    ns = x.shape[0]
    xs = x.reshape(1, ns, D_MODEL)
    *pre, u, szc, sgr, sgc, h = _prep(xs, shift.reshape(1, ns, D_MODEL), p, ns)
    o_r, wkv_new = _wkv_sample(pre, jnp.transpose(wkv, (1, 2, 3, 0)), p)
    y, conv_new = _post(xs, o_r, u, szc, sgr, sgc, jnp.transpose(conv, (1, 0, 2)), p, tm_post)
    return (y.reshape(ns, 1, D_MODEL), h.reshape(ns, D_MODEL), jnp.transpose(wkv_new, (3, 0, 1, 2)),
            jnp.transpose(conv_new, (1, 0, 2)))


def kernel(x_prompt, x_sample, state_shift, state_wkv, state_conv, norm_pre_g, w_in, mu_shift, decay_w0, decay_w2,
           iclr_a0, iclr_a2, k_k, k_a, r_k, gn_g, gn_b, conv_glu_b, conv_w, conv_b, ln_c_g, ln_c_b, w_branch_r,
           w_branch_c, w_out, norm_post_g):
    depth = w_in.shape[0]
    xp, xs = x_prompt, x_sample
    outs = [[] for _ in range(6)]
    for l in range(depth):
        p = _layer_params(l, norm_pre_g, w_in, mu_shift, decay_w0, decay_w2, iclr_a0, iclr_a2, k_k, k_a, r_k, gn_g,
                          gn_b, conv_glu_b, conv_w, conv_b, ln_c_g, ln_c_b, w_branch_r, w_branch_c, w_out,
                          norm_post_g)
        xp, a1, a2, a3 = _prompt_layer(xp, p, tm_prep=256, tm_wkv=256, tm_post=512)
        xs, b1, b2, b3 = _sample_layer(xs, state_shift[l], state_wkv[l], state_conv[l], p, tm_post=64)
        for lst, val in zip(outs, (a1, a2, a3, b1, b2, b3)):
            lst.append(val)
    return (xp, xs) + tuple(jnp.stack(o) for o in outs)
```

```python
import functools
import math

import jax
import jax.numpy as jnp
from jax import lax
from jax.experimental import pallas as pl
from jax.experimental.pallas import tpu as pltpu

F32 = jnp.float32
BF16 = jnp.bfloat16

D_MODEL = 1024
HEAD_DIM = 64
N_HEADS = D_MODEL // HEAD_DIM
D_RWKV = N_HEADS * HEAD_DIM
D_CONV = D_MODEL // 2
CONV_W = 31
LORA = 64
RMS_EPS = 1e-6
GN_EPS = 64e-5
LN_EPS = 1e-5
N_SHIFT = 4 * D_RWKV + 2 * LORA
N_IN = N_SHIFT + 2 * D_CONV + D_CONV + 2 * D_MODEL

SUBLANES = 8
PAIR = 2 * HEAD_DIM
N_PAIRS = N_HEADS // 2
MXU_DIM = 256
CHUNK = 64
WKV_GROUP = 4
CONV_PAD = 32
VMEM_LIMIT = 56 * 1024 * 1024

_O_R, _O_K, _O_V, _O_ZR = 0, D_RWKV, 2 * D_RWKV, 3 * D_RWKV
_O_LORA = 4 * D_RWKV
_O_U = N_SHIFT
_O_ZC = _O_U + 2 * D_CONV
_O_GR = _O_ZC + D_CONV
_O_GC = _O_GR + D_MODEL


def _dot(a, b):
    return jnp.dot(a, b, preferred_element_type=F32)


def _dot_nt(a, b):
    return lax.dot_general(a, b, (((1,), (1,)), ((), ())), preferred_element_type=F32)


def _sigmoid(x):
    return 1.0 / (1.0 + jnp.exp(-x))


def _split_hi_lo(x):
    hi = x.astype(BF16)
    lo = (x - hi.astype(F32)).astype(BF16)
    return hi, lo


def _head_sum(x, ones_bd):
    outs = []
    for c in range(x.shape[1] // MXU_DIM):
        hi, lo = _split_hi_lo(x[:, c * MXU_DIM:(c + 1) * MXU_DIM])
        outs.append(_dot(hi, ones_bd) + _dot(lo, ones_bd))
    return jnp.concatenate(outs, axis=1)


def _group_norm_gate(y, bonus, szr, gng, gnb, ones_bd):
    mean = _head_sum(y, ones_bd) * (1.0 / HEAD_DIM)
    d = y - mean
    var = _head_sum(d * d, ones_bd) * (1.0 / HEAD_DIM)
    y_gn = d * lax.rsqrt(var + GN_EPS) * gng + gnb
    return ((y_gn + bonus) * szr).astype(BF16)


def _prep_kernel(sample, *refs):
    if sample:
        (x_ref, prev_ref, gpre_ref, win_ref, mu_ref, w0_ref, a0_ref, w2c_ref, kk_ref, ka_ref, rk_ref,
         glub_ref, ones_ref,
         r_out, k_out, v_out, a_out, b_out, lw_out, szr_out, bon_out, u_out, szc_out, sgr_out, sgc_out,
         h_out) = refs
    else:
        (x_ref, gpre_ref, win_ref, mu_ref, w0_ref, a0_ref, w2c_ref, kk_ref, ka_ref, rk_ref,
         glub_ref, ones_ref,
         r_out, k_out, v_out, a_out, b_out, lw_out, szr_out, bon_out, u_out, szc_out, sgr_out, sgc_out,
         h_out, carry_ref) = refs

    x = x_ref[0]
    tm = x.shape[0]
    ms = jnp.mean(x * x, axis=-1, keepdims=True)
    h = x * lax.rsqrt(ms + RMS_EPS) * gpre_ref[...]
    hb = h.astype(BF16)

    if sample:
        h_out[0] = h
        pb = prev_ref[0].astype(BF16)
    else:
        h_out[0] = h[tm - SUBLANES:, :]

        @pl.when(pl.program_id(1) == 0)
        def _():
            carry_ref[...] = jnp.zeros_like(carry_ref)
        row0 = lax.broadcasted_iota(jnp.int32, (tm, 1), 0) == 0

    def shifted(off, width):
        p = _dot(hb, win_ref[:, off:off + width])
        if sample:
            pp = _dot(pb, win_ref[:, off:off + width])
        else:
            rolled = pltpu.roll(p, shift=1, axis=0)
            pp = jnp.where(row0, carry_ref[0:1, off:off + width], rolled)
            carry_ref[0:1, off:off + width] = p[tm - 1:tm, :]
        return p + (pp - p) * mu_ref[:, off:off + width]

    ones_bd = ones_ref[...]

    lo = shifted(_O_LORA, 2 * LORA)
    lane = lax.broadcasted_iota(jnp.int32, lo.shape, 1)
    lo = jnp.where(lane < LORA, jnp.tanh(lo), lo)
    lor = _dot(lo.astype(BF16), w2c_ref[...])
    w_raw = w0_ref[...] + lor[:, :D_RWKV]
    lw_out[0] = (-math.exp(-0.5)) * _sigmoid(w_raw)
    iclr = _sigmoid(a0_ref[...] + lor[:, D_RWKV:])

    k = shifted(_O_K, D_RWKV)
    kk = k * kk_ref[...]
    kk = kk * lax.rsqrt(_head_sum(kk * kk, ones_bd) + 1e-12)
    k2 = k * (1.0 + (iclr - 1.0) * ka_ref[...])
    k_out[0] = k2
    a_out[0] = -kk
    b_out[0] = kk * iclr

    r = shifted(_O_R, D_RWKV)
    r_out[0] = r
    bsum = _head_sum(r * k2 * rk_ref[...], ones_bd)
    v = shifted(_O_V, D_RWKV)
    v_out[0] = v
    bon_out[0] = bsum * v

    zr = shifted(_O_ZR, D_RWKV)
    szr_out[0] = zr * _sigmoid(zr)

    u_in = _dot(hb, win_ref[:, _O_U:_O_U + 2 * D_CONV]) + glub_ref[...]
    u_out[0] = u_in[:, :D_CONV] * _sigmoid(u_in[:, D_CONV:])
    zc = _dot(hb, win_ref[:, _O_ZC:_O_ZC + D_CONV])
    szc_out[0] = zc * _sigmoid(zc)
    sgr_out[0] = _sigmoid(_dot(hb, win_ref[:, _O_GR:_O_GR + D_MODEL]))
    sgc_out[0] = _sigmoid(_dot(hb, win_ref[:, _O_GC:_O_GC + D_MODEL]))


def _const_spec(shape):
    nd = len(shape)
    return pl.BlockSpec(shape, lambda *_: (0,) * nd, pipeline_mode=pl.Buffered(1))


def _prep(x, prev, p, tm):
    B, T, _ = x.shape
    sample = prev is not None
    tile = lambda w: pl.BlockSpec((1, tm, w), lambda b, j: (b, j, 0))
    in_specs = [tile(D_MODEL)] + ([tile(D_MODEL)] if sample else []) + [
        _const_spec((1, D_MODEL)), _const_spec((D_MODEL, N_IN)), _const_spec((1, N_SHIFT)),
        _const_spec((1, D_RWKV)), _const_spec((1, D_RWKV)), _const_spec((2 * LORA, 2 * D_RWKV)),
        _const_spec((1, D_RWKV)), _const_spec((1, D_RWKV)), _const_spec((1, D_RWKV)),
        _const_spec((1, 2 * D_CONV)), _const_spec((MXU_DIM, MXU_DIM))]
    widths = [D_RWKV] * 8 + [D_CONV, D_CONV, D_MODEL, D_MODEL]
    h_spec =(tile(D_MODEL) if sample else
              pl.BlockSpec((1, SUBLANES, D_MODEL), lambda b, j: (b, 0, 0)))
    out_specs = [tile(w) for w in widths] + [h_spec]
    out_shape = [jax.ShapeDtypeStruct((B, T, w), F32) for w in widths] + [
        jax.ShapeDtypeStruct((B, T if sample else SUBLANES, D_MODEL), F32)]
    args = [x] + ([prev] if sample else []) + [
        p["gpre"], p["w_in"], p["mu"], p["w0"], p["a0"], p["w2c"], p["k_k"], p["k_a"], p["r_k"],
        p["glu_b"], p["ones_bd"]]
    return pl.pallas_call(
        functools.partial(_prep_kernel, sample),
        grid=(B, T // tm),
        in_specs=in_specs,
        out_specs=out_specs,
        out_shape=out_shape,
        scratch_shapes=[] if sample else [pltpu.VMEM((SUBLANES, N_SHIFT), F32)],
        compiler_params=pltpu.CompilerParams(
            dimension_semantics=("arbitrary", "arbitrary"), vmem_limit_bytes=VMEM_LIMIT),
        name="prep_sample" if sample else "prep_prompt",
    )(*args)


def _block_diag(x, same_head):
    return jnp.where(same_head, jnp.concatenate([x, x], axis=0), jnp.zeros((), x.dtype))


def _wkv_chunk_kernel(r_ref, k_ref, v_ref, a_ref, b_ref, lw_ref, szr_ref, bon_ref, gng_ref, gnb_ref, ones_ref,
                      o_ref, s_out, s_ref, y_ref):
    tm = r_ref.shape[1]
    j = pl.program_id(1)

    @pl.when(j == 0)
    def _():
        s_ref[...] = jnp.zeros_like(s_ref)

    row = lax.broadcasted_iota(jnp.int32, (CHUNK, PAIR), 0)
    col = lax.broadcasted_iota(jnp.int32, (CHUNK, PAIR), 1) % HEAD_DIM
    strict = row > col
    incl = row >= col
    eye_pair = (row == col).astype(F32)
    brow = lax.broadcasted_iota(jnp.int32, (PAIR, PAIR), 0)
    bcol = lax.broadcasted_iota(jnp.int32, (PAIR, PAIR), 1)
    same_head = (brow // HEAD_DIM) == (bcol // HEAD_DIM)
    eye_full = brow == bcol
    first_head = lax.broadcasted_iota(jnp.int32, (CHUNK, PAIR), 1) < HEAD_DIM
    trow = lax.broadcasted_iota(jnp.int32, (CHUNK, CHUNK), 0)
    tcol = lax.broadcasted_iota(jnp.int32, (CHUNK, CHUNK), 1)
    tri = (trow >= tcol).astype(BF16)
    bd = lambda x: _block_diag(x.astype(BF16), same_head)

    def bd_t(x):
        return jnp.where(same_head, jnp.concatenate([x, x], axis=0).T, 0.0).astype(BF16)

    def chunk_inputs(row0):
        rows = pl.ds(row0, CHUNK)
        lw = lw_ref[0, rows, :]
        l1 = lw.astype(BF16)
        rem = lw - l1.astype(F32)
        l2 = rem.astype(BF16)
        l3 = (rem - l2.astype(F32)).astype(BF16)
        cl = _dot(tri, l1) + _dot(tri, l2) + _dot(tri, l3)
        e_neg = jnp.exp(-cl)
        w_end = jnp.exp(cl[CHUNK - 1:CHUNK, :])
        b_t = b_ref[0, rows, :] * e_neg
        k_t = k_ref[0, rows, :] * e_neg
        return dict(r=r_ref[0, rows, :] * jnp.exp(cl), a=a_ref[0, rows, :] * jnp.exp(cl - lw), b=b_t, k=k_t,
                    be=b_t * w_end, ke=k_t * w_end, v=v_ref[0, rows, :], w_end=w_end)

    def group_body(gi, carry):
        base = gi * (WKV_GROUP * CHUNK)
        if not isinstance(base, int):
            base = pl.multiple_of(base, WKV_GROUP * CHUNK)
        ch = [chunk_inputs(base + c * CHUNK) for c in range(WKV_GROUP)]
        pairs = range(N_PAIRS)
        sls = [slice(p * PAIR, (p + 1) * PAIR) for p in pairs]
        units = [(c, p) for c in range(WKV_GROUP) for p in pairs]
        us = range(len(units))
        pick = lambda name: [ch[c][name][:, sls[p]] for c, p in units]
        at, rt, bt, kt, be, ke, vp = (pick(n) for n in ("a", "r", "b", "k", "be", "ke", "v"))
        g = [_dot(jnp.concatenate([at[u], rt[u]], axis=0).astype(BF16),
                  jnp.concatenate([bd_t(bt[u]), bd_t(kt[u])], axis=1)) for u in us]
        l_ab = [jnp.where(strict, g[u][:CHUNK, :PAIR], 0.0) for u in us]
        m_rb = [jnp.where(incl, g[u][CHUNK:, :PAIR], 0.0).astype(BF16) for u in us]
        km = [jnp.concatenate([jnp.where(strict, g[u][:CHUNK, PAIR:], 0.0),
                               jnp.where(incl, g[u][CHUNK:, PAIR:], 0.0)], axis=0).astype(BF16) for u in us]
        uy = [_dot(km[u], bd(vp[u])) for u in us]
        uv0 = [uy[u][:CHUNK] for u in us]
        t_acc = [eye_pair + l_ab[u] for u in us]
        lp = [_dot(l_ab[u].astype(BF16), bd(l_ab[u])) for u in us]
        for _ in range(4):
            prod = [_dot(lp[u].astype(BF16), jnp.concatenate([bd(lp[u]), bd(t_acc[u])], axis=1)) for u in us]
            lp = [prod[u][:, :PAIR] for u in us]
            t_acc = [t_acc[u] + prod[u][:, PAIR:] for u in us]
        t_acc = [t_acc[u] + _dot(lp[u].astype(BF16), bd(t_acc[u])) for u in us]
        au = [_dot(t_acc[u].astype(BF16), jnp.concatenate([bd(at[u]), bd(uv0[u])], axis=1)) for u in us]
        a_hat = [au[u][:, :PAIR] for u in us]
        u_v = [au[u][:, PAIR:] for u in us]
        ry = [_dot(m_rb[u], jnp.concatenate([bd(a_hat[u]), bd(u_v[u])], axis=1)) for u in us]
        r_hat = [(rt[u] + ry[u][:, :PAIR]).astype(BF16) for u in us]
        y_v = [ry[u][:, PAIR:] + uy[u][CHUNK:] for u in us]
        phi = [_dot(a_hat[u].T.astype(BF16), be[u].astype(BF16)) for u in us]
        f = [_dot(jnp.concatenate([u_v[u], vp[u]], axis=0).T.astype(BF16),
                  jnp.concatenate([be[u], ke[u]], axis=0).astype(BF16)) for u in us]
        phi = [(jnp.where(same_head, phi[u], 0.0)
                + jnp.where(eye_full, ch[units[u][0]]["w_end"][:, sls[units[u][1]]], 0.0)).astype(BF16) for u in us]
        z = [jnp.where(first_head, f[u][:HEAD_DIM, :], f[u][HEAD_DIM:, :]) for u in us]
        state = [s_ref[p] for p in pairs]
        for c in range(WKV_GROUP):
            ys = [_dot(r_hat[c * N_PAIRS + p], bd_t(state[p])) + y_v[c * N_PAIRS + p] for p in pairs]
            state = [_dot(state[p].astype(BF16), phi[c * N_PAIRS + p]) + z[c * N_PAIRS + p] for p in pairs]
            y_ref[pl.ds(base + c * CHUNK, CHUNK), :] = jnp.concatenate(ys, axis=1)
        for p in pairs:
            s_ref[p] = state[p]
        return carry

    n_groups = tm // (WKV_GROUP * CHUNK)
    if n_groups == 1:
        group_body(0, 0)
    else:
        lax.fori_loop(0, n_groups, group_body, 0)
    o_ref[0] = _group_norm_gate(y_ref[...], bon_ref[0], szr_ref[0], gng_ref[...], gnb_ref[...], ones_ref[...])

    @pl.when(j == pl.num_programs(1) - 1)
    def _():
        for hd in range(N_HEADS):
            lo = (hd % 2) * HEAD_DIM
            s_out[0, hd] = s_ref[hd // 2][:, lo:lo + HEAD_DIM]


def _wkv_prompt(pre, p, tm):
    r, k, v, a, b, lw, szr, bon = pre
    B, T, _ = r.shape
    assert T % tm == 0 and tm % (WKV_GROUP * CHUNK) == 0, (T, tm)
    tile = pl.BlockSpec((1, tm, D_RWKV), lambda bb, j: (bb, j, 0))
    return pl.pallas_call(
        _wkv_chunk_kernel,
        grid=(B, T // tm),
        in_specs=[tile] * 8 + [_const_spec((1, D_RWKV)), _const_spec((1, D_RWKV)),
                               _const_spec((MXU_DIM, MXU_DIM))],
        out_specs=[tile, pl.BlockSpec((1, N_HEADS, HEAD_DIM, HEAD_DIM), lambda bb, j: (bb, 0, 0, 0))],
        out_shape=[jax.ShapeDtypeStruct((B, T, D_RWKV), BF16),
                   jax.ShapeDtypeStruct((B, N_HEADS, HEAD_DIM, HEAD_DIM), F32)],
        scratch_shapes=[pltpu.VMEM((N_PAIRS, HEAD_DIM, PAIR), F32), pltpu.VMEM((tm, D_RWKV), F32)],
        compiler_params=pltpu.CompilerParams(
            dimension_semantics=("arbitrary", "arbitrary"), vmem_limit_bytes=VMEM_LIMIT),
        name="wkv_prompt",
    )(r, k, v, a, b, lw, szr, bon, p["gn_g"], p["gn_b"], p["ones_bd"])


def _wkv_step_kernel(r_ref, k_ref, v_ref, a_ref, b_ref, lw_ref, szr_ref, bon_ref, gng_ref, gnb_ref,
                     ones_ref, s_in, o_ref, s_out, vec_t, y_t):
    hd = pl.program_id(0)
    ns = s_in.shape[-1]

    @pl.when(hd == 0)
    def _():
        for n, ref in enumerate((a_ref, b_ref, k_ref, r_ref)):
            vec_t[n] = ref[0].T
        vec_t[4] = jnp.exp(lw_ref[0]).T
        vec_t[5] = v_ref[0].T

    base = pl.multiple_of(hd * HEAD_DIM, HEAD_DIM)
    hrows = pl.ds(base, HEAD_DIM)
    a_h, b_h, k_h, r_h, w_h = (vec_t[n, hrows, :] for n in range(5))
    sub = lax.broadcasted_iota(jnp.int32, (SUBLANES, ns), 0)

    def value_group(g, carry):
        off = pl.multiple_of(g * SUBLANES, SUBLANES)
        v_g = vec_t[5, pl.ds(base + off, SUBLANES), :]
        y_g = jnp.zeros((SUBLANES, ns), F32)
        for ii in range(SUBLANES):
            s0 = s_in[0, off + ii]
            sa = jnp.sum(s0 * a_h, axis=0, keepdims=True)
            s1 = s0 * w_h + sa * b_h + v_g[ii:ii + 1, :] * k_h
            s_out[0, off + ii] = s1
            y_g = jnp.where(sub == ii, jnp.sum(s1 * r_h, axis=0, keepdims=True), y_g)
        y_t[pl.ds(base + off, SUBLANES), :] = y_g
        return carry

    lax.fori_loop(0, HEAD_DIM // SUBLANES, value_group, 0)

    @pl.when(hd == pl.num_programs(0) - 1)
    def _():
        o_ref[0] = _group_norm_gate(y_t[...].T, bon_ref[0], szr_ref[0], gng_ref[...], gnb_ref[...], ones_ref[...])


def _wkv_sample(pre, state_t, p):
    ns = state_t.shape[-1]
    vec = pl.BlockSpec((1, ns, D_RWKV), lambda h: (0, 0, 0))
    st = pl.BlockSpec((1, HEAD_DIM, HEAD_DIM, ns), lambda h: (h, 0, 0, 0))
    return pl.pallas_call(
        _wkv_step_kernel,
        grid=(N_HEADS,),
        in_specs=[vec] * 8 + [_const_spec((1, D_RWKV)), _const_spec((1, D_RWKV)),
                              _const_spec((MXU_DIM, MXU_DIM)), st],
        out_specs=[vec, st],
        out_shape=[jax.ShapeDtypeStruct((1, ns, D_RWKV), BF16),
                   jax.ShapeDtypeStruct(state_t.shape, F32)],
        scratch_shapes=[pltpu.VMEM((6, D_RWKV, ns), F32), pltpu.VMEM((D_RWKV, ns), F32)],
        compiler_params=pltpu.CompilerParams(
            dimension_semantics=("arbitrary",), vmem_limit_bytes=VMEM_LIMIT),
        name="wkv_sample",
    )(*pre, p["gn_g"], p["gn_b"], p["ones_bd"], state_t)


def _post_kernel(sample, *refs):
    if sample:
        (x_ref, or_ref, u_ref, szc_ref, sgr_ref, sgc_ref, cw_ref, cb_ref, lng_ref, lnb_ref,
         wbr_ref, wbc_ref, wout_ref, gpost_ref, st_ref, y_ref, st_out) = refs
    else:
        (x_ref, or_ref, u_ref, szc_ref, sgr_ref, sgc_ref, cw_ref, cb_ref, lng_ref, lnb_ref,
         wbr_ref, wbc_ref, wout_ref, gpost_ref, y_ref, ubuf) = refs

    u = u_ref[0]
    tm = u.shape[0]
    acc = jnp.broadcast_to(cb_ref[...], u.shape)
    if sample:
        for kk in range(CONV_W - 1):
            acc = acc + cw_ref[kk:kk + 1, :] * st_ref[kk]
        acc = acc + cw_ref[CONV_W - 1:CONV_W, :] * u
        st_out[0:CONV_W - 2] = st_ref[1:CONV_W - 1]
        st_out[CONV_W - 2] = u
    else:
        @pl.when(pl.program_id(1) == 0)
        def _():
            ubuf[0:CONV_PAD, :] = jnp.zeros((CONV_PAD, D_CONV), F32)
        ubuf[CONV_PAD:CONV_PAD + tm, :] = u
        win = ubuf[...]
        rows = tm + CONV_PAD
        base = CONV_PAD - (CONV_W - 1)
        for s in range(SUBLANES):
            shifted = win if s == 0 else pltpu.roll(win, shift=rows - s, axis=0)
            for kk in range(CONV_W):
                if (base + kk) % SUBLANES == s:
                    q = base + kk - s
                    acc = acc + cw_ref[kk:kk + 1, :] * shifted[q:q + tm, :]
        ubuf[0:CONV_PAD, :] = win[tm:tm + CONV_PAD, :]

    cm = jnp.mean(acc, axis=-1, keepdims=True)
    d = acc - cm
    cv = jnp.mean(d * d, axis=-1, keepdims=True)
    cn = d * lax.rsqrt(cv + LN_EPS) * lng_ref[...] + lnb_ref[...]
    o_c = (cn * _sigmoid(cn)) * szc_ref[0]
    branch_c = _dot(o_c.astype(BF16), wbc_ref[...])
    branch_r = _dot(or_ref[0], wbr_ref[...])
    merged = sgr_ref[0] * branch_r + sgc_ref[0] * branch_c
    out = _dot(merged.astype(BF16), wout_ref[...])
    ms = jnp.mean(out * out, axis=-1, keepdims=True)
    y_ref[0] = x_ref[0] + out * lax.rsqrt(ms + RMS_EPS) * gpost_ref[...]


def _post(x, o_r, u, szc, sgr, sgc, conv_state, p, tm):
    B, T, _ = x.shape
    sample = conv_state is not None
    tile = lambda w: pl.BlockSpec((1, tm, w), lambda b, j: (b, j, 0))
    in_specs = [tile(D_MODEL), tile(D_RWKV), tile(D_CONV), tile(D_CONV), tile(D_MODEL), tile(D_MODEL),
                _const_spec((CONV_W, D_CONV)), _const_spec((1, D_CONV)), _const_spec((1, D_CONV)),
                _const_spec((1, D_CONV)), _const_spec((D_RWKV, D_MODEL)), _const_spec((D_CONV, D_MODEL)),
                _const_spec((D_MODEL, D_MODEL)), _const_spec((1, D_MODEL))]
    args = [x, o_r, u, szc, sgr, sgc, p["conv_w"], p["conv_b"], p["ln_g"], p["ln_b"],
            p["w_br"], p["w_bc"], p["w_out"], p["gpost"]]
    out_specs = [tile(D_MODEL)]
    out_shape = [jax.ShapeDtypeStruct((B, T, D_MODEL), F32)]
    scratch = []
    if sample:
        st = pl.BlockSpec((CONV_W - 1, tm, D_CONV), lambda b, j: (0, j, 0))
        in_specs.append(st)
        args.append(conv_state)
        out_specs.append(st)
        out_shape.append(jax.ShapeDtypeStruct(conv_state.shape, F32))
    else:
        scratch = [pltpu.VMEM((tm + CONV_PAD, D_CONV), F32)]
    return pl.pallas_call(
        functools.partial(_post_kernel, sample),
        grid=(B, T // tm),
        in_specs=in_specs,
        out_specs=out_specs,
        out_shape=out_shape,
        scratch_shapes=scratch,
        compiler_params=pltpu.CompilerParams(
            dimension_semantics=("arbitrary", "arbitrary"), vmem_limit_bytes=VMEM_LIMIT),
        name="post_sample" if sample else "post_prompt",
    )(*args)


def _layer_params(l, norm_pre_g, w_in, mu_shift, decay_w0, decay_w2, iclr_a0, iclr_a2, k_k, k_a, r_k, gn_g, gn_b,
                  conv_glu_b, conv_w, conv_b, ln_c_g, ln_c_b, w_branch_r, w_branch_c, w_out, norm_post_g):
    row = lambda t: t[l].reshape(1, -1).astype(F32)
    zeros = jnp.zeros((LORA, D_RWKV), F32)
    w2c = jnp.concatenate([jnp.concatenate([decay_w2[l], zeros], axis=1),
                           jnp.concatenate([zeros, iclr_a2[l]], axis=1)], axis=0)
    blk = jnp.arange(MXU_DIM) // HEAD_DIM
    return dict(
        gpre=row(norm_pre_g), w_in=w_in[l].astype(BF16), mu=row(mu_shift), w0=row(decay_w0), a0=row(iclr_a0),
        w2c=w2c.astype(BF16), k_k=row(k_k), k_a=row(k_a), r_k=row(r_k), glu_b=row(conv_glu_b),
        ones_bd=(blk[:, None] == blk[None, :]).astype(BF16),
        gn_g=row(gn_g), gn_b=row(gn_b), conv_w=conv_w[l].astype(F32), conv_b=row(conv_b),
        ln_g=row(ln_c_g), ln_b=row(ln_c_b), w_br=w_branch_r[l].astype(BF16), w_bc=w_branch_c[l].astype(BF16),
        w_out=w_out[l].astype(BF16), gpost=row(norm_post_g))


def _prompt_layer(x, p, tm_prep, tm_wkv, tm_post):
    T = x.shape[1]
    *pre, u, szc, sgr, sgc, h_last = _prep(x, None, p, tm_prep)
    o_r, wkv = _wkv_prompt(pre, p, tm_wkv)
    (y,) = _post(x, o_r, u, szc, sgr, sgc, None, p, tm_post)
    return y, h_last[:, SUBLANES - 1], wkv, u[:, T - (CONV_W - 1):]


def _sample_layer(x, shift, wkv, conv, p, tm_post):
    ns = x.shape[0]
    xs = x.reshape(1, ns, D_MODEL)
    *pre, u, szc, sgr, sgc, h = _prep(xs, shift.reshape(1, ns, D_MODEL), p, ns)
    o_r, wkv_new = _wkv_sample(pre, jnp.transpose(wkv, (1, 2, 3, 0)), p)
    y, conv_new = _post(xs, o_r, u, szc, sgr, sgc, jnp.transpose(conv, (1, 0, 2)), p, tm_post)
    return (y.reshape(ns, 1, D_MODEL), h.reshape(ns, D_MODEL), jnp.transpose(wkv_new, (3, 0, 1, 2)),
            jnp.transpose(conv_new, (1, 0, 2)))


def kernel(x_prompt, x_sample, state_shift, state_wkv, state_conv, norm_pre_g, w_in, mu_shift, decay_w0, decay_w2,
           iclr_a0, iclr_a2, k_k, k_a, r_k, gn_g, gn_b, conv_glu_b, conv_w, conv_b, ln_c_g, ln_c_b, w_branch_r,
           w_branch_c, w_out, norm_post_g):
    depth = w_in.shape[0]
    xp, xs = x_prompt, x_sample
    outs = [[] for _ in range(6)]
    for l in range(depth):
        p = _layer_params(l, norm_pre_g, w_in, mu_shift, decay_w0, decay_w2, iclr_a0, iclr_a2, k_k, k_a, r_k, gn_g,
                          gn_b, conv_glu_b, conv_w, conv_b, ln_c_g, ln_c_b, w_branch_r, w_branch_c, w_out,
                          norm_post_g)
        xp, a1, a2, a3 = _prompt_layer(xp, p, tm_prep=256, tm_wkv=256, tm_post=512)
        xs, b1, b2, b3 = _sample_layer(xs, state_shift[l], state_wkv[l], state_conv[l], p, tm_post=64)
        for lst, val in zip(outs, (a1, a2, a3, b1, b2, b3)):
            lst.append(val)
    return (xp, xs) + tuple(jnp.stack(o) for o in outs)
```

```python
import functools
import math

import jax
import jax.numpy as jnp
from jax import lax
from jax.experimental import pallas as pl
from jax.experimental.pallas import tpu as pltpu

F32 = jnp.float32
BF16 = jnp.bfloat16

D_MODEL = 1024
HEAD_DIM = 64
N_HEADS = D_MODEL // HEAD_DIM
D_RWKV = N_HEADS * HEAD_DIM
D_CONV = D_MODEL // 2
CONV_W = 31
LORA = 64
RMS_EPS = 1e-6
GN_EPS = 64e-5
LN_EPS = 1e-5
N_SHIFT = 4 * D_RWKV + 2 * LORA
N_IN = N_SHIFT + 2 * D_CONV + D_CONV + 2 * D_MODEL

SUBLANES = 8
PAIR = 2 * HEAD_DIM
N_PAIRS = N_HEADS // 2
MXU_DIM = 256
CHUNK = 64
TILE_CHUNKS = 4
WKV_GROUP = 2
CONV_PAD = 32
VMEM_LIMIT = 56 * 1024 * 1024

_O_R, _O_K, _O_V, _O_ZR = 0, D_RWKV, 2 * D_RWKV, 3 * D_RWKV
_O_LORA = 4 * D_RWKV
_O_U = N_SHIFT
_O_ZC = _O_U + 2 * D_CONV
_O_GR = _O_ZC + D_CONV
_O_GC = _O_GR + D_MODEL


def _dot(a, b):
    return jnp.dot(a, b, preferred_element_type=F32)


def _sigmoid(x):
    return 1.0 / (1.0 + jnp.exp(-x))


def _split_hi_lo(x):
    hi = x.astype(BF16)
    lo = (x - hi.astype(F32)).astype(BF16)
    return hi, lo


def _head_sum(x, ones_bd):
    outs = []
    for c in range(x.shape[1] // MXU_DIM):
        hi, lo = _split_hi_lo(x[:, c * MXU_DIM:(c + 1) * MXU_DIM])
        outs.append(_dot(hi, ones_bd) + _dot(lo, ones_bd))
    return jnp.concatenate(outs, axis=1)


def _group_norm_gate(y, bonus, szr, gng, gnb, ones_bd):
    mean = _head_sum(y, ones_bd) * (1.0 / HEAD_DIM)
    d = y - mean
    var = _head_sum(d * d, ones_bd) * (1.0 / HEAD_DIM)
    y_gn = d * lax.rsqrt(var + GN_EPS) * gng + gnb
    return ((y_gn + bonus) * szr).astype(BF16)


def _prep_kernel(sample, *refs):
    if sample:
        (x_ref, prev_ref, gpre_ref, win_ref, mu_ref, w0_ref, a0_ref, w2c_ref, kk_ref, ka_ref, rk_ref,
         glub_ref, ones_ref,
         r_out, k_out, v_out, a_out, b_out, lw_out, szr_out, bon_out, u_out, szc_out, sgr_out, sgc_out,
         h_out) = refs
    else:
        (x_ref, gpre_ref, win_ref, mu_ref, w0_ref, a0_ref, w2c_ref, kk_ref, ka_ref, rk_ref,
         glub_ref, ones_ref,
         r_out, k_out, v_out, a_out, b_out, lw_out, szr_out, bon_out, u_out, szc_out, sgr_out, sgc_out,
         h_out, carry_ref) = refs

    x = x_ref[0]
    tm = x.shape[0]
    ms = jnp.mean(x * x, axis=-1, keepdims=True)
    h = x * lax.rsqrt(ms + RMS_EPS) * gpre_ref[...]
    hb = h.astype(BF16)

    if sample:
        h_out[0] = h
        pb = prev_ref[0].astype(BF16)
    else:
        h_out[0] = h[tm - SUBLANES:, :]

        @pl.when(pl.program_id(1) == 0)
        def _():
            carry_ref[...] = jnp.zeros_like(carry_ref)
        row0 = lax.broadcasted_iota(jnp.int32, (tm, 1), 0) == 0

    def shifted(off, width):
        p = _dot(hb, win_ref[:, off:off + width])
        if sample:
            pp = _dot(pb, win_ref[:, off:off + width])
        else:
            rolled = pltpu.roll(p, shift=1, axis=0)
            pp = jnp.where(row0, carry_ref[0:1, off:off + width], rolled)
            carry_ref[0:1, off:off + width] = p[tm - 1:tm, :]
        return p + (pp - p) * mu_ref[:, off:off + width]

    ones_bd = ones_ref[...]

    lo = shifted(_O_LORA, 2 * LORA)
    lane = lax.broadcasted_iota(jnp.int32, lo.shape, 1)
    lo = jnp.where(lane < LORA, jnp.tanh(lo), lo)
    lor = _dot(lo.astype(BF16), w2c_ref[...])
    w_raw = w0_ref[...] + lor[:, :D_RWKV]
    lw_out[0] = (-math.exp(-0.5)) * _sigmoid(w_raw)
    iclr = _sigmoid(a0_ref[...] + lor[:, D_RWKV:])

    k = shifted(_O_K, D_RWKV)
    kk = k * kk_ref[...]
    kk = kk * lax.rsqrt(_head_sum(kk * kk, ones_bd) + 1e-12)
    k2 = k * (1.0 + (iclr - 1.0) * ka_ref[...])
    k_out[0] = k2
    a_out[0] = -kk
    b_out[0] = kk * iclr

    zr = shifted(_O_ZR, D_RWKV)
    szr_out[0] = zr * _sigmoid(zr)

    u_in = _dot(hb, win_ref[:, _O_U:_O_U + 2 * D_CONV]) + glub_ref[...]
    u_out[0] = u_in[:, :D_CONV] * _sigmoid(u_in[:, D_CONV:])
    zc = _dot(hb, win_ref[:, _O_ZC:_O_ZC + D_CONV])
    szc_out[0] = zc * _sigmoid(zc)
    sgr_out[0] = _sigmoid(_dot(hb, win_ref[:, _O_GR:_O_GR + D_MODEL]))
    sgc_out[0] = _sigmoid(_dot(hb, win_ref[:, _O_GC:_O_GC + D_MODEL]))

    r = shifted(_O_R, D_RWKV)
    r_out[0] = r
    bsum = _head_sum(r * k2 * rk_ref[...], ones_bd)
    v = shifted(_O_V, D_RWKV)
    v_out[0] = v
    bon_out[0] = bsum * v


def _const_spec(shape):
    nd = len(shape)
    return pl.BlockSpec(shape, lambda *_: (0,) * nd, pipeline_mode=pl.Buffered(1))


def _prep(x, prev, p, tm):
    B, T, _ = x.shape
    sample = prev is not None
    tile = lambda w: pl.BlockSpec((1, tm, w), lambda b, j: (b, j, 0))
    in_specs = [tile(D_MODEL)] + ([tile(D_MODEL)] if sample else []) + [
        _const_spec((1, D_MODEL)), _const_spec((D_MODEL, N_IN)), _const_spec((1, N_SHIFT)),
        _const_spec((1, D_RWKV)), _const_spec((1, D_RWKV)), _const_spec((2 * LORA, 2 * D_RWKV)),
        _const_spec((1, D_RWKV)), _const_spec((1, D_RWKV)), _const_spec((1, D_RWKV)),
        _const_spec((1, 2 * D_CONV)), _const_spec((MXU_DIM, MXU_DIM))]
    widths = [D_RWKV] * 8 + [D_CONV, D_CONV, D_MODEL, D_MODEL]
    h_spec = tile(D_MODEL) if sample else pl.BlockSpec((1, SUBLANES, D_MODEL), lambda b, j: (b, 0, 0))
    out_specs = [tile(w) for w in widths] + [h_spec]
    out_shape = [jax.ShapeDtypeStruct((B, T, w), F32) for w in widths] + [
        jax.ShapeDtypeStruct((B, T if sample else SUBLANES, D_MODEL), F32)]
    args = [x] + ([prev] if sample else []) + [
        p["gpre"], p["w_in"], p["mu"], p["w0"], p["a0"], p["w2c"], p["k_k"], p["k_a"], p["r_k"],
        p["glu_b"], p["ones_bd"]]
    return pl.pallas_call(
        functools.partial(_prep_kernel, sample),
        grid=(B, T // tm),
        in_specs=in_specs,
        out_specs=out_specs,
        out_shape=out_shape,
        scratch_shapes=[] if sample else [pltpu.VMEM((SUBLANES, N_SHIFT), F32)],
        compiler_params=pltpu.CompilerParams(
            dimension_semantics=("arbitrary", "arbitrary"), vmem_limit_bytes=VMEM_LIMIT),
        name="prep_sample" if sample else "prep_prompt",
    )(*args)


def _block_diag(x, same_head):
    return jnp.where(same_head, jnp.concatenate([x, x], axis=0), jnp.zeros((), x.dtype))


def _drain(steps):
    for _ in steps:
        pass


def _wkv_masks():
    row = lax.broadcasted_iota(jnp.int32, (CHUNK, PAIR), 0)
    col = lax.broadcasted_iota(jnp.int32, (CHUNK, PAIR), 1) % HEAD_DIM
    brow = lax.broadcasted_iota(jnp.int32, (PAIR, PAIR), 0)
    bcol = lax.broadcasted_iota(jnp.int32, (PAIR, PAIR), 1)
    trow = lax.broadcasted_iota(jnp.int32, (CHUNK, CHUNK), 0)
    tcol = lax.broadcasted_iota(jnp.int32, (CHUNK, CHUNK), 1)
    return dict(strict=row > col, incl=row >= col, eye_pair=(row == col).astype(F32),
                same_head=(brow // HEAD_DIM) == (bcol // HEAD_DIM), eye_full=brow == bcol,
                first_head=lax.broadcasted_iota(jnp.int32, (CHUNK, PAIR), 1) < HEAD_DIM,
                tri=(trow >= tcol).astype(BF16))


def _block_diag_t(x, same_head):
    return jnp.where(same_head, jnp.concatenate([x, x], axis=0).T, 0.0).astype(BF16)


def _wkv_chunk_operators(r_ref, k_ref, v_ref, a_ref, b_ref, lw_ref, first_chunk, m, tick):
    strict, incl, eye_pair, same_head = m["strict"], m["incl"], m["eye_pair"], m["same_head"]
    bd = lambda x: _block_diag(x.astype(BF16), same_head)
    bd_t = lambda x: _block_diag_t(x, same_head)

    def chunk_inputs(row0):
        rows = pl.ds(row0, CHUNK)
        lw = lw_ref[0, rows, :]
        l1 = lw.astype(BF16)
        rem = lw - l1.astype(F32)
        l2 = rem.astype(BF16)
        l3 = (rem - l2.astype(F32)).astype(BF16)
        cl = _dot(m["tri"], l1) + _dot(m["tri"], l2) + _dot(m["tri"], l3)
        e_neg = jnp.exp(-cl)
        w_end = jnp.exp(cl[CHUNK - 1:CHUNK, :])
        b_t = b_ref[0, rows, :] * e_neg
        k_t = k_ref[0, rows, :] * e_neg
        return dict(r=r_ref[0, rows, :] * jnp.exp(cl), a=(a_ref[0, rows, :] * jnp.exp(cl - lw)).astype(BF16),
                    b=b_t, k=k_t, be=(b_t * w_end).astype(BF16), ke=(k_t * w_end).astype(BF16),
                    v=v_ref[0, rows, :], w_end=w_end)

    ch = []
    for c in range(WKV_GROUP):
        ch.append(chunk_inputs((first_chunk + c) * CHUNK))
        tick()
    pairs = range(N_PAIRS)
    sls = [slice(p * PAIR, (p + 1) * PAIR) for p in pairs]
    units = [(c, p) for c in range(WKV_GROUP) for p in pairs]
    us = range(len(units))
    pick = lambda name: [ch[c][name][:, sls[p]] for c, p in units]
    at, rt, bt, kt, be, ke, vp = (pick(n) for n in ("a", "r", "b", "k", "be", "ke", "v"))
    g = [_dot(jnp.concatenate([at[u], rt[u].astype(BF16)], axis=0),
              jnp.concatenate([bd_t(bt[u]), bd_t(kt[u])], axis=1)) for u in us]
    tick()
    l_ab = [jnp.where(strict, g[u][:CHUNK, :PAIR], 0.0) for u in us]
    m_rb = [jnp.where(incl, g[u][CHUNK:, :PAIR], 0.0).astype(BF16) for u in us]
    km = [jnp.concatenate([jnp.where(strict, g[u][:CHUNK, PAIR:], 0.0),
                           jnp.where(incl, g[u][CHUNK:, PAIR:], 0.0)], axis=0).astype(BF16) for u in us]
    uy = [_dot(km[u], bd(vp[u])) for u in us]
    tick()
    uv0 = [uy[u][:CHUNK].astype(BF16) for u in us]
    t_acc = [eye_pair + l_ab[u] for u in us]
    lp = [_dot(l_ab[u].astype(BF16), bd(l_ab[u])).astype(BF16) for u in us]
    tick()
    for _ in range(4):
        prod = [_dot(lp[u], jnp.concatenate([bd(lp[u]), bd(t_acc[u])], axis=1)) for u in us]
        lp = [prod[u][:, :PAIR].astype(BF16) for u in us]
        t_acc = [t_acc[u] + prod[u][:, PAIR:] for u in us]
        tick()
    t_acc = [t_acc[u] + _dot(lp[u], bd(t_acc[u])) for u in us]
    tick()
    au = [_dot(t_acc[u].astype(BF16), jnp.concatenate([bd(at[u]), bd(uv0[u])], axis=1)) for u in us]
    tick()
    a_hat = [au[u][:, :PAIR] for u in us]
    u_v = [au[u][:, PAIR:] for u in us]
    ry = [_dot(m_rb[u], jnp.concatenate([bd(a_hat[u]), bd(u_v[u])], axis=1)) for u in us]
    tick()
    r_hat = [(rt[u] + ry[u][:, :PAIR]).astype(BF16) for u in us]
    y_v = [ry[u][:, PAIR:] + uy[u][CHUNK:] for u in us]
    phi = [_dot(a_hat[u].T.astype(BF16), be[u]) for u in us]
    tick()
    f = [_dot(jnp.concatenate([u_v[u], vp[u]], axis=0).T.astype(BF16),
              jnp.concatenate([be[u], ke[u]], axis=0)) for u in us]
    tick()
    phi = [(jnp.where(same_head, phi[u], 0.0)
            + jnp.where(m["eye_full"], ch[units[u][0]]["w_end"][:, sls[units[u][1]]], 0.0)).astype(BF16)
           for u in us]
    z = [jnp.where(m["first_head"], f[u][:HEAD_DIM, :], f[u][HEAD_DIM:, :]) for u in us]
    return r_hat, y_v, phi, z


def _wkv_recurrence_steps(slot, rh_ref, yv_ref, phi_ref, z_ref, s_ref, y_ref, m):
    pairs = range(N_PAIRS)
    state = [s_ref[p] for p in pairs]
    for c in range(TILE_CHUNKS):
        us = [c * N_PAIRS + p for p in pairs]
        ys = [_dot(rh_ref[slot, us[p]], _block_diag_t(state[p], m["same_head"])) + yv_ref[slot, us[p]]
              for p in pairs]
        state = [_dot(state[p].astype(BF16), phi_ref[slot, us[p]]) + z_ref[slot, us[p]] for p in pairs]
        y_ref[c * CHUNK:(c + 1) * CHUNK, :] = jnp.concatenate(ys, axis=1)
        yield
    for p in pairs:
        s_ref[p] = state[p]


def _wkv_prompt_kernel(nt, r_ref, k_ref, v_ref, a_ref, b_ref, lw_ref, szr_ref, bon_ref, gng_ref, gnb_ref, ones_ref,
                       o_ref, s_out, s_ref, y_ref, rh_ref, yv_ref, phi_ref, z_ref):
    s = pl.program_id(0)
    prev = jnp.maximum(s - 1, 0)
    cur_slot = s % 2

    @pl.when(s == 0)
    def _():
        for ref in (rh_ref, yv_ref, phi_ref, z_ref):
            ref[1] = jnp.zeros_like(ref[1])

    @pl.when(prev % nt == 0)
    def _():
        s_ref[...] = jnp.zeros_like(s_ref)

    m = _wkv_masks()

    def previous_tile():
        yield from _wkv_recurrence_steps(1 - cur_slot, rh_ref, yv_ref, phi_ref, z_ref, s_ref, y_ref, m)
        o_ref[0] = _group_norm_gate(y_ref[...], bon_ref[0], szr_ref[0], gng_ref[...], gnb_ref[...], ones_ref[...])

    prev_steps = previous_tile()
    for first_chunk in range(0, TILE_CHUNKS, WKV_GROUP):
        ops = _wkv_chunk_operators(r_ref, k_ref, v_ref, a_ref, b_ref, lw_ref, first_chunk, m,
                                   tick=lambda: next(prev_steps, None))
        for i in range(WKV_GROUP * N_PAIRS):
            u = first_chunk * N_PAIRS + i
            for ref, val in zip((rh_ref, yv_ref, phi_ref, z_ref), ops):
                ref[cur_slot, u] = val[i]
    _drain(prev_steps)

    @pl.when((s > 0) & (prev % nt == nt - 1))
    def _():
        for hd in range(N_HEADS):
            lo = (hd % 2) * HEAD_DIM
            s_out[0, hd] = s_ref[hd // 2][:, lo:lo + HEAD_DIM]


def _wkv_prompt(pre, p, tm):
    r, k, v, a, b, lw, szr, bon = pre
    B, T, _ = r.shape
    assert T % tm == 0 and tm == TILE_CHUNKS * CHUNK and TILE_CHUNKS % WKV_GROUP == 0, (T, tm)
    nt = T // tm
    n_tiles = B * nt
    n_units = TILE_CHUNKS * N_PAIRS
    cur = pl.BlockSpec((1, tm, D_RWKV),
                       lambda s: (jnp.minimum(s, n_tiles - 1) // nt, jnp.minimum(s, n_tiles - 1) % nt, 0))
    prv = pl.BlockSpec((1, tm, D_RWKV), lambda s: (jnp.maximum(s - 1, 0) // nt, jnp.maximum(s - 1, 0) % nt, 0))
    state = pl.BlockSpec((1, N_HEADS, HEAD_DIM, HEAD_DIM), lambda s: (jnp.maximum(s - 1, 0) // nt, 0, 0, 0))
    return pl.pallas_call(
        functools.partial(_wkv_prompt_kernel, nt),
        grid=(n_tiles + 1,),
        in_specs=[cur] * 6 + [prv] * 2 + [_const_spec((1, D_RWKV)), _const_spec((1, D_RWKV)),
                                          _const_spec((MXU_DIM, MXU_DIM))],
        out_specs=[prv, state],
        out_shape=[jax.ShapeDtypeStruct((B, T, D_RWKV), BF16),
                   jax.ShapeDtypeStruct((B, N_HEADS, HEAD_DIM, HEAD_DIM), F32)],
        scratch_shapes=[pltpu.VMEM((N_PAIRS, HEAD_DIM, PAIR), F32), pltpu.VMEM((tm, D_RWKV), F32),
                        pltpu.VMEM((2, n_units, CHUNK, PAIR), BF16), pltpu.VMEM((2, n_units, CHUNK, PAIR), F32),
                        pltpu.VMEM((2, n_units, PAIR, PAIR), BF16), pltpu.VMEM((2, n_units, CHUNK, PAIR), F32)],
        compiler_params=pltpu.CompilerParams(
            dimension_semantics=("arbitrary",), vmem_limit_bytes=VMEM_LIMIT),
        name="wkv_prompt",
    )(r, k, v, a, b, lw, szr, bon, p["gn_g"], p["gn_b"], p["ones_bd"])


def _wkv_step_kernel(r_ref, k_ref, v_ref, a_ref, b_ref, lw_ref, szr_ref, bon_ref, gng_ref, gnb_ref,
                     ones_ref, s_in, o_ref, s_out, vec_t, y_t):
    hd = pl.program_id(0)
    ns = s_in.shape[-1]

    @pl.when(hd == 0)
    def _():
        for n, ref in enumerate((a_ref, b_ref, k_ref, r_ref)):
            vec_t[n] = ref[0].T
        vec_t[4] = jnp.exp(lw_ref[0]).T
        vec_t[5] = v_ref[0].T

    base = pl.multiple_of(hd * HEAD_DIM, HEAD_DIM)
    hrows = pl.ds(base, HEAD_DIM)
    a_h, b_h, k_h, r_h, w_h = (vec_t[n, hrows, :] for n in range(5))
    sub = lax.broadcasted_iota(jnp.int32, (SUBLANES, ns), 0)

    def value_group(g, carry):
        off = pl.multiple_of(g * SUBLANES, SUBLANES)
        v_g = vec_t[5, pl.ds(base + off, SUBLANES), :]
        y_g = jnp.zeros((SUBLANES, ns), F32)
        for ii in range(SUBLANES):
            s0 = s_in[0, off + ii]
            sa = jnp.sum(s0 * a_h, axis=0, keepdims=True)
            s1 = s0 * w_h + sa * b_h + v_g[ii:ii + 1, :] * k_h
            s_out[0, off + ii] = s1
            y_g = jnp.where(sub == ii, jnp.sum(s1 * r_h, axis=0, keepdims=True), y_g)
        y_t[pl.ds(base + off, SUBLANES), :] = y_g
        return carry

    lax.fori_loop(0, HEAD_DIM // SUBLANES, value_group, 0)

    @pl.when(hd == pl.num_programs(0) - 1)
    def _():
        o_ref[0] = _group_norm_gate(y_t[...].T, bon_ref[0], szr_ref[0], gng_ref[...], gnb_ref[...], ones_ref[...])


def _wkv_sample(pre, state_t, p):
    ns = state_t.shape[-1]
    vec = pl.BlockSpec((1, ns, D_RWKV), lambda h: (0, 0, 0))
    st = pl.BlockSpec((1, HEAD_DIM, HEAD_DIM, ns), lambda h: (h, 0, 0, 0))
    return pl.pallas_call(
        _wkv_step_kernel,
        grid=(N_HEADS,),
        in_specs=[vec] * 8 + [_const_spec((1, D_RWKV)), _const_spec((1, D_RWKV)),
                              _const_spec((MXU_DIM, MXU_DIM)), st],
        out_specs=[vec, st],
        out_shape=[jax.ShapeDtypeStruct((1, ns, D_RWKV), BF16),
                   jax.ShapeDtypeStruct(state_t.shape, F32)],
        scratch_shapes=[pltpu.VMEM((6, D_RWKV, ns), F32), pltpu.VMEM((D_RWKV, ns), F32)],
        compiler_params=pltpu.CompilerParams(
            dimension_semantics=("arbitrary",), vmem_limit_bytes=VMEM_LIMIT),
        name="wkv_sample",
    )(*pre, p["gn_g"], p["gn_b"], p["ones_bd"], state_t)


def _conv_tile(u, cw_ref, cb_ref, ubuf):
    tm = u.shape[0]
    ubuf[CONV_PAD:CONV_PAD + tm, :] = u
    win = ubuf[...]
    rows = tm + CONV_PAD
    base = CONV_PAD - (CONV_W - 1)
    acc = jnp.broadcast_to(cb_ref[...], u.shape)
    for s in range(SUBLANES):
        shifted = win if s == 0 else pltpu.roll(win, shift=rows - s, axis=0)
        for kk in range(CONV_W):
            if (base + kk) % SUBLANES == s:
                q = base + kk - s
                acc = acc + cw_ref[kk:kk + 1, :] * shifted[q:q + tm, :]
    ubuf[0:CONV_PAD, :] = win[tm:tm + CONV_PAD, :]
    return acc


def _post_tail(conv, x, o_r, szc, sgr, sgc, lng_ref, lnb_ref, wbr_ref, wbc_ref, wout_ref, gpost_ref):
    cm = jnp.mean(conv, axis=-1, keepdims=True)
    d = conv - cm
    cv = jnp.mean(d * d, axis=-1, keepdims=True)
    cn = d * lax.rsqrt(cv + LN_EPS) * lng_ref[...] + lnb_ref[...]
    o_c = (cn * _sigmoid(cn)) * szc
    branch_c = _dot(o_c.astype(BF16), wbc_ref[...])
    branch_r = _dot(o_r, wbr_ref[...])
    merged = sgr * branch_r + sgc * branch_c
    out = _dot(merged.astype(BF16), wout_ref[...])
    ms = jnp.mean(out * out, axis=-1, keepdims=True)
    return x + out * lax.rsqrt(ms + RMS_EPS) * gpost_ref[...]


def _post_prompt_kernel(x_ref, or_ref, u_ref, szc_ref, sgr_ref, sgc_ref, cw_ref, cb_ref, lng_ref, lnb_ref,
                        wbr_ref, wbc_ref, wout_ref, gpost_ref, y_ref, ubuf):
    @pl.when(pl.program_id(1) == 0)
    def _():
        ubuf[0:CONV_PAD, :] = jnp.zeros((CONV_PAD, D_CONV), F32)

    conv = _conv_tile(u_ref[0], cw_ref, cb_ref, ubuf)
    y_ref[0] = _post_tail(conv, x_ref[0], or_ref[0], szc_ref[0], sgr_ref[0], sgc_ref[0],
                          lng_ref, lnb_ref, wbr_ref, wbc_ref, wout_ref, gpost_ref)


def _post_sample_kernel(x_ref, or_ref, u_ref, szc_ref, sgr_ref, sgc_ref, cw_ref, cb_ref, lng_ref, lnb_ref,
                        wbr_ref, wbc_ref, wout_ref, gpost_ref, st_ref, y_ref, st_out):
    u = u_ref[0]
    acc = jnp.broadcast_to(cb_ref[...], u.shape)
    for kk in range(CONV_W - 1):
        acc = acc + cw_ref[kk:kk + 1, :] * st_ref[kk]
    acc = acc + cw_ref[CONV_W - 1:CONV_W, :] * u
    st_out[0:CONV_W - 2] = st_ref[1:CONV_W - 1]
    st_out[CONV_W - 2] = u
    y_ref[0] = _post_tail(acc, x_ref[0], or_ref[0], szc_ref[0], sgr_ref[0], sgc_ref[0],
                          lng_ref, lnb_ref, wbr_ref, wbc_ref, wout_ref, gpost_ref)


def _post(x, o_r, u, szc, sgr, sgc, conv_state, p, tm):
    B, T, _ = x.shape
    sample = conv_state is not None
    tile = lambda w: pl.BlockSpec((1, tm, w), lambda b, j: (b, j, 0))
    in_specs = [tile(D_MODEL), tile(D_RWKV), tile(D_CONV), tile(D_CONV), tile(D_MODEL), tile(D_MODEL),
                _const_spec((CONV_W, D_CONV)), _const_spec((1, D_CONV)), _const_spec((1, D_CONV)),
                _const_spec((1, D_CONV)), _const_spec((D_RWKV, D_MODEL)), _const_spec((D_CONV, D_MODEL)),
                _const_spec((D_MODEL, D_MODEL)), _const_spec((1, D_MODEL))]
    args = [x, o_r, u, szc, sgr, sgc, p["conv_w"], p["conv_b"], p["ln_g"], p["ln_b"],
            p["w_br"], p["w_bc"], p["w_out"], p["gpost"]]
    out_specs = [tile(D_MODEL)]
    out_shape = [jax.ShapeDtypeStruct((B, T, D_MODEL), F32)]
    scratch = []
    if sample:
        st = pl.BlockSpec((CONV_W - 1, tm, D_CONV), lambda b, j: (0, j, 0))
        in_specs.append(st)
        args.append(conv_state)
        out_specs.append(st)
        out_shape.append(jax.ShapeDtypeStruct(conv_state.shape, F32))
    else:
        scratch = [pltpu.VMEM((tm + CONV_PAD, D_CONV), F32)]
    return pl.pallas_call(
        _post_sample_kernel if sample else _post_prompt_kernel,
        grid=(B, T // tm),
        in_specs=in_specs,
        out_specs=out_specs,
        out_shape=out_shape,
        scratch_shapes=scratch,
        compiler_params=pltpu.CompilerParams(
            dimension_semantics=("arbitrary", "arbitrary"), vmem_limit_bytes=VMEM_LIMIT),
        name="post_sample" if sample else "post_prompt",
    )(*args)


def _layer_params(l, norm_pre_g, w_in, mu_shift, decay_w0, decay_w2, iclr_a0, iclr_a2, k_k, k_a, r_k, gn_g, gn_b,
                  conv_glu_b, conv_w, conv_b, ln_c_g, ln_c_b, w_branch_r, w_branch_c, w_out, norm_post_g):
    row = lambda t: t[l].reshape(1, -1).astype(F32)
    zeros = jnp.zeros((LORA, D_RWKV), F32)
    w2c = jnp.concatenate([jnp.concatenate([decay_w2[l], zeros], axis=1),
                           jnp.concatenate([zeros, iclr_a2[l]], axis=1)], axis=0)
    blk = jnp.arange(MXU_DIM) // HEAD_DIM
    return dict(
        gpre=row(norm_pre_g), w_in=w_in[l].astype(BF16), mu=row(mu_shift), w0=row(decay_w0), a0=row(iclr_a0),
        w2c=w2c.astype(BF16), k_k=row(k_k), k_a=row(k_a), r_k=row(r_k), glu_b=row(conv_glu_b),
        ones_bd=(blk[:, None] == blk[None, :]).astype(BF16),
        gn_g=row(gn_g), gn_b=row(gn_b), conv_w=conv_w[l].astype(F32), conv_b=row(conv_b),
        ln_g=row(ln_c_g), ln_b=row(ln_c_b), w_br=w_branch_r[l].astype(BF16), w_bc=w_branch_c[l].astype(BF16),
        w_out=w_out[l].astype(BF16), gpost=row(norm_post_g))


def _prompt_layer(x, p, tm_prep, tm_wkv, tm_post):
    T = x.shape[1]
    *pre, u, szc, sgr, sgc, h_last = _prep(x, None, p, tm_prep)
    o_r, wkv = _wkv_prompt(pre, p, tm_wkv)
    (y,) = _post(x, o_r, u, szc, sgr, sgc, None, p, tm_post)
    return y, h_last[:, SUBLANES - 1], wkv, u[:, T - (CONV_W - 1):]


def _sample_layer(x, shift, wkv, conv, p, tm_post):
    ns = x.shape[0]
    xs = x.reshape(1, ns, D_MODEL)
    *pre, u, szc, sgr, sgc, h = _prep(xs, shift.reshape(1, ns, D_MODEL), p, ns)
    o_r, wkv_new = _wkv_sample(pre, jnp.transpose(wkv, (1, 2, 3, 0)), p)
    y, conv_new = _post(xs, o_r, u, szc, sgr, sgc, jnp.transpose(conv, (1, 0, 2)), p, tm_post)
    return (y.reshape(ns, 1, D_MODEL), h.reshape(ns, D_MODEL), jnp.transpose(wkv_new, (3, 0, 1, 2)),
            jnp.transpose(conv_new, (1, 0, 2)))


def kernel(x_prompt, x_sample, state_shift, state_wkv, state_conv, norm_pre_g, w_in, mu_shift, decay_w0, decay_w2,
           iclr_a0, iclr_a2, k_k, k_a, r_k, gn_g, gn_b, conv_glu_b, conv_w, conv_b, ln_c_g, ln_c_b, w_branch_r,
           w_branch_c, w_out, norm_post_g):
    depth = w_in.shape[0]
    xp, xs = x_prompt, x_sample
    outs = [[] for _ in range(6)]
    for l in range(depth):
        p = _layer_params(l, norm_pre_g, w_in, mu_shift, decay_w0, decay_w2, iclr_a0, iclr_a2, k_k, k_a, r_k, gn_g,
                          gn_b, conv_glu_b, conv_w, conv_b, ln_c_g, ln_c_b, w_branch_r, w_branch_c, w_out,
                          norm_post_g)
        xp, a1, a2, a3 = _prompt_layer(xp, p, tm_prep=256, tm_wkv=256, tm_post=512)
        xs, b1, b2, b3 = _sample_layer(xs, state_shift[l], state_wkv[l], state_conv[l], p, tm_post=64)
        for lst, val in zip(outs, (a1, a2, a3, b1, b2, b3)):
            lst.append(val)
    return (xp, xs) + tuple(jnp.stack(o) for o in outs)
```

```python
import functools
import math

import jax
import jax.numpy as jnp
from jax import lax
from jax.experimental import pallas as pl
from jax.experimental.pallas import tpu as pltpu

F32 = jnp.float32
BF16 = jnp.bfloat16

D_MODEL = 1024
HEAD_DIM = 64
N_HEADS = D_MODEL // HEAD_DIM
D_RWKV = N_HEADS * HEAD_DIM
D_CONV = D_MODEL // 2
CONV_W = 31
LORA = 64
RMS_EPS = 1e-6
GN_EPS = 64e-5
LN_EPS = 1e-5
N_SHIFT = 4 * D_RWKV + 2 * LORA
N_IN = N_SHIFT + 2 * D_CONV + D_CONV + 2 * D_MODEL

SUBLANES = 8
PAIR = 2 * HEAD_DIM
N_PAIRS = N_HEADS // 2
MXU_DIM = 256
CHUNK = 64
TILE_CHUNKS = 4
WKV_GROUP = 2
CONV_PAD = 32
VMEM_LIMIT = 56 * 1024 * 1024

_O_R, _O_K, _O_V, _O_ZR = 0, D_RWKV, 2 * D_RWKV, 3 * D_RWKV
_O_LORA = 4 * D_RWKV
_O_U = N_SHIFT
_O_ZC = _O_U + 2 * D_CONV
_O_GR = _O_ZC + D_CONV
_O_GC = _O_GR + D_MODEL


def _dot(a, b):
    return jnp.dot(a, b, preferred_element_type=F32)


def _sigmoid(x):
    return 1.0 / (1.0 + jnp.exp(-x))


def _head_sum(x, ones_bd):
    xb = x.astype(BF16)
    return jnp.concatenate([_dot(xb[:, c * MXU_DIM:(c + 1) * MXU_DIM], ones_bd)
                            for c in range(x.shape[1] // MXU_DIM)], axis=1)


def _group_norm_gate(y, bonus, szr, gng, gnb, ones_bd):
    mean = _head_sum(y, ones_bd) * (1.0 / HEAD_DIM)
    d = y - mean
    var = _head_sum(d * d, ones_bd) * (1.0 / HEAD_DIM)
    y_gn = d * lax.rsqrt(var + GN_EPS) * gng + gnb
    return ((y_gn + bonus) * szr).astype(BF16)


def _prep_kernel(sample, *refs):
    if sample:
        (x_ref, prev_ref, gpre_ref, win_ref, mu_ref, w0_ref, a0_ref, w2c_ref, kk_ref, ka_ref, rk_ref,
         glub_ref, ones_ref,
         r_out, k_out, v_out, a_out, b_out, lw_out, szr_out, bon_out, u_out, szc_out, sgr_out, sgc_out,
         h_out) = refs
    else:
        (x_ref, gpre_ref, win_ref, mu_ref, w0_ref, a0_ref, w2c_ref, kk_ref, ka_ref, rk_ref,
         glub_ref, ones_ref,
         r_out, k_out, v_out, a_out, b_out, lw_out, szr_out, bon_out, u_out, szc_out, sgr_out, sgc_out,
         h_out, carry_ref) = refs

    x = x_ref[0]
    tm = x.shape[0]
    ms = jnp.mean(x * x, axis=-1, keepdims=True)
    h = x * lax.rsqrt(ms + RMS_EPS) * gpre_ref[...]
    hb = h.astype(BF16)

    if sample:
        h_out[0] = h
        pb = prev_ref[0].astype(BF16)
    else:
        h_out[0] = h[tm - SUBLANES:, :]

        @pl.when(pl.program_id(1) == 0)
        def _():
            carry_ref[...] = jnp.zeros_like(carry_ref)
        row0 = lax.broadcasted_iota(jnp.int32, (tm, 1), 0) == 0

    def shifted(off, width):
        p = _dot(hb, win_ref[:, off:off + width])
        if sample:
            pp = _dot(pb, win_ref[:, off:off + width])
        else:
            rolled = pltpu.roll(p, shift=1, axis=0)
            pp = jnp.where(row0, carry_ref[0:1, off:off + width], rolled)
            carry_ref[0:1, off:off + width] = p[tm - 1:tm, :]
        return p + (pp - p) * mu_ref[:, off:off + width]

    ones_bd = ones_ref[...]

    lo = shifted(_O_LORA, 2 * LORA)
    lane = lax.broadcasted_iota(jnp.int32, lo.shape, 1)
    lo = jnp.where(lane < LORA, jnp.tanh(lo), lo)
    lor = _dot(lo.astype(BF16), w2c_ref[...])
    w_raw = w0_ref[...] + lor[:, :D_RWKV]
    lw_out[0] = (-math.exp(-0.5)) * _sigmoid(w_raw)
    iclr = _sigmoid(a0_ref[...] + lor[:, D_RWKV:])

    k = shifted(_O_K, D_RWKV)
    kk = k * kk_ref[...]
    kk = kk * lax.rsqrt(_head_sum(kk * kk, ones_bd) + 1e-12)
    k2 = k * (1.0 + (iclr - 1.0) * ka_ref[...])
    k_out[0] = k2
    a_out[0] = -kk
    b_out[0] = kk * iclr

    zr = shifted(_O_ZR, D_RWKV)
    szr_out[0] = zr * _sigmoid(zr)

    u_in = _dot(hb, win_ref[:, _O_U:_O_U + 2 * D_CONV]) + glub_ref[...]
    u_out[0] = u_in[:, :D_CONV] * _sigmoid(u_in[:, D_CONV:])
    zc = _dot(hb, win_ref[:, _O_ZC:_O_ZC + D_CONV])
    szc_out[0] = zc * _sigmoid(zc)
    sgr_out[0] = _sigmoid(_dot(hb, win_ref[:, _O_GR:_O_GR + D_MODEL]))
    sgc_out[0] = _sigmoid(_dot(hb, win_ref[:, _O_GC:_O_GC + D_MODEL]))

    r = shifted(_O_R, D_RWKV)
    r_out[0] = r
    bsum = _head_sum(r * k2 * rk_ref[...], ones_bd)
    v = shifted(_O_V, D_RWKV)
    v_out[0] = v
    bon_out[0] = bsum * v


def _const_spec(shape):
    nd = len(shape)
    return pl.BlockSpec(shape, lambda *_: (0,) * nd, pipeline_mode=pl.Buffered(1))


def _prep(x, prev, p, tm):
    B, T, _ = x.shape
    sample = prev is not None
    tile = lambda w: pl.BlockSpec((1, tm, w), lambda b, j: (b, j, 0))
    in_specs = [tile(D_MODEL)] + ([tile(D_MODEL)] if sample else []) + [
        _const_spec((1, D_MODEL)), _const_spec((D_MODEL, N_IN)), _const_spec((1, N_SHIFT)),
        _const_spec((1, D_RWKV)), _const_spec((1, D_RWKV)), _const_spec((2 * LORA, 2 * D_RWKV)),
        _const_spec((1, D_RWKV)), _const_spec((1, D_RWKV)), _const_spec((1, D_RWKV)),
        _const_spec((1, 2 * D_CONV)), _const_spec((MXU_DIM, MXU_DIM))]
    widths = [D_RWKV] * 8 + [D_CONV, D_CONV, D_MODEL, D_MODEL]
    h_spec = tile(D_MODEL) if sample else pl.BlockSpec((1, SUBLANES, D_MODEL), lambda b, j: (b, 0, 0))
    out_specs = [tile(w) for w in widths] + [h_spec]
    out_shape = [jax.ShapeDtypeStruct((B, T, w), F32) for w in widths] + [
        jax.ShapeDtypeStruct((B, T if sample else SUBLANES, D_MODEL), F32)]
    args = [x] + ([prev] if sample else []) + [
        p["gpre"], p["w_in"], p["mu"], p["w0"], p["a0"], p["w2c"], p["k_k"], p["k_a"], p["r_k"],
        p["glu_b"], p["ones_bd"]]
    return pl.pallas_call(
        functools.partial(_prep_kernel, sample),
        grid=(B, T // tm),
        in_specs=in_specs,
        out_specs=out_specs,
        out_shape=out_shape,
        scratch_shapes=[] if sample else [pltpu.VMEM((SUBLANES, N_SHIFT), F32)],
        compiler_params=pltpu.CompilerParams(
            dimension_semantics=("arbitrary", "arbitrary"), vmem_limit_bytes=VMEM_LIMIT),
        name="prep_sample" if sample else "prep_prompt",
    )(*args)


def _block_diag(x, same_head):
    return jnp.where(same_head, jnp.concatenate([x, x], axis=0), jnp.zeros((), x.dtype))


def _drain(steps):
    for _ in steps:
        pass


def _wkv_masks():
    row = lax.broadcasted_iota(jnp.int32, (CHUNK, PAIR), 0)
    col = lax.broadcasted_iota(jnp.int32, (CHUNK, PAIR), 1) % HEAD_DIM
    brow = lax.broadcasted_iota(jnp.int32, (PAIR, PAIR), 0)
    bcol = lax.broadcasted_iota(jnp.int32, (PAIR, PAIR), 1)
    trow = lax.broadcasted_iota(jnp.int32, (CHUNK, CHUNK), 0)
    tcol = lax.broadcasted_iota(jnp.int32, (CHUNK, CHUNK), 1)
    return dict(strict=row > col, incl=row >= col, eye_pair=(row == col).astype(F32),
                same_head=(brow // HEAD_DIM) == (bcol // HEAD_DIM), eye_full=brow == bcol,
                first_head=lax.broadcasted_iota(jnp.int32, (CHUNK, PAIR), 1) < HEAD_DIM,
                tri=(trow >= tcol).astype(BF16))


def _block_diag_t(x, same_head):
    return jnp.where(same_head, jnp.concatenate([x, x], axis=0).T, 0.0).astype(BF16)


def _wkv_chunk_operators(r_ref, k_ref, v_ref, a_ref, b_ref, lw_ref, first_chunk, m, tick):
    strict, incl, eye_pair, same_head = m["strict"], m["incl"], m["eye_pair"], m["same_head"]
    bd = lambda x: _block_diag(x.astype(BF16), same_head)
    bd_t = lambda x: _block_diag_t(x, same_head)

    def chunk_inputs(row0):
        rows = pl.ds(row0, CHUNK)
        lw = lw_ref[0, rows, :]
        l1 = lw.astype(BF16)
        rem = lw - l1.astype(F32)
        l2 = rem.astype(BF16)
        l3 = (rem - l2.astype(F32)).astype(BF16)
        cl = _dot(m["tri"], l1) + _dot(m["tri"], l2) + _dot(m["tri"], l3)
        e_neg = jnp.exp(-cl)
        w_end = jnp.exp(cl[CHUNK - 1:CHUNK, :])
        b_t = b_ref[0, rows, :] * e_neg
        k_t = k_ref[0, rows, :] * e_neg
        return dict(r=r_ref[0, rows, :] * jnp.exp(cl), a=(a_ref[0, rows, :] * jnp.exp(cl - lw)).astype(BF16),
                    b=b_t, k=k_t, be=(b_t * w_end).astype(BF16), ke=(k_t * w_end).astype(BF16),
                    v=v_ref[0, rows, :], w_end=w_end)

    ch = []
    for c in range(WKV_GROUP):
        ch.append(chunk_inputs((first_chunk + c) * CHUNK))
        tick()
    pairs = range(N_PAIRS)
    sls = [slice(p * PAIR, (p + 1) * PAIR) for p in pairs]
    units = [(c, p) for c in range(WKV_GROUP) for p in pairs]
    us = range(len(units))
    pick = lambda name: [ch[c][name][:, sls[p]] for c, p in units]
    at, rt, bt, kt, be, ke, vp = (pick(n) for n in ("a", "r", "b", "k", "be", "ke", "v"))
    g = [_dot(jnp.concatenate([at[u], rt[u].astype(BF16)], axis=0),
              jnp.concatenate([bd_t(bt[u]), bd_t(kt[u])], axis=1)) for u in us]
    tick()
    l_ab = [jnp.where(strict, g[u][:CHUNK, :PAIR], 0.0) for u in us]
    m_rb = [jnp.where(incl, g[u][CHUNK:, :PAIR], 0.0).astype(BF16) for u in us]
    km = [jnp.concatenate([jnp.where(strict, g[u][:CHUNK, PAIR:], 0.0),
                           jnp.where(incl, g[u][CHUNK:, PAIR:], 0.0)], axis=0).astype(BF16) for u in us]
    uy = [_dot(km[u], bd(vp[u])) for u in us]
    tick()
    uv0 = [uy[u][:CHUNK].astype(BF16) for u in us]
    t_acc = [eye_pair + l_ab[u] for u in us]
    lp = [_dot(l_ab[u].astype(BF16), bd(l_ab[u])).astype(BF16) for u in us]
    tick()
    for _ in range(4):
        prod = [_dot(lp[u], jnp.concatenate([bd(lp[u]), bd(t_acc[u])], axis=1)) for u in us]
        lp = [prod[u][:, :PAIR].astype(BF16) for u in us]
        t_acc = [t_acc[u] + prod[u][:, PAIR:] for u in us]
        tick()
    t_acc = [t_acc[u] + _dot(lp[u], bd(t_acc[u])) for u in us]
    tick()
    au = [_dot(t_acc[u].astype(BF16), jnp.concatenate([bd(at[u]), bd(uv0[u])], axis=1)) for u in us]
    tick()
    a_hat = [au[u][:, :PAIR] for u in us]
    u_v = [au[u][:, PAIR:] for u in us]
    ry = [_dot(m_rb[u], jnp.concatenate([bd(a_hat[u]), bd(u_v[u])], axis=1)) for u in us]
    tick()
    r_hat = [(rt[u] + ry[u][:, :PAIR]).astype(BF16) for u in us]
    y_v = [ry[u][:, PAIR:] + uy[u][CHUNK:] for u in us]
    phi = [_dot(a_hat[u].T.astype(BF16), be[u]) for u in us]
    tick()
    f = [_dot(jnp.concatenate([u_v[u], vp[u]], axis=0).T.astype(BF16),
              jnp.concatenate([be[u], ke[u]], axis=0)) for u in us]
    tick()
    phi = [(jnp.where(same_head, phi[u], 0.0)
            + jnp.where(m["eye_full"], ch[units[u][0]]["w_end"][:, sls[units[u][1]]], 0.0)).astype(BF16)
           for u in us]
    z = [jnp.where(m["first_head"], f[u][:HEAD_DIM, :], f[u][HEAD_DIM:, :]) for u in us]
    return r_hat, y_v, phi, z


def _wkv_recurrence_steps(slot, rh_ref, yv_ref, phi_ref, z_ref, s_ref, y_ref, m):
    pairs = range(N_PAIRS)
    state = [s_ref[p] for p in pairs]
    for c in range(TILE_CHUNKS):
        us = [c * N_PAIRS + p for p in pairs]
        ys = [_dot(rh_ref[slot, us[p]], _block_diag_t(state[p], m["same_head"])) + yv_ref[slot, us[p]]
              for p in pairs]
        state = [_dot(state[p].astype(BF16), phi_ref[slot, us[p]]) + z_ref[slot, us[p]] for p in pairs]
        y_ref[c * CHUNK:(c + 1) * CHUNK, :] = jnp.concatenate(ys, axis=1)
        yield
    for p in pairs:
        s_ref[p] = state[p]


def _wkv_prompt_kernel(nt, r_ref, k_ref, v_ref, a_ref, b_ref, lw_ref, szr_ref, bon_ref, gng_ref, gnb_ref, ones_ref,
                       o_ref, s_out, s_ref, y_ref, rh_ref, yv_ref, phi_ref, z_ref):
    s = pl.program_id(0)
    prev = jnp.maximum(s - 1, 0)
    cur_slot = s % 2

    @pl.when(s == 0)
    def _():
        for ref in (rh_ref, yv_ref, phi_ref, z_ref):
            ref[1] = jnp.zeros_like(ref[1])

    @pl.when(prev % nt == 0)
    def _():
        s_ref[...] = jnp.zeros_like(s_ref)

    m = _wkv_masks()

    def previous_tile():
        yield from _wkv_recurrence_steps(1 - cur_slot, rh_ref, yv_ref, phi_ref, z_ref, s_ref, y_ref, m)
        o_ref[0] = _group_norm_gate(y_ref[...], bon_ref[0], szr_ref[0], gng_ref[...], gnb_ref[...], ones_ref[...])

    prev_steps = previous_tile()
    for first_chunk in range(0, TILE_CHUNKS, WKV_GROUP):
        ops = _wkv_chunk_operators(r_ref, k_ref, v_ref, a_ref, b_ref, lw_ref, first_chunk, m,
                                   tick=lambda: next(prev_steps, None))
        for i in range(WKV_GROUP * N_PAIRS):
            u = first_chunk * N_PAIRS + i
            for ref, val in zip((rh_ref, yv_ref, phi_ref, z_ref), ops):
                ref[cur_slot, u] = val[i]
    _drain(prev_steps)

    @pl.when((s > 0) & (prev % nt == nt - 1))
    def _():
        for hd in range(N_HEADS):
            lo = (hd % 2) * HEAD_DIM
            s_out[0, hd] = s_ref[hd // 2][:, lo:lo + HEAD_DIM]


def _wkv_prompt(pre, p, tm):
    r, k, v, a, b, lw, szr, bon = pre
    B, T, _ = r.shape
    assert T % tm == 0 and tm == TILE_CHUNKS * CHUNK and TILE_CHUNKS % WKV_GROUP == 0, (T, tm)
    nt = T // tm
    n_tiles = B * nt
    n_units = TILE_CHUNKS * N_PAIRS
    cur = pl.BlockSpec((1, tm, D_RWKV),
                       lambda s: (jnp.minimum(s, n_tiles - 1) // nt, jnp.minimum(s, n_tiles - 1) % nt, 0))
    prv = pl.BlockSpec((1, tm, D_RWKV), lambda s: (jnp.maximum(s - 1, 0) // nt, jnp.maximum(s - 1, 0) % nt, 0))
    state = pl.BlockSpec((1, N_HEADS, HEAD_DIM, HEAD_DIM), lambda s: (jnp.maximum(s - 1, 0) // nt, 0, 0, 0))
    return pl.pallas_call(
        functools.partial(_wkv_prompt_kernel, nt),
        grid=(n_tiles + 1,),
        in_specs=[cur] * 6 + [prv] * 2 + [_const_spec((1, D_RWKV)), _const_spec((1, D_RWKV)),
                                          _const_spec((MXU_DIM, MXU_DIM))],
        out_specs=[prv, state],
        out_shape=[jax.ShapeDtypeStruct((B, T, D_RWKV), BF16),
                   jax.ShapeDtypeStruct((B, N_HEADS, HEAD_DIM, HEAD_DIM), F32)],
        scratch_shapes=[pltpu.VMEM((N_PAIRS, HEAD_DIM, PAIR), F32), pltpu.VMEM((tm, D_RWKV), F32),
                        pltpu.VMEM((2, n_units, CHUNK, PAIR), BF16), pltpu.VMEM((2, n_units, CHUNK, PAIR), F32),
                        pltpu.VMEM((2, n_units, PAIR, PAIR), BF16), pltpu.VMEM((2, n_units, CHUNK, PAIR), F32)],
        compiler_params=pltpu.CompilerParams(
            dimension_semantics=("arbitrary",), vmem_limit_bytes=VMEM_LIMIT),
        name="wkv_prompt",
    )(r, k, v, a, b, lw, szr, bon, p["gn_g"], p["gn_b"], p["ones_bd"])


def _wkv_step_kernel(r_ref, k_ref, v_ref, a_ref, b_ref, lw_ref, szr_ref, bon_ref, gng_ref, gnb_ref,
                     ones_ref, s_in, o_ref, s_out, vec_t, y_t):
    hd = pl.program_id(0)
    ns = s_in.shape[-1]

    @pl.when(hd == 0)
    def _():
        for n, ref in enumerate((a_ref, b_ref, k_ref, r_ref)):
            vec_t[n] = ref[0].T
        vec_t[4] = jnp.exp(lw_ref[0]).T
        vec_t[5] = v_ref[0].T

    base = pl.multiple_of(hd * HEAD_DIM, HEAD_DIM)
    hrows = pl.ds(base, HEAD_DIM)
    a_h, b_h, k_h, r_h, w_h = (vec_t[n, hrows, :] for n in range(5))
    sub = lax.broadcasted_iota(jnp.int32, (SUBLANES, ns), 0)

    def value_group(g, carry):
        off = pl.multiple_of(g * SUBLANES, SUBLANES)
        v_g = vec_t[5, pl.ds(base + off, SUBLANES), :]
        y_g = jnp.zeros((SUBLANES, ns), F32)
        for ii in range(SUBLANES):
            s0 = s_in[0, off + ii]
            sa = jnp.sum(s0 * a_h, axis=0, keepdims=True)
            s1 = s0 * w_h + sa * b_h + v_g[ii:ii + 1, :] * k_h
            s_out[0, off + ii] = s1
            y_g = jnp.where(sub == ii, jnp.sum(s1 * r_h, axis=0, keepdims=True), y_g)
        y_t[pl.ds(base + off, SUBLANES), :] = y_g
        return carry

    lax.fori_loop(0, HEAD_DIM // SUBLANES, value_group, 0)

    @pl.when(hd == pl.num_programs(0) - 1)
    def _():
        o_ref[0] = _group_norm_gate(y_t[...].T, bon_ref[0], szr_ref[0], gng_ref[...], gnb_ref[...], ones_ref[...])


def _wkv_sample(pre, state_t, p):
    ns = state_t.shape[-1]
    vec = pl.BlockSpec((1, ns, D_RWKV), lambda h: (0, 0, 0))
    st = pl.BlockSpec((1, HEAD_DIM, HEAD_DIM, ns), lambda h: (h, 0, 0, 0))
    return pl.pallas_call(
        _wkv_step_kernel,
        grid=(N_HEADS,),
        in_specs=[vec] * 8 + [_const_spec((1, D_RWKV)), _const_spec((1, D_RWKV)),
                              _const_spec((MXU_DIM, MXU_DIM)), st],
        out_specs=[vec, st],
        out_shape=[jax.ShapeDtypeStruct((1, ns, D_RWKV), BF16),
                   jax.ShapeDtypeStruct(state_t.shape, F32)],
        scratch_shapes=[pltpu.VMEM((6, D_RWKV, ns), F32), pltpu.VMEM((D_RWKV, ns), F32)],
        compiler_params=pltpu.CompilerParams(
            dimension_semantics=("arbitrary",), vmem_limit_bytes=VMEM_LIMIT),
        name="wkv_sample",
    )(*pre, p["gn_g"], p["gn_b"], p["ones_bd"], state_t)


def _conv_tile(u, cw_ref, cb_ref, ubuf):
    tm = u.shape[0]
    ubuf[CONV_PAD:CONV_PAD + tm, :] = u
    win = ubuf[...]
    rows = tm + CONV_PAD
    base = CONV_PAD - (CONV_W - 1)
    acc = jnp.broadcast_to(cb_ref[...], u.shape)
    for s in range(SUBLANES):
        shifted = win if s == 0 else pltpu.roll(win, shift=rows - s, axis=0)
        for kk in range(CONV_W):
            if (base + kk) % SUBLANES == s:
                q = base + kk - s
                acc = acc + cw_ref[kk:kk + 1, :] * shifted[q:q + tm, :]
    ubuf[0:CONV_PAD, :] = win[tm:tm + CONV_PAD, :]
    return acc


def _post_tail(conv, x, o_r, szc, sgr, sgc, lng_ref, lnb_ref, wbr_ref, wbc_ref, wout_ref, gpost_ref):
    cm = jnp.mean(conv, axis=-1, keepdims=True)
    d = conv - cm
    cv = jnp.mean(d * d, axis=-1, keepdims=True)
    cn = d * lax.rsqrt(cv + LN_EPS) * lng_ref[...] + lnb_ref[...]
    o_c = (cn * _sigmoid(cn)) * szc
    branch_c = _dot(o_c.astype(BF16), wbc_ref[...])
    branch_r = _dot(o_r, wbr_ref[...])
    merged = sgr * branch_r + sgc * branch_c
    out = _dot(merged.astype(BF16), wout_ref[...])
    ms = jnp.mean(out * out, axis=-1, keepdims=True)
    return x + out * lax.rsqrt(ms + RMS_EPS) * gpost_ref[...]


def _post_prompt_kernel(x_ref, or_ref, u_ref, szc_ref, sgr_ref, sgc_ref, cw_ref, cb_ref, lng_ref, lnb_ref,
                        wbr_ref, wbc_ref, wout_ref, gpost_ref, y_ref, ubuf):
    @pl.when(pl.program_id(1) == 0)
    def _():
        ubuf[0:CONV_PAD, :] = jnp.zeros((CONV_PAD, D_CONV), F32)

    conv = _conv_tile(u_ref[0], cw_ref, cb_ref, ubuf)
    y_ref[0] = _post_tail(conv, x_ref[0], or_ref[0], szc_ref[0], sgr_ref[0], sgc_ref[0],
                          lng_ref, lnb_ref, wbr_ref, wbc_ref, wout_ref, gpost_ref)


def _post_sample_kernel(x_ref, or_ref, u_ref, szc_ref, sgr_ref, sgc_ref, cw_ref, cb_ref, lng_ref, lnb_ref,
                        wbr_ref, wbc_ref, wout_ref, gpost_ref, st_ref, y_ref, st_out):
    u = u_ref[0]
    acc = jnp.broadcast_to(cb_ref[...], u.shape)
    for kk in range(CONV_W - 1):
        acc = acc + cw_ref[kk:kk + 1, :] * st_ref[kk]
    acc = acc + cw_ref[CONV_W - 1:CONV_W, :] * u
    st_out[0:CONV_W - 2] = st_ref[1:CONV_W - 1]
    st_out[CONV_W - 2] = u
    y_ref[0] = _post_tail(acc, x_ref[0], or_ref[0], szc_ref[0], sgr_ref[0], sgc_ref[0],
                          lng_ref, lnb_ref, wbr_ref, wbc_ref, wout_ref, gpost_ref)


def _post(x, o_r, u, szc, sgr, sgc, conv_state, p, tm):
    B, T, _ = x.shape
    sample = conv_state is not None
    tile = lambda w: pl.BlockSpec((1, tm, w), lambda b, j: (b, j, 0))
    in_specs = [tile(D_MODEL), tile(D_RWKV), tile(D_CONV), tile(D_CONV), tile(D_MODEL), tile(D_MODEL),
                _const_spec((CONV_W, D_CONV)), _const_spec((1, D_CONV)), _const_spec((1, D_CONV)),
                _const_spec((1, D_CONV)), _const_spec((D_RWKV, D_MODEL)), _const_spec((D_CONV, D_MODEL)),
                _const_spec((D_MODEL, D_MODEL)), _const_spec((1, D_MODEL))]
    args = [x, o_r, u, szc, sgr, sgc, p["conv_w"], p["conv_b"], p["ln_g"], p["ln_b"],
            p["w_br"], p["w_bc"], p["w_out"], p["gpost"]]
    out_specs = [tile(D_MODEL)]
    out_shape = [jax.ShapeDtypeStruct((B, T, D_MODEL), F32)]
    scratch = []
    if sample:
        st = pl.BlockSpec((CONV_W - 1, tm, D_CONV), lambda b, j: (0, j, 0))
        in_specs.append(st)
        args.append(conv_state)
        out_specs.append(st)
        out_shape.append(jax.ShapeDtypeStruct(conv_state.shape, F32))
    else:
        scratch = [pltpu.VMEM((tm + CONV_PAD, D_CONV), F32)]
    return pl.pallas_call(
        _post_sample_kernel if sample else _post_prompt_kernel,
        grid=(B, T // tm),
        in_specs=in_specs,
        out_specs=out_specs,
        out_shape=out_shape,
        scratch_shapes=scratch,
        compiler_params=pltpu.CompilerParams(
            dimension_semantics=("arbitrary", "arbitrary"), vmem_limit_bytes=VMEM_LIMIT),
        name="post_sample" if sample else "post_prompt",
    )(*args)


def _layer_params(l, norm_pre_g, w_in, mu_shift, decay_w0, decay_w2, iclr_a0, iclr_a2, k_k, k_a, r_k, gn_g, gn_b,
                  conv_glu_b, conv_w, conv_b, ln_c_g, ln_c_b, w_branch_r, w_branch_c, w_out, norm_post_g):
    row = lambda t: t[l].reshape(1, -1).astype(F32)
    zeros = jnp.zeros((LORA, D_RWKV), F32)
    w2c = jnp.concatenate([jnp.concatenate([decay_w2[l], zeros], axis=1),
                           jnp.concatenate([zeros, iclr_a2[l]], axis=1)], axis=0)
    blk = jnp.arange(MXU_DIM) // HEAD_DIM
    return dict(
        gpre=row(norm_pre_g), w_in=w_in[l].astype(BF16), mu=row(mu_shift), w0=row(decay_w0), a0=row(iclr_a0),
        w2c=w2c.astype(BF16), k_k=row(k_k), k_a=row(k_a), r_k=row(r_k), glu_b=row(conv_glu_b),
        ones_bd=(blk[:, None] == blk[None, :]).astype(BF16),
        gn_g=row(gn_g), gn_b=row(gn_b), conv_w=conv_w[l].astype(F32), conv_b=row(conv_b),
        ln_g=row(ln_c_g), ln_b=row(ln_c_b), w_br=w_branch_r[l].astype(BF16), w_bc=w_branch_c[l].astype(BF16),
        w_out=w_out[l].astype(BF16), gpost=row(norm_post_g))


def _prompt_layer(x, p, tm_prep, tm_wkv, tm_post):
    T = x.shape[1]
    *pre, u, szc, sgr, sgc, h_last = _prep(x, None, p, tm_prep)
    o_r, wkv = _wkv_prompt(pre, p, tm_wkv)
    (y,) = _post(x, o_r, u, szc, sgr, sgc, None, p, tm_post)
    return y, h_last[:, SUBLANES - 1], wkv, u[:, T - (CONV_W - 1):]


def _sample_layer(x, shift, wkv, conv, p, tm_post):
    ns = x.shape[0]
    xs = x.reshape(1, ns, D_MODEL)
    *pre, u, szc, sgr, sgc, h = _prep(xs, shift.reshape(1, ns, D_MODEL), p, ns)
    o_r, wkv_new = _wkv_sample(pre, jnp.transpose(wkv, (1, 2, 3, 0)), p)
    y, conv_new = _post(xs, o_r, u, szc, sgr, sgc, jnp.transpose(conv, (1, 0, 2)), p, tm_post)
    return (y.reshape(ns, 1, D_MODEL), h.reshape(ns, D_MODEL), jnp.transpose(wkv_new, (3, 0, 1, 2)),
            jnp.transpose(conv_new, (1, 0, 2)))


def kernel(x_prompt, x_sample, state_shift, state_wkv, state_conv, norm_pre_g, w_in, mu_shift, decay_w0, decay_w2,
           iclr_a0, iclr_a2, k_k, k_a, r_k, gn_g, gn_b, conv_glu_b, conv_w, conv_b, ln_c_g, ln_c_b, w_branch_r,
           w_branch_c, w_out, norm_post_g):
    depth = w_in.shape[0]
    xp, xs = x_prompt, x_sample
    outs = [[] for _ in range(6)]
    for l in range(depth):
        p = _layer_params(l, norm_pre_g, w_in, mu_shift, decay_w0, decay_w2, iclr_a0, iclr_a2, k_k, k_a, r_k, gn_g,
                          gn_b, conv_glu_b, conv_w, conv_b, ln_c_g, ln_c_b, w_branch_r, w_branch_c, w_out,
                          norm_post_g)
        xp, a1, a2, a3 = _prompt_layer(xp, p, tm_prep=256, tm_wkv=256, tm_post=512)
        xs, b1, b2, b3 = _sample_layer(xs, state_shift[l], state_wkv[l], state_conv[l], p, tm_post=64)
        for lst, val in zip(outs, (a1, a2, a3, b1, b2, b3)):
            lst.append(val)
    return (xp, xs) + tuple(jnp.stack(o) for o in outs)
```

```python
import functools
import math

import jax
import jax.numpy as jnp
from jax import lax
from jax.experimental import pallas as pl
from jax.experimental.pallas import tpu as pltpu

F32 = jnp.float32
BF16 = jnp.bfloat16

D_MODEL = 1024
HEAD_DIM = 64
N_HEADS = D_MODEL // HEAD_DIM
D_RWKV = N_HEADS * HEAD_DIM
D_CONV = D_MODEL // 2
CONV_W = 31
LORA = 64
RMS_EPS = 1e-6
GN_EPS = 64e-5
LN_EPS = 1e-5
N_SHIFT = 4 * D_RWKV + 2 * LORA
N_IN = N_SHIFT + 2 * D_CONV + D_CONV + 2 * D_MODEL

SUBLANES = 8
PAIR = 2 * HEAD_DIM
N_PAIRS = N_HEADS // 2
MXU_DIM = 256
CHUNK = 64
TILE_CHUNKS = 4
WKV_GROUP = 2
CONV_PAD = 32
VMEM_LIMIT = 56 * 1024 * 1024

_O_R, _O_K, _O_V, _O_ZR = 0, D_RWKV, 2 * D_RWKV, 3 * D_RWKV
_O_LORA = 4 * D_RWKV
_O_U = N_SHIFT
_O_ZC = _O_U + 2 * D_CONV
_O_GR = _O_ZC + D_CONV
_O_GC = _O_GR + D_MODEL


def _dot(a, b):
    return jnp.dot(a, b, preferred_element_type=F32)


def _sigmoid(x):
    return 1.0 / (1.0 + jnp.exp(-x))


def _head_sum(x, ones_bd):
    xb = x.astype(BF16)
    return jnp.concatenate([_dot(xb[:, c * MXU_DIM:(c + 1) * MXU_DIM], ones_bd)
                            for c in range(x.shape[1] // MXU_DIM)], axis=1)


def _group_norm_gate(y, bonus, szr, gng, gnb, ones_bd):
    mean = _head_sum(y, ones_bd) * (1.0 / HEAD_DIM)
    d = y - mean
    var = _head_sum(d * d, ones_bd) * (1.0 / HEAD_DIM)
    y_gn = d * lax.rsqrt(var + GN_EPS) * gng + gnb
    return ((y_gn + bonus) * szr).astype(BF16)


def _prep_kernel(sample, *refs):
    if sample:
        (x_ref, prev_ref, gpre_ref, win_ref, mu_ref, w0_ref, a0_ref, w2c_ref, kk_ref, ka_ref, rk_ref,
         glub_ref, ones_ref,
         r_out, k_out, v_out, a_out, b_out, lw_out, szr_out, bon_out, u_out, szc_out, sgr_out, sgc_out,
         h_out) = refs
    else:
        (x_ref, gpre_ref, win_ref, mu_ref, w0_ref, a0_ref, w2c_ref, kk_ref, ka_ref, rk_ref,
         glub_ref, ones_ref,
         r_out, k_out, v_out, a_out, b_out, lw_out, szr_out, bon_out, u_out, szc_out, sgr_out, sgc_out,
         h_out, carry_ref) = refs

    x = x_ref[0]
    tm = x.shape[0]
    ms = jnp.mean(x * x, axis=-1, keepdims=True)
    h = x * lax.rsqrt(ms + RMS_EPS) * gpre_ref[...]
    hb = h.astype(BF16)

    def proj(off, width):
        return _dot(hb, win_ref[:, off:off + width])

    if sample:
        h_out[0] = h
        pb = prev_ref[0].astype(BF16)
    else:
        h_out[0] = h[tm - SUBLANES:, :]

        @pl.when(pl.program_id(1) == 0)
        def _():
            carry_ref[...] = jnp.zeros_like(carry_ref)
        row0 = lax.broadcasted_iota(jnp.int32, (tm, 1), 0) == 0

    def shifted(off, width):
        p = proj(off, width)
        if sample:
            pp = _dot(pb, win_ref[:, off:off + width])
        else:
            rolled = pltpu.roll(p, shift=1, axis=0)
            pp = jnp.where(row0, carry_ref[0:1, off:off + width], rolled)
            carry_ref[0:1, off:off + width] = p[tm - 1:tm, :]
        return p + (pp - p) * mu_ref[:, off:off + width]

    ones_bd = ones_ref[...]

    lo = shifted(_O_LORA, 2 * LORA)
    k = shifted(_O_K, D_RWKV)

    lane = lax.broadcasted_iota(jnp.int32, lo.shape, 1)
    lo = jnp.where(lane < LORA, jnp.tanh(lo), lo)
    lor = _dot(lo.astype(BF16), w2c_ref[...])

    zr = shifted(_O_ZR, D_RWKV)
    szr_out[0] = zr * _sigmoid(zr)

    kk = k * kk_ref[...]
    kk_ss = _head_sum(kk * kk, ones_bd)

    u_in = proj(_O_U, 2 * D_CONV) + glub_ref[...]
    u_out[0] = u_in[:, :D_CONV] * _sigmoid(u_in[:, D_CONV:])

    w_raw = w0_ref[...] + lor[:, :D_RWKV]
    lw_out[0] = (-math.exp(-0.5)) * _sigmoid(w_raw)
    iclr = _sigmoid(a0_ref[...] + lor[:, D_RWKV:])
    kk = kk * lax.rsqrt(kk_ss + 1e-12)
    k2 = k * (1.0 + (iclr - 1.0) * ka_ref[...])
    k_out[0] = k2
    a_out[0] = -kk
    b_out[0] = kk * iclr

    zc = proj(_O_ZC, D_CONV)
    szc_out[0] = zc * _sigmoid(zc)
    sgr_out[0] = _sigmoid(proj(_O_GR, D_MODEL))
    sgc_out[0] = _sigmoid(proj(_O_GC, D_MODEL))

    r = shifted(_O_R, D_RWKV)
    r_out[0] = r
    v = shifted(_O_V, D_RWKV)
    v_out[0] = v
    bon_out[0] = _head_sum(r * k2 * rk_ref[...], ones_bd) * v


def _const_spec(shape):
    nd = len(shape)
    return pl.BlockSpec(shape, lambda *_: (0,) * nd, pipeline_mode=pl.Buffered(1))


def _prep(x, prev, p, tm):
    B, T, _ = x.shape
    sample = prev is not None
    tile = lambda w: pl.BlockSpec((1, tm, w), lambda b, j: (b, j, 0))
    in_specs = [tile(D_MODEL)] + ([tile(D_MODEL)] if sample else []) + [
        _const_spec((1, D_MODEL)), _const_spec((D_MODEL, N_IN)), _const_spec((1, N_SHIFT)),
        _const_spec((1, D_RWKV)), _const_spec((1, D_RWKV)), _const_spec((2 * LORA, 2 * D_RWKV)),
        _const_spec((1, D_RWKV)), _const_spec((1, D_RWKV)), _const_spec((1, D_RWKV)),
        _const_spec((1, 2 * D_CONV)), _const_spec((MXU_DIM, MXU_DIM))]
    widths = [D_RWKV] * 8 + [D_CONV, D_CONV, D_MODEL, D_MODEL]
    h_spec = tile(D_MODEL) if sample else pl.BlockSpec((1, SUBLANES, D_MODEL), lambda b, j: (b, 0, 0))
    out_specs = [tile(w) for w in widths] + [h_spec]
    out_shape = [jax.ShapeDtypeStruct((B, T, w), F32) for w in widths] + [
        jax.ShapeDtypeStruct((B, T if sample else SUBLANES, D_MODEL), F32)]
    args = [x] + ([prev] if sample else []) + [
        p["gpre"], p["w_in"], p["mu"], p["w0"], p["a0"], p["w2c"], p["k_k"], p["k_a"], p["r_k"],
        p["glu_b"], p["ones_bd"]]
    return pl.pallas_call(
        functools.partial(_prep_kernel, sample),
        grid=(B, T // tm),
        in_specs=in_specs,
        out_specs=out_specs,
        out_shape=out_shape,
        scratch_shapes=[] if sample else [pltpu.VMEM((SUBLANES, N_SHIFT), F32)],
        compiler_params=pltpu.CompilerParams(
            dimension_semantics=("arbitrary", "arbitrary"), vmem_limit_bytes=VMEM_LIMIT),
        name="prep_sample" if sample else "prep_prompt",
    )(*args)


def _block_diag(x, same_head):
    return jnp.where(same_head, jnp.concatenate([x, x], axis=0), jnp.zeros((), x.dtype))


def _drain(steps):
    for _ in steps:
        pass


def _wkv_masks():
    row = lax.broadcasted_iota(jnp.int32, (CHUNK, PAIR), 0)
    col = lax.broadcasted_iota(jnp.int32, (CHUNK, PAIR), 1) % HEAD_DIM
    brow = lax.broadcasted_iota(jnp.int32, (PAIR, PAIR), 0)
    bcol = lax.broadcasted_iota(jnp.int32, (PAIR, PAIR), 1)
    trow = lax.broadcasted_iota(jnp.int32, (CHUNK, CHUNK), 0)
    tcol = lax.broadcasted_iota(jnp.int32, (CHUNK, CHUNK), 1)
    return dict(strict=row > col, incl=row >= col, eye_pair=(row == col).astype(F32),
                same_head=(brow // HEAD_DIM) == (bcol // HEAD_DIM), eye_full=brow == bcol,
                first_head=lax.broadcasted_iota(jnp.int32, (CHUNK, PAIR), 1) < HEAD_DIM,
                tri=(trow >= tcol).astype(BF16))


def _block_diag_t(x, same_head):
    return jnp.where(same_head, jnp.concatenate([x, x], axis=0).T, 0.0).astype(BF16)


def _wkv_chunk_operators(r_ref, k_ref, v_ref, a_ref, b_ref, lw_ref, first_chunk, m, tick):
    strict, incl, eye_pair, same_head = m["strict"], m["incl"], m["eye_pair"], m["same_head"]
    bd = lambda x: _block_diag(x.astype(BF16), same_head)
    bd_t = lambda x: _block_diag_t(x, same_head)

    def chunk_inputs(row0):
        rows = pl.ds(row0, CHUNK)
        lw = lw_ref[0, rows, :]
        l1 = lw.astype(BF16)
        rem = lw - l1.astype(F32)
        l2 = rem.astype(BF16)
        l3 = (rem - l2.astype(F32)).astype(BF16)
        cl = _dot(m["tri"], l1) + _dot(m["tri"], l2) + _dot(m["tri"], l3)
        e_neg = jnp.exp(-cl)
        w_end = jnp.exp(cl[CHUNK - 1:CHUNK, :])
        b_t = b_ref[0, rows, :] * e_neg
        k_t = k_ref[0, rows, :] * e_neg
        return dict(r=r_ref[0, rows, :] * jnp.exp(cl), a=(a_ref[0, rows, :] * jnp.exp(cl - lw)).astype(BF16),
                    b=b_t, k=k_t, be=(b_t * w_end).astype(BF16), ke=(k_t * w_end).astype(BF16),
                    v=v_ref[0, rows, :], w_end=w_end)

    ch = []
    for c in range(WKV_GROUP):
        ch.append(chunk_inputs((first_chunk + c) * CHUNK))
        tick()
    pairs = range(N_PAIRS)
    sls = [slice(p * PAIR, (p + 1) * PAIR) for p in pairs]
    units = [(c, p) for c in range(WKV_GROUP) for p in pairs]
    us = range(len(units))
    pick = lambda name: [ch[c][name][:, sls[p]] for c, p in units]
    at, rt, bt, kt, be, ke, vp = (pick(n) for n in ("a", "r", "b", "k", "be", "ke", "v"))
    g = [_dot(jnp.concatenate([at[u], rt[u].astype(BF16)], axis=0),
              jnp.concatenate([bd_t(bt[u]), bd_t(kt[u])], axis=1)) for u in us]
    tick()
    l_ab = [jnp.where(strict, g[u][:CHUNK, :PAIR], 0.0) for u in us]
    m_rb = [jnp.where(incl, g[u][CHUNK:, :PAIR], 0.0).astype(BF16) for u in us]
    km = [jnp.concatenate([jnp.where(strict, g[u][:CHUNK, PAIR:], 0.0),
                           jnp.where(incl, g[u][CHUNK:, PAIR:], 0.0)], axis=0).astype(BF16) for u in us]
    uy = [_dot(km[u], bd(vp[u])) for u in us]
    tick()
    uv0 = [uy[u][:CHUNK].astype(BF16) for u in us]
    t_acc = [eye_pair + l_ab[u] for u in us]
    lp = [_dot(l_ab[u].astype(BF16), bd(l_ab[u])).astype(BF16) for u in us]
    tick()
    for _ in range(4):
        prod = [_dot(lp[u], jnp.concatenate([bd(lp[u]), bd(t_acc[u])], axis=1)) for u in us]
        lp = [prod[u][:, :PAIR].astype(BF16) for u in us]
        t_acc = [t_acc[u] + prod[u][:, PAIR:] for u in us]
        tick()
    t_acc = [t_acc[u] + _dot(lp[u], bd(t_acc[u])) for u in us]
    tick()
    au = [_dot(t_acc[u].astype(BF16), jnp.concatenate([bd(at[u]), bd(uv0[u])], axis=1)) for u in us]
    tick()
    a_hat = [au[u][:, :PAIR] for u in us]
    u_v = [au[u][:, PAIR:] for u in us]
    ry = [_dot(m_rb[u], jnp.concatenate([bd(a_hat[u]), bd(u_v[u])], axis=1)) for u in us]
    tick()
    r_hat = [(rt[u] + ry[u][:, :PAIR]).astype(BF16) for u in us]
    y_v = [ry[u][:, PAIR:] + uy[u][CHUNK:] for u in us]
    phi = [_dot(a_hat[u].T.astype(BF16), be[u]) for u in us]
    tick()
    f = [_dot(jnp.concatenate([u_v[u], vp[u]], axis=0).T.astype(BF16),
              jnp.concatenate([be[u], ke[u]], axis=0)) for u in us]
    tick()
    phi = [(jnp.where(same_head, phi[u], 0.0)
            + jnp.where(m["eye_full"], ch[units[u][0]]["w_end"][:, sls[units[u][1]]], 0.0)).astype(BF16)
           for u in us]
    z = [jnp.where(m["first_head"], f[u][:HEAD_DIM, :], f[u][HEAD_DIM:, :]) for u in us]
    return r_hat, y_v, phi, z


def _wkv_recurrence_steps(slot, rh_ref, yv_ref, phi_ref, z_ref, s_ref, y_ref, m):
    pairs = range(N_PAIRS)
    state = [s_ref[p] for p in pairs]
    for c in range(TILE_CHUNKS):
        us = [c * N_PAIRS + p for p in pairs]
        ys = [_dot(rh_ref[slot, us[p]], _block_diag_t(state[p], m["same_head"])) + yv_ref[slot, us[p]]
              for p in pairs]
        state = [_dot(state[p].astype(BF16), phi_ref[slot, us[p]]) + z_ref[slot, us[p]] for p in pairs]
        y_ref[c * CHUNK:(c + 1) * CHUNK, :] = jnp.concatenate(ys, axis=1)
        yield
    for p in pairs:
        s_ref[p] = state[p]


def _wkv_prompt_kernel(nt, r_ref, k_ref, v_ref, a_ref, b_ref, lw_ref, szr_ref, bon_ref, gng_ref, gnb_ref, ones_ref,
                       o_ref, s_out, s_ref, y_ref, rh_ref, yv_ref, phi_ref, z_ref):
    s = pl.program_id(0)
    prev = jnp.maximum(s - 1, 0)
    cur_slot = s % 2

    @pl.when(s == 0)
    def _():
        for ref in (rh_ref, yv_ref, phi_ref, z_ref):
            ref[1] = jnp.zeros_like(ref[1])

    @pl.when(prev % nt == 0)
    def _():
        s_ref[...] = jnp.zeros_like(s_ref)

    m = _wkv_masks()

    def previous_tile():
        yield from _wkv_recurrence_steps(1 - cur_slot, rh_ref, yv_ref, phi_ref, z_ref, s_ref, y_ref, m)
        o_ref[0] = _group_norm_gate(y_ref[...], bon_ref[0], szr_ref[0], gng_ref[...], gnb_ref[...], ones_ref[...])

    prev_steps = previous_tile()
    for first_chunk in range(0, TILE_CHUNKS, WKV_GROUP):
        ops = _wkv_chunk_operators(r_ref, k_ref, v_ref, a_ref, b_ref, lw_ref, first_chunk, m,
                                   tick=lambda: next(prev_steps, None))
        for i in range(WKV_GROUP * N_PAIRS):
            u = first_chunk * N_PAIRS + i
            for ref, val in zip((rh_ref, yv_ref, phi_ref, z_ref), ops):
                ref[cur_slot, u] = val[i]
    _drain(prev_steps)

    @pl.when((s > 0) & (prev % nt == nt - 1))
    def _():
        for hd in range(N_HEADS):
            lo = (hd % 2) * HEAD_DIM
            s_out[0, hd] = s_ref[hd // 2][:, lo:lo + HEAD_DIM]


def _wkv_prompt(pre, p, tm):
    r, k, v, a, b, lw, szr, bon = pre
    B, T, _ = r.shape
    assert T % tm == 0 and tm == TILE_CHUNKS * CHUNK and TILE_CHUNKS % WKV_GROUP == 0, (T, tm)
    nt = T // tm
    n_tiles = B * nt
    n_units = TILE_CHUNKS * N_PAIRS
    cur = pl.BlockSpec((1, tm, D_RWKV),
                       lambda s: (jnp.minimum(s, n_tiles - 1) // nt, jnp.minimum(s, n_tiles - 1) % nt, 0))
    prv = pl.BlockSpec((1, tm, D_RWKV), lambda s: (jnp.maximum(s - 1, 0) // nt, jnp.maximum(s - 1, 0) % nt, 0))
    state = pl.BlockSpec((1, N_HEADS, HEAD_DIM, HEAD_DIM), lambda s: (jnp.maximum(s - 1, 0) // nt, 0, 0, 0))
    return pl.pallas_call(
        functools.partial(_wkv_prompt_kernel, nt),
        grid=(n_tiles + 1,),
        in_specs=[cur] * 6 + [prv] * 2 + [_const_spec((1, D_RWKV)), _const_spec((1, D_RWKV)),
                                          _const_spec((MXU_DIM, MXU_DIM))],
        out_specs=[prv, state],
        out_shape=[jax.ShapeDtypeStruct((B, T, D_RWKV), BF16),
                   jax.ShapeDtypeStruct((B, N_HEADS, HEAD_DIM, HEAD_DIM), F32)],
        scratch_shapes=[pltpu.VMEM((N_PAIRS, HEAD_DIM, PAIR), F32), pltpu.VMEM((tm, D_RWKV), F32),
                        pltpu.VMEM((2, n_units, CHUNK, PAIR), BF16), pltpu.VMEM((2, n_units, CHUNK, PAIR), F32),
                        pltpu.VMEM((2, n_units, PAIR, PAIR), BF16), pltpu.VMEM((2, n_units, CHUNK, PAIR), F32)],
        compiler_params=pltpu.CompilerParams(
            dimension_semantics=("arbitrary",), vmem_limit_bytes=VMEM_LIMIT),
        name="wkv_prompt",
    )(r, k, v, a, b, lw, szr, bon, p["gn_g"], p["gn_b"], p["ones_bd"])


def _wkv_step_kernel(r_ref, k_ref, v_ref, a_ref, b_ref, lw_ref, szr_ref, bon_ref, gng_ref, gnb_ref,
                     ones_ref, s_in, o_ref, s_out, vec_t, y_t):
    hd = pl.program_id(0)
    ns = s_in.shape[-1]

    @pl.when(hd == 0)
    def _():
        for n, ref in enumerate((a_ref, b_ref, k_ref, r_ref)):
            vec_t[n] = ref[0].T
        vec_t[4] = jnp.exp(lw_ref[0]).T
        vec_t[5] = v_ref[0].T

    base = pl.multiple_of(hd * HEAD_DIM, HEAD_DIM)
    hrows = pl.ds(base, HEAD_DIM)
    a_h, b_h, k_h, r_h, w_h = (vec_t[n, hrows, :] for n in range(5))
    sub = lax.broadcasted_iota(jnp.int32, (SUBLANES, ns), 0)

    def value_group(g, carry):
        off = pl.multiple_of(g * SUBLANES, SUBLANES)
        v_g = vec_t[5, pl.ds(base + off, SUBLANES), :]
        y_g = jnp.zeros((SUBLANES, ns), F32)
        for ii in range(SUBLANES):
            s0 = s_in[0, off + ii]
            sa = jnp.sum(s0 * a_h, axis=0, keepdims=True)
            s1 = s0 * w_h + sa * b_h + v_g[ii:ii + 1, :] * k_h
            s_out[0, off + ii] = s1
            y_g = jnp.where(sub == ii, jnp.sum(s1 * r_h, axis=0, keepdims=True), y_g)
        y_t[pl.ds(base + off, SUBLANES), :] = y_g
        return carry

    lax.fori_loop(0, HEAD_DIM // SUBLANES, value_group, 0)

    @pl.when(hd == pl.num_programs(0) - 1)
    def _():
        o_ref[0] = _group_norm_gate(y_t[...].T, bon_ref[0], szr_ref[0], gng_ref[...], gnb_ref[...], ones_ref[...])


def _wkv_sample(pre, state_t, p):
    ns = state_t.shape[-1]
    vec = pl.BlockSpec((1, ns, D_RWKV), lambda h: (0, 0, 0))
    st = pl.BlockSpec((1, HEAD_DIM, HEAD_DIM, ns), lambda h: (h, 0, 0, 0))
    return pl.pallas_call(
        _wkv_step_kernel,
        grid=(N_HEADS,),
        in_specs=[vec] * 8 + [_const_spec((1, D_RWKV)), _const_spec((1, D_RWKV)),
                              _const_spec((MXU_DIM, MXU_DIM)), st],
        out_specs=[vec, st],
        out_shape=[jax.ShapeDtypeStruct((1, ns, D_RWKV), BF16),
                   jax.ShapeDtypeStruct(state_t.shape, F32)],
        scratch_shapes=[pltpu.VMEM((6, D_RWKV, ns), F32), pltpu.VMEM((D_RWKV, ns), F32)],
        compiler_params=pltpu.CompilerParams(
            dimension_semantics=("arbitrary",), vmem_limit_bytes=VMEM_LIMIT),
        name="wkv_sample",
    )(*pre, p["gn_g"], p["gn_b"], p["ones_bd"], state_t)


def _conv_tile(u, cw_ref, cb_ref, ubuf):
    tm = u.shape[0]
    ubuf[CONV_PAD:CONV_PAD + tm, :] = u
    win = ubuf[...]
    rows = tm + CONV_PAD
    base = CONV_PAD - (CONV_W - 1)
    acc = jnp.broadcast_to(cb_ref[...], u.shape)
    for s in range(SUBLANES):
        shifted = win if s == 0 else pltpu.roll(win, shift=rows - s, axis=0)
        for kk in range(CONV_W):
            if (base + kk) % SUBLANES == s:
                q = base + kk - s
                acc = acc + cw_ref[kk:kk + 1, :] * shifted[q:q + tm, :]
    ubuf[0:CONV_PAD, :] = win[tm:tm + CONV_PAD, :]
    return acc


def _post_tail(conv, x, o_r, szc, sgr, sgc, lng_ref, lnb_ref, wbr_ref, wbc_ref, wout_ref, gpost_ref):
    cm = jnp.mean(conv, axis=-1, keepdims=True)
    d = conv - cm
    cv = jnp.mean(d * d, axis=-1, keepdims=True)
    cn = d * lax.rsqrt(cv + LN_EPS) * lng_ref[...] + lnb_ref[...]
    o_c = (cn * _sigmoid(cn)) * szc
    branch_c = _dot(o_c.astype(BF16), wbc_ref[...])
    branch_r = _dot(o_r, wbr_ref[...])
    merged = sgr * branch_r + sgc * branch_c
    out = _dot(merged.astype(BF16), wout_ref[...])
    ms = jnp.mean(out * out, axis=-1, keepdims=True)
    return x + out * lax.rsqrt(ms + RMS_EPS) * gpost_ref[...]


def _post_prompt_kernel(x_ref, or_ref, u_ref, szc_ref, sgr_ref, sgc_ref, cw_ref, cb_ref, lng_ref, lnb_ref,
                        wbr_ref, wbc_ref, wout_ref, gpost_ref, y_ref, ubuf):
    @pl.when(pl.program_id(1) == 0)
    def _():
        ubuf[0:CONV_PAD, :] = jnp.zeros((CONV_PAD, D_CONV), F32)

    conv = _conv_tile(u_ref[0], cw_ref, cb_ref, ubuf)
    y_ref[0] = _post_tail(conv, x_ref[0], or_ref[0], szc_ref[0], sgr_ref[0], sgc_ref[0],
                          lng_ref, lnb_ref, wbr_ref, wbc_ref, wout_ref, gpost_ref)


def _post_sample_kernel(x_ref, or_ref, u_ref, szc_ref, sgr_ref, sgc_ref, cw_ref, cb_ref, lng_ref, lnb_ref,
                        wbr_ref, wbc_ref, wout_ref, gpost_ref, st_ref, y_ref, st_out):
    u = u_ref[0]
    acc = jnp.broadcast_to(cb_ref[...], u.shape)
    for kk in range(CONV_W - 1):
        acc = acc + cw_ref[kk:kk + 1, :] * st_ref[kk]
    acc = acc + cw_ref[CONV_W - 1:CONV_W, :] * u
    st_out[0:CONV_W - 2] = st_ref[1:CONV_W - 1]
    st_out[CONV_W - 2] = u
    y_ref[0] = _post_tail(acc, x_ref[0], or_ref[0], szc_ref[0], sgr_ref[0], sgc_ref[0],
                          lng_ref, lnb_ref, wbr_ref, wbc_ref, wout_ref, gpost_ref)


def _post(x, o_r, u, szc, sgr, sgc, conv_state, p, tm):
    B, T, _ = x.shape
    sample = conv_state is not None
    tile = lambda w: pl.BlockSpec((1, tm, w), lambda b, j: (b, j, 0))
    in_specs = [tile(D_MODEL), tile(D_RWKV), tile(D_CONV), tile(D_CONV), tile(D_MODEL), tile(D_MODEL),
                _const_spec((CONV_W, D_CONV)), _const_spec((1, D_CONV)), _const_spec((1, D_CONV)),
                _const_spec((1, D_CONV)), _const_spec((D_RWKV, D_MODEL)), _const_spec((D_CONV, D_MODEL)),
                _const_spec((D_MODEL, D_MODEL)), _const_spec((1, D_MODEL))]
    args = [x, o_r, u, szc, sgr, sgc, p["conv_w"], p["conv_b"], p["ln_g"], p["ln_b"],
            p["w_br"], p["w_bc"], p["w_out"], p["gpost"]]
    out_specs = [tile(D_MODEL)]
    out_shape = [jax.ShapeDtypeStruct((B, T, D_MODEL), F32)]
    scratch = []
    if sample:
        st = pl.BlockSpec((CONV_W - 1, tm, D_CONV), lambda b, j: (0, j, 0))
        in_specs.append(st)
        args.append(conv_state)
        out_specs.append(st)
        out_shape.append(jax.ShapeDtypeStruct(conv_state.shape, F32))
    else:
        scratch = [pltpu.VMEM((tm + CONV_PAD, D_CONV), F32)]
    return pl.pallas_call(
        _post_sample_kernel if sample else _post_prompt_kernel,
        grid=(B, T // tm),
        in_specs=in_specs,
        out_specs=out_specs,
        out_shape=out_shape,
        scratch_shapes=scratch,
        compiler_params=pltpu.CompilerParams(
            dimension_semantics=("arbitrary", "arbitrary"), vmem_limit_bytes=VMEM_LIMIT),
        name="post_sample" if sample else "post_prompt",
    )(*args)


def _layer_params(l, norm_pre_g, w_in, mu_shift, decay_w0, decay_w2, iclr_a0, iclr_a2, k_k, k_a, r_k, gn_g, gn_b,
                  conv_glu_b, conv_w, conv_b, ln_c_g, ln_c_b, w_branch_r, w_branch_c, w_out, norm_post_g):
    row = lambda t: t[l].reshape(1, -1).astype(F32)
    zeros = jnp.zeros((LORA, D_RWKV), F32)
    w2c = jnp.concatenate([jnp.concatenate([decay_w2[l], zeros], axis=1),
                           jnp.concatenate([zeros, iclr_a2[l]], axis=1)], axis=0)
    blk = jnp.arange(MXU_DIM) // HEAD_DIM
    return dict(
        gpre=row(norm_pre_g), w_in=w_in[l].astype(BF16), mu=row(mu_shift), w0=row(decay_w0), a0=row(iclr_a0),
        w2c=w2c.astype(BF16), k_k=row(k_k), k_a=row(k_a), r_k=row(r_k), glu_b=row(conv_glu_b),
        ones_bd=(blk[:, None] == blk[None, :]).astype(BF16),
        gn_g=row(gn_g), gn_b=row(gn_b), conv_w=conv_w[l].astype(F32), conv_b=row(conv_b),
        ln_g=row(ln_c_g), ln_b=row(ln_c_b), w_br=w_branch_r[l].astype(BF16), w_bc=w_branch_c[l].astype(BF16),
        w_out=w_out[l].astype(BF16), gpost=row(norm_post_g))


def _prompt_layer(x, p, tm_prep, tm_wkv, tm_post):
    T = x.shape[1]
    *pre, u, szc, sgr, sgc, h_last = _prep(x, None, p, tm_prep)
    o_r, wkv = _wkv_prompt(pre, p, tm_wkv)
    (y,) = _post(x, o_r, u, szc, sgr, sgc, None, p, tm_post)
    return y, h_last[:, SUBLANES - 1], wkv, u[:, T - (CONV_W - 1):]


def _sample_layer(x, shift, wkv, conv, p, tm_post):
    ns = x.shape[0]
    xs = x.reshape(1, ns, D_MODEL)
    *pre, u, szc, sgr, sgc, h = _prep(xs, shift.reshape(1, ns, D_MODEL), p, ns)
    o_r, wkv_new = _wkv_sample(pre, jnp.transpose(wkv, (1, 2, 3, 0)), p)
    y, conv_new = _post(xs, o_r, u, szc, sgr, sgc, jnp.transpose(conv, (1, 0, 2)), p, tm_post)
    return (y.reshape(ns, 1, D_MODEL), h.reshape(ns, D_MODEL), jnp.transpose(wkv_new, (3, 0, 1, 2)),
            jnp.transpose(conv_new, (1, 0, 2)))


def kernel(x_prompt, x_sample, state_shift, state_wkv, state_conv, norm_pre_g, w_in, mu_shift, decay_w0, decay_w2,
           iclr_a0, iclr_a2, k_k, k_a, r_k, gn_g, gn_b, conv_glu_b, conv_w, conv_b, ln_c_g, ln_c_b, w_branch_r,
           w_branch_c, w_out, norm_post_g):
    depth = w_in.shape[0]
    xp, xs = x_prompt, x_sample
    outs = [[] for _ in range(6)]
    for l in range(depth):
        p = _layer_params(l, norm_pre_g, w_in, mu_shift, decay_w0, decay_w2, iclr_a0, iclr_a2, k_k, k_a, r_k, gn_g,
                          gn_b, conv_glu_b, conv_w, conv_b, ln_c_g, ln_c_b, w_branch_r, w_branch_c, w_out,
                          norm_post_g)
        xp, a1, a2, a3 = _prompt_layer(xp, p, tm_prep=256, tm_wkv=256, tm_post=512)
        xs, b1, b2, b3 = _sample_layer(xs, state_shift[l], state_wkv[l], state_conv[l], p, tm_post=64)
        for lst, val in zip(outs, (a1, a2, a3, b1, b2, b3)):
            lst.append(val)
    return (xp, xs) + tuple(jnp.stack(o) for o in outs)
```

```python
import functools
import math

import jax
import jax.numpy as jnp
from jax import lax
from jax.experimental import pallas as pl
from jax.experimental.pallas import tpu as pltpu

F32 = jnp.float32
BF16 = jnp.bfloat16

D_MODEL = 1024
HEAD_DIM = 64
N_HEADS = D_MODEL // HEAD_DIM
D_RWKV = N_HEADS * HEAD_DIM
D_CONV = D_MODEL // 2
CONV_W = 31
LORA = 64
RMS_EPS = 1e-6
GN_EPS = 64e-5
LN_EPS = 1e-5
N_SHIFT = 4 * D_RWKV + 2 * LORA
N_IN = N_SHIFT + 2 * D_CONV + D_CONV + 2 * D_MODEL

SUBLANES = 8
PAIR = 2 * HEAD_DIM
N_PAIRS = N_HEADS // 2
MXU_DIM = 256
CHUNK = 64
TILE_CHUNKS = 4
WKV_GROUP = 2
CONV_PAD = 32
POST_PARTS = 2
VMEM_LIMIT = 56 * 1024 * 1024

_O_R, _O_K, _O_V, _O_ZR = 0, D_RWKV, 2 * D_RWKV, 3 * D_RWKV
_O_LORA = 4 * D_RWKV
_O_U = N_SHIFT
_O_ZC = _O_U + 2 * D_CONV
_O_GR = _O_ZC + D_CONV
_O_GC = _O_GR + D_MODEL


def _dot(a, b):
    return jnp.dot(a, b, preferred_element_type=F32)


def _sigmoid(x):
    return 1.0 / (1.0 + jnp.exp(-x))


def _head_sum(x, ones_bd):
    xb = x.astype(BF16)
    return jnp.concatenate([_dot(xb[:, c * MXU_DIM:(c + 1) * MXU_DIM], ones_bd)
                            for c in range(x.shape[1] // MXU_DIM)], axis=1)


def _group_norm_gate(y, bonus, szr, gng, gnb, ones_bd):
    mean = _head_sum(y, ones_bd) * (1.0 / HEAD_DIM)
    d = y - mean
    var = _head_sum(d * d, ones_bd) * (1.0 / HEAD_DIM)
    y_gn = d * lax.rsqrt(var + GN_EPS) * gng + gnb
    return ((y_gn + bonus) * szr).astype(BF16)


def _prep_kernel(sample, *refs):
    if sample:
        (x_ref, prev_ref, gpre_ref, win_ref, mu_ref, w0_ref, a0_ref, w2c_ref, kk_ref, ka_ref, rk_ref,
         glub_ref, ones_ref,
         r_out, k_out, v_out, a_out, b_out, lw_out, szr_out, bon_out, u_out, szc_out, sgr_out, sgc_out,
         h_out) = refs
    else:
        (x_ref, gpre_ref, win_ref, mu_ref, w0_ref, a0_ref, w2c_ref, kk_ref, ka_ref, rk_ref,
         glub_ref, ones_ref,
         r_out, k_out, v_out, a_out, b_out, lw_out, szr_out, bon_out, u_out, szc_out, sgr_out, sgc_out,
         h_out, carry_ref) = refs

    x = x_ref[0]
    tm = x.shape[0]
    ms = jnp.mean(x * x, axis=-1, keepdims=True)
    h = x * lax.rsqrt(ms + RMS_EPS) * gpre_ref[...]
    hb = h.astype(BF16)

    def proj(off, width):
        return _dot(hb, win_ref[:, off:off + width])

    if sample:
        h_out[0] = h
        pb = prev_ref[0].astype(BF16)
    else:
        h_out[0] = h[tm - SUBLANES:, :]

        @pl.when(pl.program_id(1) == 0)
        def _():
            carry_ref[...] = jnp.zeros_like(carry_ref)
        row0 = lax.broadcasted_iota(jnp.int32, (tm, 1), 0) == 0

    def shifted(off, width):
        p = proj(off, width)
        if sample:
            pp = _dot(pb, win_ref[:, off:off + width])
        else:
            rolled = pltpu.roll(p, shift=1, axis=0)
            pp = jnp.where(row0, carry_ref[0:1, off:off + width], rolled)
            carry_ref[0:1, off:off + width] = p[tm - 1:tm, :]
        return p + (pp - p) * mu_ref[:, off:off + width]

    ones_bd = ones_ref[...]

    lo = shifted(_O_LORA, 2 * LORA)
    k = shifted(_O_K, D_RWKV)

    lane = lax.broadcasted_iota(jnp.int32, lo.shape, 1)
    lo = jnp.where(lane < LORA, jnp.tanh(lo), lo)
    lor = _dot(lo.astype(BF16), w2c_ref[...])

    zr = shifted(_O_ZR, D_RWKV)
    szr_out[0] = zr * _sigmoid(zr)

    kk = k * kk_ref[...]
    kk_ss = _head_sum(kk * kk, ones_bd)

    u_in = proj(_O_U, 2 * D_CONV) + glub_ref[...]
    u_out[0] = u_in[:, :D_CONV] * _sigmoid(u_in[:, D_CONV:])

    w_raw = w0_ref[...] + lor[:, :D_RWKV]
    lw_out[0] = (-math.exp(-0.5)) * _sigmoid(w_raw)
    iclr = _sigmoid(a0_ref[...] + lor[:, D_RWKV:])
    kk = kk * lax.rsqrt(kk_ss + 1e-12)
    k2 = k * (1.0 + (iclr - 1.0) * ka_ref[...])
    k_out[0] = k2
    a_out[0] = -kk
    b_out[0] = kk * iclr

    zc = proj(_O_ZC, D_CONV)
    szc_out[0] = zc * _sigmoid(zc)
    sgr_out[0] = _sigmoid(proj(_O_GR, D_MODEL))
    sgc_out[0] = _sigmoid(proj(_O_GC, D_MODEL))

    r = shifted(_O_R, D_RWKV)
    r_out[0] = r
    v = shifted(_O_V, D_RWKV)
    v_out[0] = v
    bon_out[0] = _head_sum(r * k2 * rk_ref[...], ones_bd) * v


def _const_spec(shape):
    nd = len(shape)
    return pl.BlockSpec(shape, lambda *_: (0,) * nd, pipeline_mode=pl.Buffered(1))


def _prep(x, prev, p, tm):
    B, T, _ = x.shape
    sample = prev is not None
    tile = lambda w: pl.BlockSpec((1, tm, w), lambda b, j: (b, j, 0))
    in_specs = [tile(D_MODEL)] + ([tile(D_MODEL)] if sample else []) + [
        _const_spec((1, D_MODEL)), _const_spec((D_MODEL, N_IN)), _const_spec((1, N_SHIFT)),
        _const_spec((1, D_RWKV)), _const_spec((1, D_RWKV)), _const_spec((2 * LORA, 2 * D_RWKV)),
        _const_spec((1, D_RWKV)), _const_spec((1, D_RWKV)), _const_spec((1, D_RWKV)),
        _const_spec((1, 2 * D_CONV)), _const_spec((MXU_DIM, MXU_DIM))]
    widths = [D_RWKV] * 8 + [D_CONV, D_CONV, D_MODEL, D_MODEL]
    h_spec = tile(D_MODEL) if sample else pl.BlockSpec((1, SUBLANES, D_MODEL), lambda b, j: (b, 0, 0))
    out_specs = [tile(w) for w in widths] + [h_spec]
    out_shape = [jax.ShapeDtypeStruct((B, T, w), F32) for w in widths] + [
        jax.ShapeDtypeStruct((B, T if sample else SUBLANES, D_MODEL), F32)]
    args = [x] + ([prev] if sample else []) + [
        p["gpre"], p["w_in"], p["mu"], p["w0"], p["a0"], p["w2c"], p["k_k"], p["k_a"], p["r_k"],
        p["glu_b"], p["ones_bd"]]
    return pl.pallas_call(
        functools.partial(_prep_kernel, sample),
        grid=(B, T // tm),
        in_specs=in_specs,
        out_specs=out_specs,
        out_shape=out_shape,
        scratch_shapes=[] if sample else [pltpu.VMEM((SUBLANES, N_SHIFT), F32)],
        compiler_params=pltpu.CompilerParams(
            dimension_semantics=("arbitrary", "arbitrary"), vmem_limit_bytes=VMEM_LIMIT),
        name="prep_sample" if sample else "prep_prompt",
    )(*args)


def _block_diag(x, same_head):
    return jnp.where(same_head, jnp.concatenate([x, x], axis=0), jnp.zeros((), x.dtype))


def _drain(steps):
    for _ in steps:
        pass


def _wkv_masks():
    row = lax.broadcasted_iota(jnp.int32, (CHUNK, PAIR), 0)
    col = lax.broadcasted_iota(jnp.int32, (CHUNK, PAIR), 1) % HEAD_DIM
    brow = lax.broadcasted_iota(jnp.int32, (PAIR, PAIR), 0)
    bcol = lax.broadcasted_iota(jnp.int32, (PAIR, PAIR), 1)
    trow = lax.broadcasted_iota(jnp.int32, (CHUNK, CHUNK), 0)
    tcol = lax.broadcasted_iota(jnp.int32, (CHUNK, CHUNK), 1)
    return dict(strict=row > col, incl=row >= col, eye_pair=(row == col).astype(F32),
                same_head=(brow // HEAD_DIM) == (bcol // HEAD_DIM), eye_full=brow == bcol,
                first_head=lax.broadcasted_iota(jnp.int32, (CHUNK, PAIR), 1) < HEAD_DIM,
                tri=(trow >= tcol).astype(BF16))


def _block_diag_t(x, same_head):
    return jnp.where(same_head, jnp.concatenate([x, x], axis=0).T, 0.0).astype(BF16)


def _wkv_chunk_operators(r_ref, k_ref, v_ref, a_ref, b_ref, lw_ref, first_chunk, m, tick):
    strict, incl, eye_pair, same_head = m["strict"], m["incl"], m["eye_pair"], m["same_head"]
    bd = lambda x: _block_diag(x.astype(BF16), same_head)
    bd_t = lambda x: _block_diag_t(x, same_head)

    def chunk_inputs(row0):
        rows = pl.ds(row0, CHUNK)
        lw = lw_ref[0, rows, :]
        l1 = lw.astype(BF16)
        rem = lw - l1.astype(F32)
        l2 = rem.astype(BF16)
        l3 = (rem - l2.astype(F32)).astype(BF16)
        cl = _dot(m["tri"], l1) + _dot(m["tri"], l2) + _dot(m["tri"], l3)
        e_neg = jnp.exp(-cl)
        w_end = jnp.exp(cl[CHUNK - 1:CHUNK, :])
        b_t = b_ref[0, rows, :] * e_neg
        k_t = k_ref[0, rows, :] * e_neg
        return dict(r=r_ref[0, rows, :] * jnp.exp(cl), a=(a_ref[0, rows, :] * jnp.exp(cl - lw)).astype(BF16),
                    b=b_t, k=k_t, be=(b_t * w_end).astype(BF16), ke=(k_t * w_end).astype(BF16),
                    v=v_ref[0, rows, :], w_end=w_end)

    ch = []
    for c in range(WKV_GROUP):
        ch.append(chunk_inputs((first_chunk + c) * CHUNK))
        tick()
    pairs = range(N_PAIRS)
    sls = [slice(p * PAIR, (p + 1) * PAIR) for p in pairs]
    units = [(c, p) for c in range(WKV_GROUP) for p in pairs]
    us = range(len(units))
    pick = lambda name: [ch[c][name][:, sls[p]] for c, p in units]
    at, rt, bt, kt, be, ke, vp = (pick(n) for n in ("a", "r", "b", "k", "be", "ke", "v"))
    g = [_dot(jnp.concatenate([at[u], rt[u].astype(BF16)], axis=0),
              jnp.concatenate([bd_t(bt[u]), bd_t(kt[u])], axis=1)) for u in us]
    tick()
    l_ab = [jnp.where(strict, g[u][:CHUNK, :PAIR], 0.0) for u in us]
    m_rb = [jnp.where(incl, g[u][CHUNK:, :PAIR], 0.0).astype(BF16) for u in us]
    km = [jnp.concatenate([jnp.where(strict, g[u][:CHUNK, PAIR:], 0.0),
                           jnp.where(incl, g[u][CHUNK:, PAIR:], 0.0)], axis=0).astype(BF16) for u in us]
    uy = [_dot(km[u], bd(vp[u])) for u in us]
    tick()
    uv0 = [uy[u][:CHUNK].astype(BF16) for u in us]
    t_acc = [eye_pair + l_ab[u] for u in us]
    lp = [_dot(l_ab[u].astype(BF16), bd(l_ab[u])).astype(BF16) for u in us]
    tick()
    for _ in range(4):
        prod = [_dot(lp[u], jnp.concatenate([bd(lp[u]), bd(t_acc[u])], axis=1)) for u in us]
        lp = [prod[u][:, :PAIR].astype(BF16) for u in us]
        t_acc = [t_acc[u] + prod[u][:, PAIR:] for u in us]
        tick()
    t_acc = [t_acc[u] + _dot(lp[u], bd(t_acc[u])) for u in us]
    tick()
    au = [_dot(t_acc[u].astype(BF16), jnp.concatenate([bd(at[u]), bd(uv0[u])], axis=1)) for u in us]
    tick()
    a_hat = [au[u][:, :PAIR] for u in us]
    u_v = [au[u][:, PAIR:] for u in us]
    ry = [_dot(m_rb[u], jnp.concatenate([bd(a_hat[u]), bd(u_v[u])], axis=1)) for u in us]
    tick()
    r_hat = [(rt[u] + ry[u][:, :PAIR]).astype(BF16) for u in us]
    y_v = [ry[u][:, PAIR:] + uy[u][CHUNK:] for u in us]
    phi = [_dot(a_hat[u].T.astype(BF16), be[u]) for u in us]
    tick()
    f = [_dot(jnp.concatenate([u_v[u], vp[u]], axis=0).T.astype(BF16),
              jnp.concatenate([be[u], ke[u]], axis=0)) for u in us]
    tick()
    phi = [(jnp.where(same_head, phi[u], 0.0)
            + jnp.where(m["eye_full"], ch[units[u][0]]["w_end"][:, sls[units[u][1]]], 0.0)).astype(BF16)
           for u in us]
    z = [jnp.where(m["first_head"], f[u][:HEAD_DIM, :], f[u][HEAD_DIM:, :]) for u in us]
    return r_hat, y_v, phi, z


def _wkv_recurrence_steps(slot, rh_ref, yv_ref, phi_ref, z_ref, s_ref, y_ref, m):
    pairs = range(N_PAIRS)
    state = [s_ref[p] for p in pairs]
    for c in range(TILE_CHUNKS):
        us = [c * N_PAIRS + p for p in pairs]
        ys = [_dot(rh_ref[slot, us[p]], _block_diag_t(state[p], m["same_head"])) + yv_ref[slot, us[p]]
              for p in pairs]
        state = [_dot(state[p].astype(BF16), phi_ref[slot, us[p]]) + z_ref[slot, us[p]] for p in pairs]
        y_ref[c * CHUNK:(c + 1) * CHUNK, :] = jnp.concatenate(ys, axis=1)
        yield
    for p in pairs:
        s_ref[p] = state[p]


def _wkv_prompt_kernel(nt, r_ref, k_ref, v_ref, a_ref, b_ref, lw_ref, szr_ref, bon_ref, gng_ref, gnb_ref, ones_ref,
                       o_ref, s_out, s_ref, y_ref, rh_ref, yv_ref, phi_ref, z_ref):
    s = pl.program_id(0)
    prev = jnp.maximum(s - 1, 0)
    cur_slot = s % 2

    @pl.when(s == 0)
    def _():
        for ref in (rh_ref, yv_ref, phi_ref, z_ref):
            ref[1] = jnp.zeros_like(ref[1])

    @pl.when(prev % nt == 0)
    def _():
        s_ref[...] = jnp.zeros_like(s_ref)

    m = _wkv_masks()

    def previous_tile():
        yield from _wkv_recurrence_steps(1 - cur_slot, rh_ref, yv_ref, phi_ref, z_ref, s_ref, y_ref, m)
        o_ref[0] = _group_norm_gate(y_ref[...], bon_ref[0], szr_ref[0], gng_ref[...], gnb_ref[...], ones_ref[...])

    prev_steps = previous_tile()
    for first_chunk in range(0, TILE_CHUNKS, WKV_GROUP):
        ops = _wkv_chunk_operators(r_ref, k_ref, v_ref, a_ref, b_ref, lw_ref, first_chunk, m,
                                   tick=lambda: next(prev_steps, None))
        for i in range(WKV_GROUP * N_PAIRS):
            u = first_chunk * N_PAIRS + i
            for ref, val in zip((rh_ref, yv_ref, phi_ref, z_ref), ops):
                ref[cur_slot, u] = val[i]
    _drain(prev_steps)

    @pl.when((s > 0) & (prev % nt == nt - 1))
    def _():
        for hd in range(N_HEADS):
            lo = (hd % 2) * HEAD_DIM
            s_out[0, hd] = s_ref[hd // 2][:, lo:lo + HEAD_DIM]


def _wkv_prompt(pre, p, tm):
    r, k, v, a, b, lw, szr, bon = pre
    B, T, _ = r.shape
    assert T % tm == 0 and tm == TILE_CHUNKS * CHUNK and TILE_CHUNKS % WKV_GROUP == 0, (T, tm)
    nt = T // tm
    n_tiles = B * nt
    n_units = TILE_CHUNKS * N_PAIRS
    cur = pl.BlockSpec((1, tm, D_RWKV),
                       lambda s: (jnp.minimum(s, n_tiles - 1) // nt, jnp.minimum(s, n_tiles - 1) % nt, 0))
    prv = pl.BlockSpec((1, tm, D_RWKV), lambda s: (jnp.maximum(s - 1, 0) // nt, jnp.maximum(s - 1, 0) % nt, 0))
    state = pl.BlockSpec((1, N_HEADS, HEAD_DIM, HEAD_DIM), lambda s: (jnp.maximum(s - 1, 0) // nt, 0, 0, 0))
    return pl.pallas_call(
        functools.partial(_wkv_prompt_kernel, nt),
        grid=(n_tiles + 1,),
        in_specs=[cur] * 6 + [prv] * 2 + [_const_spec((1, D_RWKV)), _const_spec((1, D_RWKV)),
                                          _const_spec((MXU_DIM, MXU_DIM))],
        out_specs=[prv, state],
        out_shape=[jax.ShapeDtypeStruct((B, T, D_RWKV), BF16),
                   jax.ShapeDtypeStruct((B, N_HEADS, HEAD_DIM, HEAD_DIM), F32)],
        scratch_shapes=[pltpu.VMEM((N_PAIRS, HEAD_DIM, PAIR), F32), pltpu.VMEM((tm, D_RWKV), F32),
                        pltpu.VMEM((2, n_units, CHUNK, PAIR), BF16), pltpu.VMEM((2, n_units, CHUNK, PAIR), F32),
                        pltpu.VMEM((2, n_units, PAIR, PAIR), BF16), pltpu.VMEM((2, n_units, CHUNK, PAIR), F32)],
        compiler_params=pltpu.CompilerParams(
            dimension_semantics=("arbitrary",), vmem_limit_bytes=VMEM_LIMIT),
        name="wkv_prompt",
    )(r, k, v, a, b, lw, szr, bon, p["gn_g"], p["gn_b"], p["ones_bd"])


def _wkv_step_kernel(r_ref, k_ref, v_ref, a_ref, b_ref, lw_ref, szr_ref, bon_ref, gng_ref, gnb_ref,
                     ones_ref, s_in, o_ref, s_out, vec_t, y_t):
    hd = pl.program_id(0)
    ns = s_in.shape[-1]

    @pl.when(hd == 0)
    def _():
        for n, ref in enumerate((a_ref, b_ref, k_ref, r_ref)):
            vec_t[n] = ref[0].T
        vec_t[4] = jnp.exp(lw_ref[0]).T
        vec_t[5] = v_ref[0].T

    base = pl.multiple_of(hd * HEAD_DIM, HEAD_DIM)
    hrows = pl.ds(base, HEAD_DIM)
    a_h, b_h, k_h, r_h, w_h = (vec_t[n, hrows, :] for n in range(5))
    sub = lax.broadcasted_iota(jnp.int32, (SUBLANES, ns), 0)

    def value_group(g, carry):
        off = pl.multiple_of(g * SUBLANES, SUBLANES)
        v_g = vec_t[5, pl.ds(base + off, SUBLANES), :]
        y_g = jnp.zeros((SUBLANES, ns), F32)
        for ii in range(SUBLANES):
            s0 = s_in[0, off + ii]
            sa = jnp.sum(s0 * a_h, axis=0, keepdims=True)
            s1 = s0 * w_h + sa * b_h + v_g[ii:ii + 1, :] * k_h
            s_out[0, off + ii] = s1
            y_g = jnp.where(sub == ii, jnp.sum(s1 * r_h, axis=0, keepdims=True), y_g)
        y_t[pl.ds(base + off, SUBLANES), :] = y_g
        return carry

    lax.fori_loop(0, HEAD_DIM // SUBLANES, value_group, 0)

    @pl.when(hd == pl.num_programs(0) - 1)
    def _():
        o_ref[0] = _group_norm_gate(y_t[...].T, bon_ref[0], szr_ref[0], gng_ref[...], gnb_ref[...], ones_ref[...])


def _wkv_sample(pre, state_t, p):
    ns = state_t.shape[-1]
    vec = pl.BlockSpec((1, ns, D_RWKV), lambda h: (0, 0, 0))
    st = pl.BlockSpec((1, HEAD_DIM, HEAD_DIM, ns), lambda h: (h, 0, 0, 0))
    return pl.pallas_call(
        _wkv_step_kernel,
        grid=(N_HEADS,),
        in_specs=[vec] * 8 + [_const_spec((1, D_RWKV)), _const_spec((1, D_RWKV)),
                              _const_spec((MXU_DIM, MXU_DIM)), st],
        out_specs=[vec, st],
        out_shape=[jax.ShapeDtypeStruct((1, ns, D_RWKV), BF16),
                   jax.ShapeDtypeStruct(state_t.shape, F32)],
        scratch_shapes=[pltpu.VMEM((6, D_RWKV, ns), F32), pltpu.VMEM((D_RWKV, ns), F32)],
        compiler_params=pltpu.CompilerParams(
            dimension_semantics=("arbitrary",), vmem_limit_bytes=VMEM_LIMIT),
        name="wkv_sample",
    )(*pre, p["gn_g"], p["gn_b"], p["ones_bd"], state_t)


def _conv_rows(win, cw_ref, cb_ref):
    rows = win.shape[0]
    n = rows - CONV_PAD
    base = CONV_PAD - (CONV_W - 1)
    acc = jnp.broadcast_to(cb_ref[...], (n, D_CONV))
    for s in range(SUBLANES):
        shifted = win if s == 0 else pltpu.roll(win, shift=rows - s, axis=0)
        for kk in range(CONV_W):
            if (base + kk) % SUBLANES == s:
                q = base + kk - s
                acc = acc + cw_ref[kk:kk + 1, :] * shifted[q:q + n, :]
    return acc


def _post_tail(conv, x, o_r, szc, sgr, sgc, lng_ref, lnb_ref, wbr_ref, wbc_ref, wout_ref, gpost_ref):
    branch_r = _dot(o_r, wbr_ref[...])
    cm = jnp.mean(conv, axis=-1, keepdims=True)
    d = conv - cm
    cv = jnp.mean(d * d, axis=-1, keepdims=True)
    cn = d * lax.rsqrt(cv + LN_EPS) * lng_ref[...] + lnb_ref[...]
    o_c = (cn * _sigmoid(cn)) * szc
    branch_c = _dot(o_c.astype(BF16), wbc_ref[...])
    merged = sgr * branch_r + sgc * branch_c
    out = _dot(merged.astype(BF16), wout_ref[...])
    ms = jnp.mean(out * out, axis=-1, keepdims=True)
    return x + out * lax.rsqrt(ms + RMS_EPS) * gpost_ref[...]


def _post_prompt_kernel(x_ref, or_ref, u_ref, szc_ref, sgr_ref, sgc_ref, cw_ref, cb_ref, lng_ref, lnb_ref,
                        wbr_ref, wbc_ref, wout_ref, gpost_ref, y_ref, ubuf):
    @pl.when(pl.program_id(1) == 0)
    def _():
        ubuf[0:CONV_PAD, :] = jnp.zeros((CONV_PAD, D_CONV), F32)

    tm = u_ref.shape[1]
    ubuf[CONV_PAD:CONV_PAD + tm, :] = u_ref[0]
    part = tm // POST_PARTS
    for i in range(POST_PARTS):
        rows = slice(i * part, (i + 1) * part)
        conv = _conv_rows(ubuf[i * part:(i + 1) * part + CONV_PAD, :], cw_ref, cb_ref)
        y_ref[0, rows, :] = _post_tail(conv, x_ref[0, rows, :], or_ref[0, rows, :], szc_ref[0, rows, :],
                                       sgr_ref[0, rows, :], sgc_ref[0, rows, :],
                                       lng_ref, lnb_ref, wbr_ref, wbc_ref, wout_ref, gpost_ref)
    ubuf[0:CONV_PAD, :] = ubuf[tm:tm + CONV_PAD, :]


def _post_sample_kernel(x_ref, or_ref, u_ref, szc_ref, sgr_ref, sgc_ref, cw_ref, cb_ref, lng_ref, lnb_ref,
                        wbr_ref, wbc_ref, wout_ref, gpost_ref, st_ref, y_ref, st_out):
    u = u_ref[0]
    acc = jnp.broadcast_to(cb_ref[...], u.shape)
    for kk in range(CONV_W - 1):
        acc = acc + cw_ref[kk:kk + 1, :] * st_ref[kk]
    acc = acc + cw_ref[CONV_W - 1:CONV_W, :] * u
    st_out[0:CONV_W - 2] = st_ref[1:CONV_W - 1]
    st_out[CONV_W - 2] = u
    y_ref[0] = _post_tail(acc, x_ref[0], or_ref[0], szc_ref[0], sgr_ref[0], sgc_ref[0],
                          lng_ref, lnb_ref, wbr_ref, wbc_ref, wout_ref, gpost_ref)


def _post(x, o_r, u, szc, sgr, sgc, conv_state, p, tm):
    B, T, _ = x.shape
    sample = conv_state is not None
    tile = lambda w: pl.BlockSpec((1, tm, w), lambda b, j: (b, j, 0))
    in_specs = [tile(D_MODEL), tile(D_RWKV), tile(D_CONV), tile(D_CONV), tile(D_MODEL), tile(D_MODEL),
                _const_spec((CONV_W, D_CONV)), _const_spec((1, D_CONV)), _const_spec((1, D_CONV)),
                _const_spec((1, D_CONV)), _const_spec((D_RWKV, D_MODEL)), _const_spec((D_CONV, D_MODEL)),
                _const_spec((D_MODEL, D_MODEL)), _const_spec((1, D_MODEL))]
    args = [x, o_r, u, szc, sgr, sgc, p["conv_w"], p["conv_b"], p["ln_g"], p["ln_b"],
            p["w_br"], p["w_bc"], p["w_out"], p["gpost"]]
    out_specs = [tile(D_MODEL)]
    out_shape = [jax.ShapeDtypeStruct((B, T, D_MODEL), F32)]
    scratch = []
    if sample:
        st = pl.BlockSpec((CONV_W - 1, tm, D_CONV), lambda b, j: (0, j, 0))
        in_specs.append(st)
        args.append(conv_state)
        out_specs.append(st)
        out_shape.append(jax.ShapeDtypeStruct(conv_state.shape, F32))
    else:
        scratch = [pltpu.VMEM((tm + CONV_PAD, D_CONV), F32)]
    return pl.pallas_call(
        _post_sample_kernel if sample else _post_prompt_kernel,
        grid=(B, T // tm),
        in_specs=in_specs,
        out_specs=out_specs,
        out_shape=out_shape,
        scratch_shapes=scratch,
        compiler_params=pltpu.CompilerParams(
            dimension_semantics=("arbitrary", "arbitrary"), vmem_limit_bytes=VMEM_LIMIT),
        name="post_sample" if sample else "post_prompt",
    )(*args)


def _layer_params(l, norm_pre_g, w_in, mu_shift, decay_w0, decay_w2, iclr_a0, iclr_a2, k_k, k_a, r_k, gn_g, gn_b,
                  conv_glu_b, conv_w, conv_b, ln_c_g, ln_c_b, w_branch_r, w_branch_c, w_out, norm_post_g):
    row = lambda t: t[l].reshape(1, -1).astype(F32)
    zeros = jnp.zeros((LORA, D_RWKV), F32)
    w2c = jnp.concatenate([jnp.concatenate([decay_w2[l], zeros], axis=1),
                           jnp.concatenate([zeros, iclr_a2[l]], axis=1)], axis=0)
    blk = jnp.arange(MXU_DIM) // HEAD_DIM
    return dict(
        gpre=row(norm_pre_g), w_in=w_in[l].astype(BF16), mu=row(mu_shift), w0=row(decay_w0), a0=row(iclr_a0),
        w2c=w2c.astype(BF16), k_k=row(k_k), k_a=row(k_a), r_k=row(r_k), glu_b=row(conv_glu_b),
        ones_bd=(blk[:, None] == blk[None, :]).astype(BF16),
        gn_g=row(gn_g), gn_b=row(gn_b), conv_w=conv_w[l].astype(F32), conv_b=row(conv_b),
        ln_g=row(ln_c_g), ln_b=row(ln_c_b), w_br=w_branch_r[l].astype(BF16), w_bc=w_branch_c[l].astype(BF16),
        w_out=w_out[l].astype(BF16), gpost=row(norm_post_g))


def _prompt_layer(x, p, tm_prep, tm_wkv, tm_post):
    T = x.shape[1]
    *pre, u, szc, sgr, sgc, h_last = _prep(x, None, p, tm_prep)
    o_r, wkv = _wkv_prompt(pre, p, tm_wkv)
    (y,) = _post(x, o_r, u, szc, sgr, sgc, None, p, tm_post)
    return y, h_last[:, SUBLANES - 1], wkv, u[:, T - (CONV_W - 1):]


def _sample_layer(x, shift, wkv, conv, p, tm_post):
    ns = x.shape[0]
    xs = x.reshape(1, ns, D_MODEL)
    *pre, u, szc, sgr, sgc, h = _prep(xs, shift.reshape(1, ns, D_MODEL), p, ns)
    o_r, wkv_new = _wkv_sample(pre, jnp.transpose(wkv, (1, 2, 3, 0)), p)
    y, conv_new = _post(xs, o_r, u, szc, sgr, sgc, jnp.transpose(conv, (1, 0, 2)), p, tm_post)
    return (y.reshape(ns, 1, D_MODEL), h.reshape(ns, D_MODEL), jnp.transpose(wkv_new, (3, 0, 1, 2)),
            jnp.transpose(conv_new, (1, 0, 2)))


def kernel(x_prompt, x_sample, state_shift, state_wkv, state_conv, norm_pre_g, w_in, mu_shift, decay_w0, decay_w2,
           iclr_a0, iclr_a2, k_k, k_a, r_k, gn_g, gn_b, conv_glu_b, conv_w, conv_b, ln_c_g, ln_c_b, w_branch_r,
           w_branch_c, w_out, norm_post_g):
    depth = w_in.shape[0]
    xp, xs = x_prompt, x_sample
    outs = [[] for _ in range(6)]
    for l in range(depth):
        p = _layer_params(l, norm_pre_g, w_in, mu_shift, decay_w0, decay_w2, iclr_a0, iclr_a2, k_k, k_a, r_k, gn_g,
                          gn_b, conv_glu_b, conv_w, conv_b, ln_c_g, ln_c_b, w_branch_r, w_branch_c, w_out,
                          norm_post_g)
        xp, a1, a2, a3 = _prompt_layer(xp, p, tm_prep=256, tm_wkv=256, tm_post=512)
        xs, b1, b2, b3 = _sample_layer(xs, state_shift[l], state_wkv[l], state_conv[l], p, tm_post=64)
        for lst, val in zip(outs, (a1, a2, a3, b1, b2, b3)):
            lst.append(val)
    return (xp, xs) + tuple(jnp.stack(o) for o in outs)
```

```python
import functools
import math

import jax
import jax.numpy as jnp
from jax import lax
from jax.experimental import pallas as pl
from jax.experimental.pallas import tpu as pltpu

F32 = jnp.float32
BF16 = jnp.bfloat16

D_MODEL = 1024
HEAD_DIM = 64
N_HEADS = D_MODEL // HEAD_DIM
D_RWKV = N_HEADS * HEAD_DIM
D_CONV = D_MODEL // 2
CONV_W = 31
LORA = 64
RMS_EPS = 1e-6
GN_EPS = 64e-5
LN_EPS = 1e-5
N_SHIFT = 4 * D_RWKV + 2 * LORA
N_IN = N_SHIFT + 2 * D_CONV + D_CONV + 2 * D_MODEL

SUBLANES = 8
PAIR = 2 * HEAD_DIM
N_PAIRS = N_HEADS // 2
MXU_DIM = 256
CHUNK = 64
TILE_CHUNKS = 4
WKV_GROUP = 2
CONV_PAD = 32
VMEM_LIMIT = 56 * 1024 * 1024

_O_R, _O_K, _O_V, _O_ZR = 0, D_RWKV, 2 * D_RWKV, 3 * D_RWKV
_O_LORA = 4 * D_RWKV
_O_U = N_SHIFT
_O_ZC = _O_U + 2 * D_CONV
_O_GR = _O_ZC + D_CONV
_O_GC = _O_GR + D_MODEL


def _dot(a, b):
    return jnp.dot(a, b, preferred_element_type=F32)


def _sigmoid(x):
    return 1.0 / (1.0 + jnp.exp(-x))


def _head_sum(x, ones_bd):
    xb = x.astype(BF16)
    return jnp.concatenate([_dot(xb[:, c * MXU_DIM:(c + 1) * MXU_DIM], ones_bd)
                            for c in range(x.shape[1] // MXU_DIM)], axis=1)


def _group_norm_gate(y, bonus, szr, gng, gnb, ones_bd):
    mean = _head_sum(y, ones_bd) * (1.0 / HEAD_DIM)
    d = y - mean
    var = _head_sum(d * d, ones_bd) * (1.0 / HEAD_DIM)
    y_gn = d * lax.rsqrt(var + GN_EPS) * gng + gnb
    return ((y_gn + bonus) * szr).astype(BF16)


def _prep_kernel(sample, *refs):
    if sample:
        (x_ref, prev_ref, gpre_ref, win_ref, mu_ref, w0_ref, a0_ref, w2c_ref, kk_ref, ka_ref, rk_ref,
         glub_ref, ones_ref,
         r_out, k_out, v_out, a_out, b_out, lw_out, szr_out, bon_out, u_out, szc_out, sgr_out, sgc_out,
         h_out) = refs
    else:
        (x_ref, gpre_ref, win_ref, mu_ref, w0_ref, a0_ref, w2c_ref, kk_ref, ka_ref, rk_ref,
         glub_ref, ones_ref,
         r_out, k_out, v_out, a_out, b_out, lw_out, szr_out, bon_out, u_out, szc_out, sgr_out, sgc_out,
         h_out, carry_ref) = refs

    x = x_ref[0]
    tm = x.shape[0]
    ms = jnp.mean(x * x, axis=-1, keepdims=True)
    h = x * lax.rsqrt(ms + RMS_EPS) * gpre_ref[...]
    hb = h.astype(BF16)

    def proj(off, width):
        return _dot(hb, win_ref[:, off:off + width])

    if sample:
        h_out[0] = h
        pb = prev_ref[0].astype(BF16)
    else:
        h_out[0] = h[tm - SUBLANES:, :]

        @pl.when(pl.program_id(1) == 0)
        def _():
            carry_ref[...] = jnp.zeros_like(carry_ref)
        row0 = lax.broadcasted_iota(jnp.int32, (tm, 1), 0) == 0

    def shifted(off, width):
        p = proj(off, width)
        if sample:
            pp = _dot(pb, win_ref[:, off:off + width])
        else:
            rolled = pltpu.roll(p, shift=1, axis=0)
            pp = jnp.where(row0, carry_ref[0:1, off:off + width], rolled)
            carry_ref[0:1, off:off + width] = p[tm - 1:tm, :]
        return p + (pp - p) * mu_ref[:, off:off + width]

    ones_bd = ones_ref[...]

    lo = shifted(_O_LORA, 2 * LORA)
    k = shifted(_O_K, D_RWKV)

    lane = lax.broadcasted_iota(jnp.int32, lo.shape, 1)
    lo = jnp.where(lane < LORA, jnp.tanh(lo), lo)
    lor = _dot(lo.astype(BF16), w2c_ref[...])

    zr = shifted(_O_ZR, D_RWKV)
    szr_out[0] = (zr * _sigmoid(zr)).astype(BF16)

    kk = k * kk_ref[...]
    kk_ss = _head_sum(kk * kk, ones_bd)

    u_in = proj(_O_U, 2 * D_CONV) + glub_ref[...]
    u_out[0] = u_in[:, :D_CONV] * _sigmoid(u_in[:, D_CONV:])

    w_raw = w0_ref[...] + lor[:, :D_RWKV]
    lw_out[0] = (-math.exp(-0.5)) * _sigmoid(w_raw)
    iclr = _sigmoid(a0_ref[...] + lor[:, D_RWKV:])
    kk = kk * lax.rsqrt(kk_ss + 1e-12)
    k2 = k * (1.0 + (iclr - 1.0) * ka_ref[...])
    k_out[0] = k2
    a_out[0] = -kk
    b_out[0] = kk * iclr

    zc = proj(_O_ZC, D_CONV)
    szc_out[0] = (zc * _sigmoid(zc)).astype(BF16)
    sgr_out[0] = _sigmoid(proj(_O_GR, D_MODEL)).astype(BF16)
    sgc_out[0] = _sigmoid(proj(_O_GC, D_MODEL)).astype(BF16)

    r = shifted(_O_R, D_RWKV)
    r_out[0] = r
    v = shifted(_O_V, D_RWKV)
    v_out[0] = v
    bon_out[0] = _head_sum(r * k2 * rk_ref[...], ones_bd) * v


def _const_spec(shape):
    nd = len(shape)
    return pl.BlockSpec(shape, lambda *_: (0,) * nd, pipeline_mode=pl.Buffered(1))


def _prep(x, prev, p, tm):
    B, T, _ = x.shape
    sample = prev is not None
    tile = lambda w: pl.BlockSpec((1, tm, w), lambda b, j: (b, j, 0))
    in_specs = [tile(D_MODEL)] + ([tile(D_MODEL)] if sample else []) + [
        _const_spec((1, D_MODEL)), _const_spec((D_MODEL, N_IN)), _const_spec((1, N_SHIFT)),
        _const_spec((1, D_RWKV)), _const_spec((1, D_RWKV)), _const_spec((2 * LORA, 2 * D_RWKV)),
        _const_spec((1, D_RWKV)), _const_spec((1, D_RWKV)), _const_spec((1, D_RWKV)),
        _const_spec((1, 2 * D_CONV)), _const_spec((MXU_DIM, MXU_DIM))]
    widths = [D_RWKV] * 8 + [D_CONV, D_CONV, D_MODEL, D_MODEL]
    dtypes = [F32] * 6 + [BF16, F32, F32, BF16, BF16, BF16]
    h_spec = tile(D_MODEL) if sample else pl.BlockSpec((1, SUBLANES, D_MODEL), lambda b, j: (b, 0, 0))
    out_specs = [tile(w) for w in widths] + [h_spec]
    out_shape = [jax.ShapeDtypeStruct((B, T, w), dt) for w, dt in zip(widths, dtypes)] + [
        jax.ShapeDtypeStruct((B, T if sample else SUBLANES, D_MODEL), F32)]
    args = [x] + ([prev] if sample else []) + [
        p["gpre"], p["w_in"], p["mu"], p["w0"], p["a0"], p["w2c"], p["k_k"], p["k_a"], p["r_k"],
        p["glu_b"], p["ones_bd"]]
    return pl.pallas_call(
        functools.partial(_prep_kernel, sample),
        grid=(B, T // tm),
        in_specs=in_specs,
        out_specs=out_specs,
        out_shape=out_shape,
        scratch_shapes=[] if sample else [pltpu.VMEM((SUBLANES, N_SHIFT), F32)],
        compiler_params=pltpu.CompilerParams(
            dimension_semantics=("arbitrary", "arbitrary"), vmem_limit_bytes=VMEM_LIMIT),
        name="prep_sample" if sample else "prep_prompt",
    )(*args)


def _block_diag(x, same_head):
    return jnp.where(same_head, jnp.concatenate([x, x], axis=0), jnp.zeros((), x.dtype))


def _drain(steps):
    for _ in steps:
        pass


def _wkv_masks():
    row = lax.broadcasted_iota(jnp.int32, (CHUNK, PAIR), 0)
    col = lax.broadcasted_iota(jnp.int32, (CHUNK, PAIR), 1) % HEAD_DIM
    brow = lax.broadcasted_iota(jnp.int32, (PAIR, PAIR), 0)
    bcol = lax.broadcasted_iota(jnp.int32, (PAIR, PAIR), 1)
    trow = lax.broadcasted_iota(jnp.int32, (CHUNK, CHUNK), 0)
    tcol = lax.broadcasted_iota(jnp.int32, (CHUNK, CHUNK), 1)
    return dict(strict=row > col, incl=row >= col, eye_pair=(row == col).astype(F32),
                same_head=(brow // HEAD_DIM) == (bcol // HEAD_DIM), eye_full=brow == bcol,
                first_head=lax.broadcasted_iota(jnp.int32, (CHUNK, PAIR), 1) < HEAD_DIM,
                tri=(trow >= tcol).astype(BF16))


def _block_diag_t(x, same_head):
    return jnp.where(same_head, jnp.concatenate([x, x], axis=0).T, 0.0).astype(BF16)


def _wkv_chunk_operators(r_ref, k_ref, v_ref, a_ref, b_ref, lw_ref, first_chunk, m, tick):
    strict, incl, eye_pair, same_head = m["strict"], m["incl"], m["eye_pair"], m["same_head"]
    bd = lambda x: _block_diag(x.astype(BF16), same_head)
    bd_t = lambda x: _block_diag_t(x, same_head)

    def chunk_inputs(row0):
        rows = pl.ds(row0, CHUNK)
        lw = lw_ref[0, rows, :]
        l1 = lw.astype(BF16)
        rem = lw - l1.astype(F32)
        l2 = rem.astype(BF16)
        l3 = (rem - l2.astype(F32)).astype(BF16)
        cl = _dot(m["tri"], l1) + _dot(m["tri"], l2) + _dot(m["tri"], l3)
        e_neg = jnp.exp(-cl)
        w_end = jnp.exp(cl[CHUNK - 1:CHUNK, :])
        b_t = b_ref[0, rows, :] * e_neg
        k_t = k_ref[0, rows, :] * e_neg
        return dict(r=r_ref[0, rows, :] * jnp.exp(cl), a=(a_ref[0, rows, :] * jnp.exp(cl - lw)).astype(BF16),
                    b=b_t, k=k_t, be=(b_t * w_end).astype(BF16), ke=(k_t * w_end).astype(BF16),
                    v=v_ref[0, rows, :], w_end=w_end)

    ch = []
    for c in range(WKV_GROUP):
        ch.append(chunk_inputs((first_chunk + c) * CHUNK))
        tick()
    pairs = range(N_PAIRS)
    sls = [slice(p * PAIR, (p + 1) * PAIR) for p in pairs]
    units = [(c, p) for c in range(WKV_GROUP) for p in pairs]
    us = range(len(units))
    pick = lambda name: [ch[c][name][:, sls[p]] for c, p in units]
    at, rt, bt, kt, be, ke, vp = (pick(n) for n in ("a", "r", "b", "k", "be", "ke", "v"))
    g = [_dot(jnp.concatenate([at[u], rt[u].astype(BF16)], axis=0),
              jnp.concatenate([bd_t(bt[u]), bd_t(kt[u])], axis=1)) for u in us]
    tick()
    l_ab = [jnp.where(strict, g[u][:CHUNK, :PAIR], 0.0) for u in us]
    m_rb = [jnp.where(incl, g[u][CHUNK:, :PAIR], 0.0).astype(BF16) for u in us]
    km = [jnp.concatenate([jnp.where(strict, g[u][:CHUNK, PAIR:], 0.0),
                           jnp.where(incl, g[u][CHUNK:, PAIR:], 0.0)], axis=0).astype(BF16) for u in us]
    uy = [_dot(km[u], bd(vp[u])) for u in us]
    tick()
    uv0 = [uy[u][:CHUNK].astype(BF16) for u in us]
    t_acc = [eye_pair + l_ab[u] for u in us]
    lp = [_dot(l_ab[u].astype(BF16), bd(l_ab[u])).astype(BF16) for u in us]
    tick()
    for _ in range(4):
        prod = [_dot(lp[u], jnp.concatenate([bd(lp[u]), bd(t_acc[u])], axis=1)) for u in us]
        lp = [prod[u][:, :PAIR].astype(BF16) for u in us]
        t_acc = [t_acc[u] + prod[u][:, PAIR:] for u in us]
        tick()
    t_acc = [t_acc[u] + _dot(lp[u], bd(t_acc[u])) for u in us]
    tick()
    au = [_dot(t_acc[u].astype(BF16), jnp.concatenate([bd(at[u]), bd(uv0[u])], axis=1)) for u in us]
    tick()
    a_hat = [au[u][:, :PAIR] for u in us]
    u_v = [au[u][:, PAIR:] for u in us]
    ry = [_dot(m_rb[u], jnp.concatenate([bd(a_hat[u]), bd(u_v[u])], axis=1)) for u in us]
    tick()
    r_hat = [(rt[u] + ry[u][:, :PAIR]).astype(BF16) for u in us]
    y_v = [ry[u][:, PAIR:] + uy[u][CHUNK:] for u in us]
    phi = [_dot(a_hat[u].T.astype(BF16), be[u]) for u in us]
    tick()
    f = [_dot(jnp.concatenate([u_v[u], vp[u]], axis=0).T.astype(BF16),
              jnp.concatenate([be[u], ke[u]], axis=0)) for u in us]
    tick()
    phi = [(jnp.where(same_head, phi[u], 0.0)
            + jnp.where(m["eye_full"], ch[units[u][0]]["w_end"][:, sls[units[u][1]]], 0.0)).astype(BF16)
           for u in us]
    z = [jnp.where(m["first_head"], f[u][:HEAD_DIM, :], f[u][HEAD_DIM:, :]) for u in us]
    return r_hat, y_v, phi, z


def _wkv_recurrence_steps(slot, rh_ref, yv_ref, phi_ref, z_ref, s_ref, y_ref, m):
    pairs = range(N_PAIRS)
    state = [s_ref[p] for p in pairs]
    for c in range(TILE_CHUNKS):
        us = [c * N_PAIRS + p for p in pairs]
        ys = [_dot(rh_ref[slot, us[p]], _block_diag_t(state[p], m["same_head"])) + yv_ref[slot, us[p]]
              for p in pairs]
        state = [_dot(state[p].astype(BF16), phi_ref[slot, us[p]]) + z_ref[slot, us[p]] for p in pairs]
        y_ref[c * CHUNK:(c + 1) * CHUNK, :] = jnp.concatenate(ys, axis=1)
        yield
    for p in pairs:
        s_ref[p] = state[p]


def _wkv_prompt_kernel(nt, r_ref, k_ref, v_ref, a_ref, b_ref, lw_ref, szr_ref, bon_ref, gng_ref, gnb_ref, ones_ref,
                       o_ref, s_out, s_ref, y_ref, rh_ref, yv_ref, phi_ref, z_ref):
    s = pl.program_id(0)
    prev = jnp.maximum(s - 1, 0)
    cur_slot = s % 2

    @pl.when(s == 0)
    def _():
        for ref in (rh_ref, yv_ref, phi_ref, z_ref):
            ref[1] = jnp.zeros_like(ref[1])

    @pl.when(prev % nt == 0)
    def _():
        s_ref[...] = jnp.zeros_like(s_ref)

    m = _wkv_masks()

    def previous_tile():
        yield from _wkv_recurrence_steps(1 - cur_slot, rh_ref, yv_ref, phi_ref, z_ref, s_ref, y_ref, m)
        o_ref[0] = _group_norm_gate(y_ref[...], bon_ref[0], szr_ref[0], gng_ref[...], gnb_ref[...], ones_ref[...])

    prev_steps = previous_tile()
    for first_chunk in range(0, TILE_CHUNKS, WKV_GROUP):
        ops = _wkv_chunk_operators(r_ref, k_ref, v_ref, a_ref, b_ref, lw_ref, first_chunk, m,
                                   tick=lambda: next(prev_steps, None))
        for i in range(WKV_GROUP * N_PAIRS):
            u = first_chunk * N_PAIRS + i
            for ref, val in zip((rh_ref, yv_ref, phi_ref, z_ref), ops):
                ref[cur_slot, u] = val[i]
    _drain(prev_steps)

    @pl.when((s > 0) & (prev % nt == nt - 1))
    def _():
        for hd in range(N_HEADS):
            lo = (hd % 2) * HEAD_DIM
            s_out[0, hd] = s_ref[hd // 2][:, lo:lo + HEAD_DIM]


def _wkv_prompt(pre, p, tm):
    r, k, v, a, b, lw, szr, bon = pre
    B, T, _ = r.shape
    assert T % tm == 0 and tm == TILE_CHUNKS * CHUNK and TILE_CHUNKS % WKV_GROUP == 0, (T, tm)
    nt = T // tm
    n_tiles = B * nt
    n_units = TILE_CHUNKS * N_PAIRS
    cur = pl.BlockSpec((1, tm, D_RWKV),
                       lambda s: (jnp.minimum(s, n_tiles - 1) // nt, jnp.minimum(s, n_tiles - 1) % nt, 0))
    prv = pl.BlockSpec((1, tm, D_RWKV), lambda s: (jnp.maximum(s - 1, 0) // nt, jnp.maximum(s - 1, 0) % nt, 0))
    state = pl.BlockSpec((1, N_HEADS, HEAD_DIM, HEAD_DIM), lambda s: (jnp.maximum(s - 1, 0) // nt, 0, 0, 0))
    return pl.pallas_call(
        functools.partial(_wkv_prompt_kernel, nt),
        grid=(n_tiles + 1,),
        in_specs=[cur] * 6 + [prv] * 2 + [_const_spec((1, D_RWKV)), _const_spec((1, D_RWKV)),
                                          _const_spec((MXU_DIM, MXU_DIM))],
        out_specs=[prv, state],
        out_shape=[jax.ShapeDtypeStruct((B, T, D_RWKV), BF16),
                   jax.ShapeDtypeStruct((B, N_HEADS, HEAD_DIM, HEAD_DIM), F32)],
        scratch_shapes=[pltpu.VMEM((N_PAIRS, HEAD_DIM, PAIR), F32), pltpu.VMEM((tm, D_RWKV), F32),
                        pltpu.VMEM((2, n_units, CHUNK, PAIR), BF16), pltpu.VMEM((2, n_units, CHUNK, PAIR), F32),
                        pltpu.VMEM((2, n_units, PAIR, PAIR), BF16), pltpu.VMEM((2, n_units, CHUNK, PAIR), F32)],
        compiler_params=pltpu.CompilerParams(
            dimension_semantics=("arbitrary",), vmem_limit_bytes=VMEM_LIMIT),
        name="wkv_prompt",
    )(r, k, v, a, b, lw, szr, bon, p["gn_g"], p["gn_b"], p["ones_bd"])


def _wkv_step_kernel(r_ref, k_ref, v_ref, a_ref, b_ref, lw_ref, szr_ref, bon_ref, gng_ref, gnb_ref,
                     ones_ref, s_in, o_ref, s_out, vec_t, y_t):
    hd = pl.program_id(0)
    ns = s_in.shape[-1]

    @pl.when(hd == 0)
    def _():
        for n, ref in enumerate((a_ref, b_ref, k_ref, r_ref)):
            vec_t[n] = ref[0].T
        vec_t[4] = jnp.exp(lw_ref[0]).T
        vec_t[5] = v_ref[0].T

    base = pl.multiple_of(hd * HEAD_DIM, HEAD_DIM)
    hrows = pl.ds(base, HEAD_DIM)
    a_h, b_h, k_h, r_h, w_h = (vec_t[n, hrows, :] for n in range(5))
    sub = lax.broadcasted_iota(jnp.int32, (SUBLANES, ns), 0)

    def value_group(g, carry):
        off = pl.multiple_of(g * SUBLANES, SUBLANES)
        v_g = vec_t[5, pl.ds(base + off, SUBLANES), :]
        y_g = jnp.zeros((SUBLANES, ns), F32)
        for ii in range(SUBLANES):
            s0 = s_in[0, off + ii]
            sa = jnp.sum(s0 * a_h, axis=0, keepdims=True)
            s1 = s0 * w_h + sa * b_h + v_g[ii:ii + 1, :] * k_h
            s_out[0, off + ii] = s1
            y_g = jnp.where(sub == ii, jnp.sum(s1 * r_h, axis=0, keepdims=True), y_g)
        y_t[pl.ds(base + off, SUBLANES), :] = y_g
        return carry

    lax.fori_loop(0, HEAD_DIM // SUBLANES, value_group, 0)

    @pl.when(hd == pl.num_programs(0) - 1)
    def _():
        o_ref[0] = _group_norm_gate(y_t[...].T, bon_ref[0], szr_ref[0], gng_ref[...], gnb_ref[...], ones_ref[...])


def _wkv_sample(pre, state_t, p):
    ns = state_t.shape[-1]
    vec = pl.BlockSpec((1, ns, D_RWKV), lambda h: (0, 0, 0))
    st = pl.BlockSpec((1, HEAD_DIM, HEAD_DIM, ns), lambda h: (h, 0, 0, 0))
    return pl.pallas_call(
        _wkv_step_kernel,
        grid=(N_HEADS,),
        in_specs=[vec] * 8 + [_const_spec((1, D_RWKV)), _const_spec((1, D_RWKV)),
                              _const_spec((MXU_DIM, MXU_DIM)), st],
        out_specs=[vec, st],
        out_shape=[jax.ShapeDtypeStruct((1, ns, D_RWKV), BF16),
                   jax.ShapeDtypeStruct(state_t.shape, F32)],
        scratch_shapes=[pltpu.VMEM((6, D_RWKV, ns), F32), pltpu.VMEM((D_RWKV, ns), F32)],
        compiler_params=pltpu.CompilerParams(
            dimension_semantics=("arbitrary",), vmem_limit_bytes=VMEM_LIMIT),
        name="wkv_sample",
    )(*pre, p["gn_g"], p["gn_b"], p["ones_bd"], state_t)


def _conv_tile(u, cw_ref, cb_ref, ubuf):
    tm = u.shape[0]
    ubuf[CONV_PAD:CONV_PAD + tm, :] = u
    win = ubuf[...]
    rows = tm + CONV_PAD
    base = CONV_PAD - (CONV_W - 1)
    acc = jnp.broadcast_to(cb_ref[...], u.shape)
    for s in range(SUBLANES):
        shifted = win if s == 0 else pltpu.roll(win, shift=rows - s, axis=0)
        for kk in range(CONV_W):
            if (base + kk) % SUBLANES == s:
                q = base + kk - s
                acc = acc + cw_ref[kk:kk + 1, :] * shifted[q:q + tm, :]
    ubuf[0:CONV_PAD, :] = win[tm:tm + CONV_PAD, :]
    return acc


def _post_tail(conv, x, o_r, szc, sgr, sgc, lng_ref, lnb_ref, wbr_ref, wbc_ref, wout_ref, gpost_ref):
    cm = jnp.mean(conv, axis=-1, keepdims=True)
    d = conv - cm
    cv = jnp.mean(d * d, axis=-1, keepdims=True)
    cn = d * lax.rsqrt(cv + LN_EPS) * lng_ref[...] + lnb_ref[...]
    o_c = (cn * _sigmoid(cn)) * szc
    branch_c = _dot(o_c.astype(BF16), wbc_ref[...])
    branch_r = _dot(o_r, wbr_ref[...])
    merged = sgr * branch_r + sgc * branch_c
    out = _dot(merged.astype(BF16), wout_ref[...])
    ms = jnp.mean(out * out, axis=-1, keepdims=True)
    return x + out * lax.rsqrt(ms + RMS_EPS) * gpost_ref[...]


def _post_prompt_kernel(x_ref, or_ref, u_ref, szc_ref, sgr_ref, sgc_ref, cw_ref, cb_ref, lng_ref, lnb_ref,
                        wbr_ref, wbc_ref, wout_ref, gpost_ref, y_ref, ubuf):
    @pl.when(pl.program_id(1) == 0)
    def _():
        ubuf[0:CONV_PAD, :] = jnp.zeros((CONV_PAD, D_CONV), F32)

    conv = _conv_tile(u_ref[0], cw_ref, cb_ref, ubuf)
    y_ref[0] = _post_tail(conv, x_ref[0], or_ref[0], szc_ref[0], sgr_ref[0], sgc_ref[0],
                          lng_ref, lnb_ref, wbr_ref, wbc_ref, wout_ref, gpost_ref)


def _post_sample_kernel(x_ref, or_ref, u_ref, szc_ref, sgr_ref, sgc_ref, cw_ref, cb_ref, lng_ref, lnb_ref,
                        wbr_ref, wbc_ref, wout_ref, gpost_ref, st_ref, y_ref, st_out):
    u = u_ref[0]
    acc = jnp.broadcast_to(cb_ref[...], u.shape)
    for kk in range(CONV_W - 1):
        acc = acc + cw_ref[kk:kk + 1, :] * st_ref[kk]
    acc = acc + cw_ref[CONV_W - 1:CONV_W, :] * u
    st_out[0:CONV_W - 2] = st_ref[1:CONV_W - 1]
    st_out[CONV_W - 2] = u
    y_ref[0] = _post_tail(acc, x_ref[0], or_ref[0], szc_ref[0], sgr_ref[0], sgc_ref[0],
                          lng_ref, lnb_ref, wbr_ref, wbc_ref, wout_ref, gpost_ref)


def _post(x, o_r, u, szc, sgr, sgc, conv_state, p, tm):
    B, T, _ = x.shape
    sample = conv_state is not None
    tile = lambda w: pl.BlockSpec((1, tm, w), lambda b, j: (b, j, 0))
    in_specs = [tile(D_MODEL), tile(D_RWKV), tile(D_CONV), tile(D_CONV), tile(D_MODEL), tile(D_MODEL),
                _const_spec((CONV_W, D_CONV)), _const_spec((1, D_CONV)), _const_spec((1, D_CONV)),
                _const_spec((1, D_CONV)), _const_spec((D_RWKV, D_MODEL)), _const_spec((D_CONV, D_MODEL)),
                _const_spec((D_MODEL, D_MODEL)), _const_spec((1, D_MODEL))]
    args = [x, o_r, u, szc, sgr, sgc, p["conv_w"], p["conv_b"], p["ln_g"], p["ln_b"],
            p["w_br"], p["w_bc"], p["w_out"], p["gpost"]]
    out_specs = [tile(D_MODEL)]
    out_shape = [jax.ShapeDtypeStruct((B, T, D_MODEL), F32)]
    scratch = []
    if sample:
        st = pl.BlockSpec((CONV_W - 1, tm, D_CONV), lambda b, j: (0, j, 0))
        in_specs.append(st)
        args.append(conv_state)
        out_specs.append(st)
        out_shape.append(jax.ShapeDtypeStruct(conv_state.shape, F32))
    else:
        scratch = [pltpu.VMEM((tm + CONV_PAD, D_CONV), F32)]
    return pl.pallas_call(
        _post_sample_kernel if sample else _post_prompt_kernel,
        grid=(B, T // tm),
        in_specs=in_specs,
        out_specs=out_specs,
        out_shape=out_shape,
        scratch_shapes=scratch,
        compiler_params=pltpu.CompilerParams(
            dimension_semantics=("arbitrary", "arbitrary"), vmem_limit_bytes=VMEM_LIMIT),
        name="post_sample" if sample else "post_prompt",
    )(*args)


def _layer_params(l, norm_pre_g, w_in, mu_shift, decay_w0, decay_w2, iclr_a0, iclr_a2, k_k, k_a, r_k, gn_g, gn_b,
                  conv_glu_b, conv_w, conv_b, ln_c_g, ln_c_b, w_branch_r, w_branch_c, w_out, norm_post_g):
    row = lambda t: t[l].reshape(1, -1).astype(F32)
    zeros = jnp.zeros((LORA, D_RWKV), F32)
    w2c = jnp.concatenate([jnp.concatenate([decay_w2[l], zeros], axis=1),
                           jnp.concatenate([zeros, iclr_a2[l]], axis=1)], axis=0)
    blk = jnp.arange(MXU_DIM) // HEAD_DIM
    return dict(
        gpre=row(norm_pre_g), w_in=w_in[l].astype(BF16), mu=row(mu_shift), w0=row(decay_w0), a0=row(iclr_a0),
        w2c=w2c.astype(BF16), k_k=row(k_k), k_a=row(k_a), r_k=row(r_k), glu_b=row(conv_glu_b),
        ones_bd=(blk[:, None] == blk[None, :]).astype(BF16),
        gn_g=row(gn_g), gn_b=row(gn_b), conv_w=conv_w[l].astype(F32), conv_b=row(conv_b),
        ln_g=row(ln_c_g), ln_b=row(ln_c_b), w_br=w_branch_r[l].astype(BF16), w_bc=w_branch_c[l].astype(BF16),
        w_out=w_out[l].astype(BF16), gpost=row(norm_post_g))


def _prompt_layer(x, p, tm_prep, tm_wkv, tm_post):
    T = x.shape[1]
    *pre, u, szc, sgr, sgc, h_last = _prep(x, None, p, tm_prep)
    o_r, wkv = _wkv_prompt(pre, p, tm_wkv)
    (y,) = _post(x, o_r, u, szc, sgr, sgc, None, p, tm_post)
    return y, h_last[:, SUBLANES - 1], wkv, u[:, T - (CONV_W - 1):]


def _sample_layer(x, shift, wkv, conv, p, tm_post):
    ns = x.shape[0]
    xs = x.reshape(1, ns, D_MODEL)
    *pre, u, szc, sgr, sgc, h = _prep(xs, shift.reshape(1, ns, D_MODEL), p, ns)
    o_r, wkv_new = _wkv_sample(pre, jnp.transpose(wkv, (1, 2, 3, 0)), p)
    y, conv_new = _post(xs, o_r, u, szc, sgr, sgc, jnp.transpose(conv, (1, 0, 2)), p, tm_post)
    return (y.reshape(ns, 1, D_MODEL), h.reshape(ns, D_MODEL), jnp.transpose(wkv_new, (3, 0, 1, 2)),
            jnp.transpose(conv_new, (1, 0, 2)))


def kernel(x_prompt, x_sample, state_shift, state_wkv, state_conv, norm_pre_g, w_in, mu_shift, decay_w0, decay_w2,
           iclr_a0, iclr_a2, k_k, k_a, r_k, gn_g, gn_b, conv_glu_b, conv_w, conv_b, ln_c_g, ln_c_b, w_branch_r,
           w_branch_c, w_out, norm_post_g):
    depth = w_in.shape[0]
    xp, xs = x_prompt, x_sample
    outs = [[] for _ in range(6)]
    for l in range(depth):
        p = _layer_params(l, norm_pre_g, w_in, mu_shift, decay_w0, decay_w2, iclr_a0, iclr_a2, k_k, k_a, r_k, gn_g,
                          gn_b, conv_glu_b, conv_w, conv_b, ln_c_g, ln_c_b, w_branch_r, w_branch_c, w_out,
                          norm_post_g)
        xp, a1, a2, a3 = _prompt_layer(xp, p, tm_prep=256, tm_wkv=256, tm_post=512)
        xs, b1, b2, b3 = _sample_layer(xs, state_shift[l], state_wkv[l], state_conv[l], p, tm_post=64)
        for lst, val in zip(outs, (a1, a2, a3, b1, b2, b3)):
            lst.append(val)
    return (xp, xs) + tuple(jnp.stack(o) for o in outs)
```

```python
import functools
import math

import jax
import jax.numpy as jnp
from jax import lax
from jax.experimental import pallas as pl
from jax.experimental.pallas import tpu as pltpu

F32 = jnp.float32
BF16 = jnp.bfloat16

D_MODEL = 1024
HEAD_DIM = 64
N_HEADS = D_MODEL // HEAD_DIM
D_RWKV = N_HEADS * HEAD_DIM
D_CONV = D_MODEL // 2
CONV_W = 31
LORA = 64
RMS_EPS = 1e-6
GN_EPS = 64e-5
LN_EPS = 1e-5
N_SHIFT = 4 * D_RWKV + 2 * LORA
N_IN = N_SHIFT + 2 * D_CONV + D_CONV + 2 * D_MODEL

SUBLANES = 8
PAIR = 2 * HEAD_DIM
N_PAIRS = N_HEADS // 2
MXU_DIM = 256
CHUNK = 64
TILE_CHUNKS = 4
WKV_GROUP = 2
CONV_PAD = 32
VMEM_LIMIT = 56 * 1024 * 1024

_O_R, _O_K, _O_V, _O_ZR = 0, D_RWKV, 2 * D_RWKV, 3 * D_RWKV
_O_LORA = 4 * D_RWKV
_O_U = N_SHIFT
_O_ZC = _O_U + 2 * D_CONV
_O_GR = _O_ZC + D_CONV
_O_GC = _O_GR + D_MODEL


def _dot(a, b):
    return jnp.dot(a, b, preferred_element_type=F32)


def _sigmoid(x):
    return 1.0 / (1.0 + jnp.exp(-x))


def _head_sum(x, ones_bd):
    xb = x.astype(BF16)
    return jnp.concatenate([_dot(xb[:, c * MXU_DIM:(c + 1) * MXU_DIM], ones_bd)
                            for c in range(x.shape[1] // MXU_DIM)], axis=1)


def _group_norm_gate(y, bonus, szr, gng, gnb, ones_bd):
    mean = _head_sum(y, ones_bd) * (1.0 / HEAD_DIM)
    d = y - mean
    var = _head_sum(d * d, ones_bd) * (1.0 / HEAD_DIM)
    y_gn = d * lax.rsqrt(var + GN_EPS) * gng + gnb
    return ((y_gn + bonus) * szr).astype(BF16)


def _prep_kernel(sample, *refs):
    if sample:
        (x_ref, prev_ref, gpre_ref, win_ref, mu_ref, w0_ref, a0_ref, w2c_ref, kk_ref, ka_ref, rk_ref,
         glub_ref, ones_ref,
         r_out, k_out, v_out, a_out, b_out, lw_out, szr_out, bon_out, u_out, szc_out, sgr_out, sgc_out,
         h_out) = refs
    else:
        (x_ref, gpre_ref, win_ref, mu_ref, w0_ref, a0_ref, w2c_ref, kk_ref, ka_ref, rk_ref,
         glub_ref, ones_ref,
         r_out, k_out, v_out, a_out, b_out, lw_out, szr_out, bon_out, u_out, szc_out, sgr_out, sgc_out,
         h_out, carry_ref) = refs

    x = x_ref[0]
    tm = x.shape[0]
    ms = jnp.mean(x * x, axis=-1, keepdims=True)
    h = x * lax.rsqrt(ms + RMS_EPS) * gpre_ref[...]
    hb = h.astype(BF16)

    def proj(off, width):
        return _dot(hb, win_ref[:, off:off + width])

    if sample:
        h_out[0] = h
        pb = prev_ref[0].astype(BF16)
    else:
        h_out[0] = h[tm - SUBLANES:, :]

        @pl.when(pl.program_id(1) == 0)
        def _():
            carry_ref[...] = jnp.zeros_like(carry_ref)
        row0 = lax.broadcasted_iota(jnp.int32, (tm, 1), 0) == 0

    def shifted(off, width):
        p = proj(off, width)
        if sample:
            pp = _dot(pb, win_ref[:, off:off + width])
        else:
            rolled = pltpu.roll(p, shift=1, axis=0)
            pp = jnp.where(row0, carry_ref[0:1, off:off + width], rolled)
            carry_ref[0:1, off:off + width] = p[tm - 1:tm, :]
        return p + (pp - p) * mu_ref[:, off:off + width]

    ones_bd = ones_ref[...]

    lo = shifted(_O_LORA, 2 * LORA)
    k = shifted(_O_K, D_RWKV)

    lane = lax.broadcasted_iota(jnp.int32, lo.shape, 1)
    lo = jnp.where(lane < LORA, jnp.tanh(lo), lo)
    lor = _dot(lo.astype(BF16), w2c_ref[...])

    zr = shifted(_O_ZR, D_RWKV)
    szr_out[0] = zr * _sigmoid(zr)

    kk = k * kk_ref[...]
    kk_ss = _head_sum(kk * kk, ones_bd)

    u_in = proj(_O_U, 2 * D_CONV) + glub_ref[...]
    u_out[0] = u_in[:, :D_CONV] * _sigmoid(u_in[:, D_CONV:])

    w_raw = w0_ref[...] + lor[:, :D_RWKV]
    lw_out[0] = (-math.exp(-0.5)) * _sigmoid(w_raw)
    iclr = _sigmoid(a0_ref[...] + lor[:, D_RWKV:])
    kk = kk * lax.rsqrt(kk_ss + 1e-12)
    k2 = k * (1.0 + (iclr - 1.0) * ka_ref[...])
    k_out[0] = k2
    a_out[0] = -kk
    b_out[0] = kk * iclr

    zc = proj(_O_ZC, D_CONV)
    szc_out[0] = zc * _sigmoid(zc)
    sgr_out[0] = _sigmoid(proj(_O_GR, D_MODEL))
    sgc_out[0] = _sigmoid(proj(_O_GC, D_MODEL))

    r = shifted(_O_R, D_RWKV)
    r_out[0] = r
    v = shifted(_O_V, D_RWKV)
    v_out[0] = v
    bon_out[0] = _head_sum(r * k2 * rk_ref[...], ones_bd) * v


def _const_spec(shape):
    nd = len(shape)
    return pl.BlockSpec(shape, lambda *_: (0,) * nd, pipeline_mode=pl.Buffered(1))


def _prep(x, prev, p, tm):
    B, T, _ = x.shape
    sample = prev is not None
    tile = lambda w: pl.BlockSpec((1, tm, w), lambda b, j: (b, j, 0))
    in_specs = [tile(D_MODEL)] + ([tile(D_MODEL)] if sample else []) + [
        _const_spec((1, D_MODEL)), _const_spec((D_MODEL, N_IN)), _const_spec((1, N_SHIFT)),
        _const_spec((1, D_RWKV)), _const_spec((1, D_RWKV)), _const_spec((2 * LORA, 2 * D_RWKV)),
        _const_spec((1, D_RWKV)), _const_spec((1, D_RWKV)), _const_spec((1, D_RWKV)),
        _const_spec((1, 2 * D_CONV)), _const_spec((MXU_DIM, MXU_DIM))]
    widths = [D_RWKV] * 8 + [D_CONV, D_CONV, D_MODEL, D_MODEL]
    h_spec = tile(D_MODEL) if sample else pl.BlockSpec((1, SUBLANES, D_MODEL), lambda b, j: (b, 0, 0))
    out_specs = [tile(w) for w in widths] + [h_spec]
    out_shape = [jax.ShapeDtypeStruct((B, T, w), F32) for w in widths] + [
        jax.ShapeDtypeStruct((B, T if sample else SUBLANES, D_MODEL), F32)]
    args = [x] + ([prev] if sample else []) + [
        p["gpre"], p["w_in"], p["mu"], p["w0"], p["a0"], p["w2c"], p["k_k"], p["k_a"], p["r_k"],
        p["glu_b"], p["ones_bd"]]
    return pl.pallas_call(
        functools.partial(_prep_kernel, sample),
        grid=(B, T // tm),
        in_specs=in_specs,
        out_specs=out_specs,
        out_shape=out_shape,
        scratch_shapes=[] if sample else [pltpu.VMEM((SUBLANES, N_SHIFT), F32)],
        compiler_params=pltpu.CompilerParams(
            dimension_semantics=("arbitrary", "arbitrary"), vmem_limit_bytes=VMEM_LIMIT),
        name="prep_sample" if sample else "prep_prompt",
    )(*args)


def _block_diag(x, same_head):
    return jnp.where(same_head, jnp.concatenate([x, x], axis=0), jnp.zeros((), x.dtype))


def _drain(steps):
    for _ in steps:
        pass


def _wkv_masks():
    row = lax.broadcasted_iota(jnp.int32, (CHUNK, PAIR), 0)
    col = lax.broadcasted_iota(jnp.int32, (CHUNK, PAIR), 1) % HEAD_DIM
    brow = lax.broadcasted_iota(jnp.int32, (PAIR, PAIR), 0)
    bcol = lax.broadcasted_iota(jnp.int32, (PAIR, PAIR), 1)
    trow = lax.broadcasted_iota(jnp.int32, (CHUNK, CHUNK), 0)
    tcol = lax.broadcasted_iota(jnp.int32, (CHUNK, CHUNK), 1)
    return dict(strict=row > col, incl=row >= col, eye_pair=(row == col).astype(F32),
                same_head=(brow // HEAD_DIM) == (bcol // HEAD_DIM), eye_full=brow == bcol,
                first_head=lax.broadcasted_iota(jnp.int32, (CHUNK, PAIR), 1) < HEAD_DIM,
                tri=(trow >= tcol).astype(BF16))


def _block_diag_t(x, same_head):
    return jnp.where(same_head, jnp.concatenate([x, x], axis=0).T, 0.0).astype(BF16)


def _wkv_chunk_operators(r_ref, k_ref, v_ref, a_ref, b_ref, lw_ref, first_chunk, m, tick):
    strict, incl, eye_pair, same_head = m["strict"], m["incl"], m["eye_pair"], m["same_head"]
    bd = lambda x: _block_diag(x.astype(BF16), same_head)
    bd_t = lambda x: _block_diag_t(x, same_head)

    def chunk_inputs(row0):
        rows = pl.ds(row0, CHUNK)
        lw = lw_ref[0, rows, :]
        l1 = lw.astype(BF16)
        l2 = (lw - l1.astype(F32)).astype(BF16)
        cl = _dot(m["tri"], l1) + _dot(m["tri"], l2)
        e_neg = jnp.exp(-cl)
        w_end = jnp.exp(cl[CHUNK - 1:CHUNK, :])
        b_t = b_ref[0, rows, :] * e_neg
        k_t = k_ref[0, rows, :] * e_neg
        return dict(r=r_ref[0, rows, :] * jnp.exp(cl), a=(a_ref[0, rows, :] * jnp.exp(cl - lw)).astype(BF16),
                    b=b_t, k=k_t, be=(b_t * w_end).astype(BF16), ke=(k_t * w_end).astype(BF16),
                    v=v_ref[0, rows, :], w_end=w_end)

    ch = []
    for c in range(WKV_GROUP):
        ch.append(chunk_inputs((first_chunk + c) * CHUNK))
        tick()
    pairs = range(N_PAIRS)
    sls = [slice(p * PAIR, (p + 1) * PAIR) for p in pairs]
    units = [(c, p) for c in range(WKV_GROUP) for p in pairs]
    us = range(len(units))
    pick = lambda name: [ch[c][name][:, sls[p]] for c, p in units]
    at, rt, bt, kt, be, ke, vp = (pick(n) for n in ("a", "r", "b", "k", "be", "ke", "v"))
    g = [_dot(jnp.concatenate([at[u], rt[u].astype(BF16)], axis=0),
              jnp.concatenate([bd_t(bt[u]), bd_t(kt[u])], axis=1)) for u in us]
    tick()
    l_ab = [jnp.where(strict, g[u][:CHUNK, :PAIR], 0.0) for u in us]
    m_rb = [jnp.where(incl, g[u][CHUNK:, :PAIR], 0.0).astype(BF16) for u in us]
    km = [jnp.concatenate([jnp.where(strict, g[u][:CHUNK, PAIR:], 0.0),
                           jnp.where(incl, g[u][CHUNK:, PAIR:], 0.0)], axis=0).astype(BF16) for u in us]
    uy = [_dot(km[u], bd(vp[u])) for u in us]
    tick()
    uv0 = [uy[u][:CHUNK].astype(BF16) for u in us]
    t_acc = [eye_pair + l_ab[u] for u in us]
    lp = [_dot(l_ab[u].astype(BF16), bd(l_ab[u])).astype(BF16) for u in us]
    tick()
    for _ in range(4):
        prod = [_dot(lp[u], jnp.concatenate([bd(lp[u]), bd(t_acc[u])], axis=1)) for u in us]
        lp = [prod[u][:, :PAIR].astype(BF16) for u in us]
        t_acc = [t_acc[u] + prod[u][:, PAIR:] for u in us]
        tick()
    t_acc = [t_acc[u] + _dot(lp[u], bd(t_acc[u])) for u in us]
    tick()
    au = [_dot(t_acc[u].astype(BF16), jnp.concatenate([bd(at[u]), bd(uv0[u])], axis=1)) for u in us]
    tick()
    a_hat = [au[u][:, :PAIR] for u in us]
    u_v = [au[u][:, PAIR:] for u in us]
    ry = [_dot(m_rb[u], jnp.concatenate([bd(a_hat[u]), bd(u_v[u])], axis=1)) for u in us]
    tick()
    r_hat = [(rt[u] + ry[u][:, :PAIR]).astype(BF16) for u in us]
    y_v = [ry[u][:, PAIR:] + uy[u][CHUNK:] for u in us]
    phi = [_dot(a_hat[u].T.astype(BF16), be[u]) for u in us]
    tick()
    f = [_dot(jnp.concatenate([u_v[u], vp[u]], axis=0).T.astype(BF16),
              jnp.concatenate([be[u], ke[u]], axis=0)) for u in us]
    tick()
    phi = [(jnp.where(same_head, phi[u], 0.0)
            + jnp.where(m["eye_full"], ch[units[u][0]]["w_end"][:, sls[units[u][1]]], 0.0)).astype(BF16)
           for u in us]
    z = [jnp.where(m["first_head"], f[u][:HEAD_DIM, :], f[u][HEAD_DIM:, :]) for u in us]
    return r_hat, y_v, phi, z


def _wkv_recurrence_steps(slot, rh_ref, yv_ref, phi_ref, z_ref, s_ref, y_ref, m):
    pairs = range(N_PAIRS)
    state = [s_ref[p] for p in pairs]
    for c in range(TILE_CHUNKS):
        us = [c * N_PAIRS + p for p in pairs]
        ys = [_dot(rh_ref[slot, us[p]], _block_diag_t(state[p], m["same_head"])) + yv_ref[slot, us[p]]
              for p in pairs]
        state = [_dot(state[p].astype(BF16), phi_ref[slot, us[p]]) + z_ref[slot, us[p]] for p in pairs]
        y_ref[c * CHUNK:(c + 1) * CHUNK, :] = jnp.concatenate(ys, axis=1)
        yield
    for p in pairs:
        s_ref[p] = state[p]


def _wkv_prompt_kernel(nt, r_ref, k_ref, v_ref, a_ref, b_ref, lw_ref, szr_ref, bon_ref, gng_ref, gnb_ref, ones_ref,
                       o_ref, s_out, s_ref, y_ref, rh_ref, yv_ref, phi_ref, z_ref):
    s = pl.program_id(0)
    prev = jnp.maximum(s - 1, 0)
    cur_slot = s % 2

    @pl.when(s == 0)
    def _():
        for ref in (rh_ref, yv_ref, phi_ref, z_ref):
            ref[1] = jnp.zeros_like(ref[1])

    @pl.when(prev % nt == 0)
    def _():
        s_ref[...] = jnp.zeros_like(s_ref)

    m = _wkv_masks()

    def previous_tile():
        yield from _wkv_recurrence_steps(1 - cur_slot, rh_ref, yv_ref, phi_ref, z_ref, s_ref, y_ref, m)
        o_ref[0] = _group_norm_gate(y_ref[...], bon_ref[0], szr_ref[0], gng_ref[...], gnb_ref[...], ones_ref[...])

    prev_steps = previous_tile()
    for first_chunk in range(0, TILE_CHUNKS, WKV_GROUP):
        ops = _wkv_chunk_operators(r_ref, k_ref, v_ref, a_ref, b_ref, lw_ref, first_chunk, m,
                                   tick=lambda: next(prev_steps, None))
        for i in range(WKV_GROUP * N_PAIRS):
            u = first_chunk * N_PAIRS + i
            for ref, val in zip((rh_ref, yv_ref, phi_ref, z_ref), ops):
                ref[cur_slot, u] = val[i]
    _drain(prev_steps)

    @pl.when((s > 0) & (prev % nt == nt - 1))
    def _():
        for hd in range(N_HEADS):
            lo = (hd % 2) * HEAD_DIM
            s_out[0, hd] = s_ref[hd // 2][:, lo:lo + HEAD_DIM]


def _wkv_prompt(pre, p, tm):
    r, k, v, a, b, lw, szr, bon = pre
    B, T, _ = r.shape
    assert T % tm == 0 and tm == TILE_CHUNKS * CHUNK and TILE_CHUNKS % WKV_GROUP == 0, (T, tm)
    nt = T // tm
    n_tiles = B * nt
    n_units = TILE_CHUNKS * N_PAIRS
    cur = pl.BlockSpec((1, tm, D_RWKV),
                       lambda s: (jnp.minimum(s, n_tiles - 1) // nt, jnp.minimum(s, n_tiles - 1) % nt, 0))
    prv = pl.BlockSpec((1, tm, D_RWKV), lambda s: (jnp.maximum(s - 1, 0) // nt, jnp.maximum(s - 1, 0) % nt, 0))
    state = pl.BlockSpec((1, N_HEADS, HEAD_DIM, HEAD_DIM), lambda s: (jnp.maximum(s - 1, 0) // nt, 0, 0, 0))
    return pl.pallas_call(
        functools.partial(_wkv_prompt_kernel, nt),
        grid=(n_tiles + 1,),
        in_specs=[cur] * 6 + [prv] * 2 + [_const_spec((1, D_RWKV)), _const_spec((1, D_RWKV)),
                                          _const_spec((MXU_DIM, MXU_DIM))],
        out_specs=[prv, state],
        out_shape=[jax.ShapeDtypeStruct((B, T, D_RWKV), BF16),
                   jax.ShapeDtypeStruct((B, N_HEADS, HEAD_DIM, HEAD_DIM), F32)],
        scratch_shapes=[pltpu.VMEM((N_PAIRS, HEAD_DIM, PAIR), F32), pltpu.VMEM((tm, D_RWKV), F32),
                        pltpu.VMEM((2, n_units, CHUNK, PAIR), BF16), pltpu.VMEM((2, n_units, CHUNK, PAIR), F32),
                        pltpu.VMEM((2, n_units, PAIR, PAIR), BF16), pltpu.VMEM((2, n_units, CHUNK, PAIR), F32)],
        compiler_params=pltpu.CompilerParams(
            dimension_semantics=("arbitrary",), vmem_limit_bytes=VMEM_LIMIT),
        name="wkv_prompt",
    )(r, k, v, a, b, lw, szr, bon, p["gn_g"], p["gn_b"], p["ones_bd"])


def _wkv_step_kernel(r_ref, k_ref, v_ref, a_ref, b_ref, lw_ref, szr_ref, bon_ref, gng_ref, gnb_ref,
                     ones_ref, s_in, o_ref, s_out, vec_t, y_t):
    hd = pl.program_id(0)
    ns = s_in.shape[-1]

    @pl.when(hd == 0)
    def _():
        for n, ref in enumerate((a_ref, b_ref, k_ref, r_ref)):
            vec_t[n] = ref[0].T
        vec_t[4] = jnp.exp(lw_ref[0]).T
        vec_t[5] = v_ref[0].T

    base = pl.multiple_of(hd * HEAD_DIM, HEAD_DIM)
    hrows = pl.ds(base, HEAD_DIM)
    a_h, b_h, k_h, r_h, w_h = (vec_t[n, hrows, :] for n in range(5))
    sub = lax.broadcasted_iota(jnp.int32, (SUBLANES, ns), 0)

    def value_group(g, carry):
        off = pl.multiple_of(g * SUBLANES, SUBLANES)
        v_g = vec_t[5, pl.ds(base + off, SUBLANES), :]
        y_g = jnp.zeros((SUBLANES, ns), F32)
        for ii in range(SUBLANES):
            s0 = s_in[0, off + ii]
            sa = jnp.sum(s0 * a_h, axis=0, keepdims=True)
            s1 = s0 * w_h + sa * b_h + v_g[ii:ii + 1, :] * k_h
            s_out[0, off + ii] = s1
            y_g = jnp.where(sub == ii, jnp.sum(s1 * r_h, axis=0, keepdims=True), y_g)
        y_t[pl.ds(base + off, SUBLANES), :] = y_g
        return carry

    lax.fori_loop(0, HEAD_DIM // SUBLANES, value_group, 0)

    @pl.when(hd == pl.num_programs(0) - 1)
    def _():
        o_ref[0] = _group_norm_gate(y_t[...].T, bon_ref[0], szr_ref[0], gng_ref[...], gnb_ref[...], ones_ref[...])


def _wkv_sample(pre, state_t, p):
    ns = state_t.shape[-1]
    vec = pl.BlockSpec((1, ns, D_RWKV), lambda h: (0, 0, 0))
    st = pl.BlockSpec((1, HEAD_DIM, HEAD_DIM, ns), lambda h: (h, 0, 0, 0))
    return pl.pallas_call(
        _wkv_step_kernel,
        grid=(N_HEADS,),
        in_specs=[vec] * 8 + [_const_spec((1, D_RWKV)), _const_spec((1, D_RWKV)),
                              _const_spec((MXU_DIM, MXU_DIM)), st],
        out_specs=[vec, st],
        out_shape=[jax.ShapeDtypeStruct((1, ns, D_RWKV), BF16),
                   jax.ShapeDtypeStruct(state_t.shape, F32)],
        scratch_shapes=[pltpu.VMEM((6, D_RWKV, ns), F32), pltpu.VMEM((D_RWKV, ns), F32)],
        compiler_params=pltpu.CompilerParams(
            dimension_semantics=("arbitrary",), vmem_limit_bytes=VMEM_LIMIT),
        name="wkv_sample",
    )(*pre, p["gn_g"], p["gn_b"], p["ones_bd"], state_t)


def _conv_tile(u, cw_ref, cb_ref, ubuf):
    tm = u.shape[0]
    ubuf[CONV_PAD:CONV_PAD + tm, :] = u
    win = ubuf[...]
    rows = tm + CONV_PAD
    base = CONV_PAD - (CONV_W - 1)
    acc = jnp.broadcast_to(cb_ref[...], u.shape)
    for s in range(SUBLANES):
        shifted = win if s == 0 else pltpu.roll(win, shift=rows - s, axis=0)
        for kk in range(CONV_W):
            if (base + kk) % SUBLANES == s:
                q = base + kk - s
                acc = acc + cw_ref[kk:kk + 1, :] * shifted[q:q + tm, :]
    ubuf[0:CONV_PAD, :] = win[tm:tm + CONV_PAD, :]
    return acc


def _post_tail(conv, x, o_r, szc, sgr, sgc, lng_ref, lnb_ref, wbr_ref, wbc_ref, wout_ref, gpost_ref):
    cm = jnp.mean(conv, axis=-1, keepdims=True)
    d = conv - cm
    cv = jnp.mean(d * d, axis=-1, keepdims=True)
    cn = d * lax.rsqrt(cv + LN_EPS) * lng_ref[...] + lnb_ref[...]
    o_c = (cn * _sigmoid(cn)) * szc
    branch_c = _dot(o_c.astype(BF16), wbc_ref[...])
    branch_r = _dot(o_r, wbr_ref[...])
    merged = sgr * branch_r + sgc * branch_c
    out = _dot(merged.astype(BF16), wout_ref[...])
    ms = jnp.mean(out * out, axis=-1, keepdims=True)
    return x + out * lax.rsqrt(ms + RMS_EPS) * gpost_ref[...]


def _post_prompt_kernel(x_ref, or_ref, u_ref, szc_ref, sgr_ref, sgc_ref, cw_ref, cb_ref, lng_ref, lnb_ref,
                        wbr_ref, wbc_ref, wout_ref, gpost_ref, y_ref, ubuf):
    @pl.when(pl.program_id(1) == 0)
    def _():
        ubuf[0:CONV_PAD, :] = jnp.zeros((CONV_PAD, D_CONV), F32)

    conv = _conv_tile(u_ref[0], cw_ref, cb_ref, ubuf)
    y_ref[0] = _post_tail(conv, x_ref[0], or_ref[0], szc_ref[0], sgr_ref[0], sgc_ref[0],
                          lng_ref, lnb_ref, wbr_ref, wbc_ref, wout_ref, gpost_ref)


def _post_sample_kernel(x_ref, or_ref, u_ref, szc_ref, sgr_ref, sgc_ref, cw_ref, cb_ref, lng_ref, lnb_ref,
                        wbr_ref, wbc_ref, wout_ref, gpost_ref, st_ref, y_ref, st_out):
    u = u_ref[0]
    acc = jnp.broadcast_to(cb_ref[...], u.shape)
    for kk in range(CONV_W - 1):
        acc = acc + cw_ref[kk:kk + 1, :] * st_ref[kk]
    acc = acc + cw_ref[CONV_W - 1:CONV_W, :] * u
    st_out[0:CONV_W - 2] = st_ref[1:CONV_W - 1]
    st_out[CONV_W - 2] = u
    y_ref[0] = _post_tail(acc, x_ref[0], or_ref[0], szc_ref[0], sgr_ref[0], sgc_ref[0],
                          lng_ref, lnb_ref, wbr_ref, wbc_ref, wout_ref, gpost_ref)


def _post(x, o_r, u, szc, sgr, sgc, conv_state, p, tm):
    B, T, _ = x.shape
    sample = conv_state is not None
    tile = lambda w: pl.BlockSpec((1, tm, w), lambda b, j: (b, j, 0))
    in_specs = [tile(D_MODEL), tile(D_RWKV), tile(D_CONV), tile(D_CONV), tile(D_MODEL), tile(D_MODEL),
                _const_spec((CONV_W, D_CONV)), _const_spec((1, D_CONV)), _const_spec((1, D_CONV)),
                _const_spec((1, D_CONV)), _const_spec((D_RWKV, D_MODEL)), _const_spec((D_CONV, D_MODEL)),
                _const_spec((D_MODEL, D_MODEL)), _const_spec((1, D_MODEL))]
    args = [x, o_r, u, szc, sgr, sgc, p["conv_w"], p["conv_b"], p["ln_g"], p["ln_b"],
            p["w_br"], p["w_bc"], p["w_out"], p["gpost"]]
    out_specs = [tile(D_MODEL)]
    out_shape = [jax.ShapeDtypeStruct((B, T, D_MODEL), F32)]
    scratch = []
    if sample:
        st = pl.BlockSpec((CONV_W - 1, tm, D_CONV), lambda b, j: (0, j, 0))
        in_specs.append(st)
        args.append(conv_state)
        out_specs.append(st)
        out_shape.append(jax.ShapeDtypeStruct(conv_state.shape, F32))
    else:
        scratch = [pltpu.VMEM((tm + CONV_PAD, D_CONV), F32)]
    return pl.pallas_call(
        _post_sample_kernel if sample else _post_prompt_kernel,
        grid=(B, T // tm),
        in_specs=in_specs,
        out_specs=out_specs,
        out_shape=out_shape,
        scratch_shapes=scratch,
        compiler_params=pltpu.CompilerParams(
            dimension_semantics=("arbitrary", "arbitrary"), vmem_limit_bytes=VMEM_LIMIT),
        name="post_sample" if sample else "post_prompt",
    )(*args)


def _layer_params(l, norm_pre_g, w_in, mu_shift, decay_w0, decay_w2, iclr_a0, iclr_a2, k_k, k_a, r_k, gn_g, gn_b,
                  conv_glu_b, conv_w, conv_b, ln_c_g, ln_c_b, w_branch_r, w_branch_c, w_out, norm_post_g):
    row = lambda t: t[l].reshape(1, -1).astype(F32)
    zeros = jnp.zeros((LORA, D_RWKV), F32)
    w2c = jnp.concatenate([jnp.concatenate([decay_w2[l], zeros], axis=1),
                           jnp.concatenate([zeros, iclr_a2[l]], axis=1)], axis=0)
    blk = jnp.arange(MXU_DIM) // HEAD_DIM
    return dict(
        gpre=row(norm_pre_g), w_in=w_in[l].astype(BF16), mu=row(mu_shift), w0=row(decay_w0), a0=row(iclr_a0),
        w2c=w2c.astype(BF16), k_k=row(k_k), k_a=row(k_a), r_k=row(r_k), glu_b=row(conv_glu_b),
        ones_bd=(blk[:, None] == blk[None, :]).astype(BF16),
        gn_g=row(gn_g), gn_b=row(gn_b), conv_w=conv_w[l].astype(F32), conv_b=row(conv_b),
        ln_g=row(ln_c_g), ln_b=row(ln_c_b), w_br=w_branch_r[l].astype(BF16), w_bc=w_branch_c[l].astype(BF16),
        w_out=w_out[l].astype(BF16), gpost=row(norm_post_g))


def _prompt_layer(x, p, tm_prep, tm_wkv, tm_post):
    T = x.shape[1]
    *pre, u, szc, sgr, sgc, h_last = _prep(x, None, p, tm_prep)
    o_r, wkv = _wkv_prompt(pre, p, tm_wkv)
    (y,) = _post(x, o_r, u, szc, sgr, sgc, None, p, tm_post)
    return y, h_last[:, SUBLANES - 1], wkv, u[:, T - (CONV_W - 1):]


def _sample_layer(x, shift, wkv, conv, p, tm_post):
    ns = x.shape[0]
    xs = x.reshape(1, ns, D_MODEL)
    *pre, u, szc, sgr, sgc, h = _prep(xs, shift.reshape(1, ns, D_MODEL), p, ns)
    o_r, wkv_new = _wkv_sample(pre, jnp.transpose(wkv, (1, 2, 3, 0)), p)
    y, conv_new = _post(xs, o_r, u, szc, sgr, sgc, jnp.transpose(conv, (1, 0, 2)), p, tm_post)
    return (y.reshape(ns, 1, D_MODEL), h.reshape(ns, D_MODEL), jnp.transpose(wkv_new, (3, 0, 1, 2)),
            jnp.transpose(conv_new, (1, 0, 2)))


def kernel(x_prompt, x_sample, state_shift, state_wkv, state_conv, norm_pre_g, w_in, mu_shift, decay_w0, decay_w2,
           iclr_a0, iclr_a2, k_k, k_a, r_k, gn_g, gn_b, conv_glu_b, conv_w, conv_b, ln_c_g, ln_c_b, w_branch_r,
           w_branch_c, w_out, norm_post_g):
    depth = w_in.shape[0]
    xp, xs = x_prompt, x_sample
    outs = [[] for _ in range(6)]
    for l in range(depth):
        p = _layer_params(l, norm_pre_g, w_in, mu_shift, decay_w0, decay_w2, iclr_a0, iclr_a2, k_k, k_a, r_k, gn_g,
                          gn_b, conv_glu_b, conv_w, conv_b, ln_c_g, ln_c_b, w_branch_r, w_branch_c, w_out,
                          norm_post_g)
        xp, a1, a2, a3 = _prompt_layer(xp, p, tm_prep=256, tm_wkv=256, tm_post=512)
        xs, b1, b2, b3 = _sample_layer(xs, state_shift[l], state_wkv[l], state_conv[l], p, tm_post=64)
        for lst, val in zip(outs, (a1, a2, a3, b1, b2, b3)):
            lst.append(val)
    return (xp, xs) + tuple(jnp.stack(o) for o in outs)
```

```python
import functools
import math

import jax
import jax.numpy as jnp
from jax import lax
from jax.experimental import pallas as pl
from jax.experimental.pallas import tpu as pltpu

F32 = jnp.float32
BF16 = jnp.bfloat16

D_MODEL = 1024
HEAD_DIM = 64
N_HEADS = D_MODEL // HEAD_DIM
D_RWKV = N_HEADS * HEAD_DIM
D_CONV = D_MODEL // 2
CONV_W = 31
LORA = 64
RMS_EPS = 1e-6
GN_EPS = 64e-5
LN_EPS = 1e-5
N_SHIFT = 4 * D_RWKV + 2 * LORA
N_IN = N_SHIFT + 2 * D_CONV + D_CONV + 2 * D_MODEL

SUBLANES = 8
PAIR = 2 * HEAD_DIM
N_PAIRS = N_HEADS // 2
MXU_DIM = 256
CHUNK = 64
TILE_CHUNKS = 4
WKV_GROUP = 2
CONV_PAD = 32
CONV_ROWS = 64
VMEM_LIMIT = 56 * 1024 * 1024

_O_R, _O_K, _O_V, _O_ZR = 0, D_RWKV, 2 * D_RWKV, 3 * D_RWKV
_O_LORA = 4 * D_RWKV
_O_U = N_SHIFT
_O_ZC = _O_U + 2 * D_CONV
_O_GR = _O_ZC + D_CONV
_O_GC = _O_GR + D_MODEL


def _dot(a, b):
    return jnp.dot(a, b, preferred_element_type=F32)


def _sigmoid(x):
    return 1.0 / (1.0 + jnp.exp(-x))


def _head_sum(x, ones_bd):
    xb = x.astype(BF16)
    return jnp.concatenate([_dot(xb[:, c * MXU_DIM:(c + 1) * MXU_DIM], ones_bd)
                            for c in range(x.shape[1] // MXU_DIM)], axis=1)


def _group_norm_gate(y, bonus, szr, gng, gnb, ones_bd):
    mean = _head_sum(y, ones_bd) * (1.0 / HEAD_DIM)
    d = y - mean
    var = _head_sum(d * d, ones_bd) * (1.0 / HEAD_DIM)
    y_gn = d * lax.rsqrt(var + GN_EPS) * gng + gnb
    return ((y_gn + bonus) * szr).astype(BF16)


def _prep_kernel(sample, *refs):
    if sample:
        (x_ref, prev_ref, gpre_ref, win_ref, mu_ref, w0_ref, a0_ref, w2c_ref, kk_ref, ka_ref, rk_ref,
         glub_ref, ones_ref,
         r_out, k_out, v_out, a_out, b_out, lw_out, szr_out, bon_out, u_out, szc_out, sgr_out, sgc_out,
         h_out) = refs
    else:
        (x_ref, gpre_ref, win_ref, mu_ref, w0_ref, a0_ref, w2c_ref, kk_ref, ka_ref, rk_ref,
         glub_ref, ones_ref,
         r_out, k_out, v_out, a_out, b_out, lw_out, szr_out, bon_out, u_out, szc_out, sgr_out, sgc_out,
         h_out, carry_ref) = refs

    x = x_ref[0]
    tm = x.shape[0]
    ms = jnp.mean(x * x, axis=-1, keepdims=True)
    h = x * lax.rsqrt(ms + RMS_EPS) * gpre_ref[...]
    hb = h.astype(BF16)

    def proj(off, width):
        return _dot(hb, win_ref[:, off:off + width])

    if sample:
        h_out[0] = h
        pb = prev_ref[0].astype(BF16)
    else:
        h_out[0] = h[tm - SUBLANES:, :]

        @pl.when(pl.program_id(1) == 0)
        def _():
            carry_ref[...] = jnp.zeros_like(carry_ref)
        row0 = lax.broadcasted_iota(jnp.int32, (tm, 1), 0) == 0

    def shifted(off, width):
        p = proj(off, width)
        if sample:
            pp = _dot(pb, win_ref[:, off:off + width])
        else:
            rolled = pltpu.roll(p, shift=1, axis=0)
            pp = jnp.where(row0, carry_ref[0:1, off:off + width], rolled)
            carry_ref[0:1, off:off + width] = p[tm - 1:tm, :]
        return p + (pp - p) * mu_ref[:, off:off + width]

    ones_bd = ones_ref[...]

    lo = shifted(_O_LORA, 2 * LORA)
    k = shifted(_O_K, D_RWKV)

    lane = lax.broadcasted_iota(jnp.int32, lo.shape, 1)
    lo = jnp.where(lane < LORA, jnp.tanh(lo), lo)
    lor = _dot(lo.astype(BF16), w2c_ref[...])

    zr = shifted(_O_ZR, D_RWKV)
    szr_out[0] = zr * _sigmoid(zr)

    kk = k * kk_ref[...]
    kk_ss = _head_sum(kk * kk, ones_bd)

    u_in = proj(_O_U, 2 * D_CONV) + glub_ref[...]
    u_out[0] = u_in[:, :D_CONV] * _sigmoid(u_in[:, D_CONV:])

    w_raw = w0_ref[...] + lor[:, :D_RWKV]
    lw_out[0] = (-math.exp(-0.5)) * _sigmoid(w_raw)
    iclr = _sigmoid(a0_ref[...] + lor[:, D_RWKV:])
    kk = kk * lax.rsqrt(kk_ss + 1e-12)
    k2 = k * (1.0 + (iclr - 1.0) * ka_ref[...])
    k_out[0] = k2
    a_out[0] = -kk
    b_out[0] = kk * iclr

    zc = proj(_O_ZC, D_CONV)
    szc_out[0] = zc * _sigmoid(zc)
    sgr_out[0] = _sigmoid(proj(_O_GR, D_MODEL))
    sgc_out[0] = _sigmoid(proj(_O_GC, D_MODEL))

    r = shifted(_O_R, D_RWKV)
    r_out[0] = r
    v = shifted(_O_V, D_RWKV)
    v_out[0] = v
    bon_out[0] = _head_sum(r * k2 * rk_ref[...], ones_bd) * v


def _const_spec(shape):
    nd = len(shape)
    return pl.BlockSpec(shape, lambda *_: (0,) * nd, pipeline_mode=pl.Buffered(1))


def _prep(x, prev, p, tm):
    B, T, _ = x.shape
    sample = prev is not None
    tile = lambda w: pl.BlockSpec((1, tm, w), lambda b, j: (b, j, 0))
    in_specs = [tile(D_MODEL)] + ([tile(D_MODEL)] if sample else []) + [
        _const_spec((1, D_MODEL)), _const_spec((D_MODEL, N_IN)), _const_spec((1, N_SHIFT)),
        _const_spec((1, D_RWKV)), _const_spec((1, D_RWKV)), _const_spec((2 * LORA, 2 * D_RWKV)),
        _const_spec((1, D_RWKV)), _const_spec((1, D_RWKV)), _const_spec((1, D_RWKV)),
        _const_spec((1, 2 * D_CONV)), _const_spec((MXU_DIM, MXU_DIM))]
    widths = [D_RWKV] * 8 + [D_CONV, D_CONV, D_MODEL, D_MODEL]
    h_spec = tile(D_MODEL) if sample else pl.BlockSpec((1, SUBLANES, D_MODEL), lambda b, j: (b, 0, 0))
    out_specs = [tile(w) for w in widths] + [h_spec]
    out_shape = [jax.ShapeDtypeStruct((B, T, w), F32) for w in widths] + [
        jax.ShapeDtypeStruct((B, T if sample else SUBLANES, D_MODEL), F32)]
    args = [x] + ([prev] if sample else []) + [
        p["gpre"], p["w_in"], p["mu"], p["w0"], p["a0"], p["w2c"], p["k_k"], p["k_a"], p["r_k"],
        p["glu_b"], p["ones_bd"]]
    return pl.pallas_call(
        functools.partial(_prep_kernel, sample),
        grid=(B, T // tm),
        in_specs=in_specs,
        out_specs=out_specs,
        out_shape=out_shape,
        scratch_shapes=[] if sample else [pltpu.VMEM((SUBLANES, N_SHIFT), F32)],
        compiler_params=pltpu.CompilerParams(
            dimension_semantics=("arbitrary", "arbitrary"), vmem_limit_bytes=VMEM_LIMIT),
        name="prep_sample" if sample else "prep_prompt",
    )(*args)


def _block_diag(x, same_head):
    return jnp.where(same_head, jnp.concatenate([x, x], axis=0), jnp.zeros((), x.dtype))


def _drain(steps):
    for _ in steps:
        pass


def _wkv_masks():
    row = lax.broadcasted_iota(jnp.int32, (CHUNK, PAIR), 0)
    col = lax.broadcasted_iota(jnp.int32, (CHUNK, PAIR), 1) % HEAD_DIM
    brow = lax.broadcasted_iota(jnp.int32, (PAIR, PAIR), 0)
    bcol = lax.broadcasted_iota(jnp.int32, (PAIR, PAIR), 1)
    trow = lax.broadcasted_iota(jnp.int32, (CHUNK, CHUNK), 0)
    tcol = lax.broadcasted_iota(jnp.int32, (CHUNK, CHUNK), 1)
    return dict(strict=row > col, incl=row >= col, eye_pair=(row == col).astype(F32),
                same_head=(brow // HEAD_DIM) == (bcol // HEAD_DIM), eye_full=brow == bcol,
                first_head=lax.broadcasted_iota(jnp.int32, (CHUNK, PAIR), 1) < HEAD_DIM,
                tri=(trow >= tcol).astype(BF16))


def _block_diag_t(x, same_head):
    return jnp.where(same_head, jnp.concatenate([x, x], axis=0).T, 0.0).astype(BF16)


def _wkv_chunk_operators(r_ref, k_ref, v_ref, a_ref, b_ref, lw_ref, first_chunk, m, tick):
    strict, incl, eye_pair, same_head = m["strict"], m["incl"], m["eye_pair"], m["same_head"]
    bd = lambda x: _block_diag(x.astype(BF16), same_head)
    bd_t = lambda x: _block_diag_t(x, same_head)

    def chunk_inputs(row0):
        rows = pl.ds(row0, CHUNK)
        lw = lw_ref[0, rows, :]
        l1 = lw.astype(BF16)
        rem = lw - l1.astype(F32)
        l2 = rem.astype(BF16)
        l3 = (rem - l2.astype(F32)).astype(BF16)
        cl = _dot(m["tri"], l1) + _dot(m["tri"], l2) + _dot(m["tri"], l3)
        e_neg = jnp.exp(-cl)
        w_end = jnp.exp(cl[CHUNK - 1:CHUNK, :])
        b_t = b_ref[0, rows, :] * e_neg
        k_t = k_ref[0, rows, :] * e_neg
        return dict(r=r_ref[0, rows, :] * jnp.exp(cl), a=(a_ref[0, rows, :] * jnp.exp(cl - lw)).astype(BF16),
                    b=b_t, k=k_t, be=(b_t * w_end).astype(BF16), ke=(k_t * w_end).astype(BF16),
                    v=v_ref[0, rows, :], w_end=w_end)

    ch = []
    for c in range(WKV_GROUP):
        ch.append(chunk_inputs((first_chunk + c) * CHUNK))
        tick()
    pairs = range(N_PAIRS)
    sls = [slice(p * PAIR, (p + 1) * PAIR) for p in pairs]
    units = [(c, p) for c in range(WKV_GROUP) for p in pairs]
    us = range(len(units))
    pick = lambda name: [ch[c][name][:, sls[p]] for c, p in units]
    at, rt, bt, kt, be, ke, vp = (pick(n) for n in ("a", "r", "b", "k", "be", "ke", "v"))
    g = [_dot(jnp.concatenate([at[u], rt[u].astype(BF16)], axis=0),
              jnp.concatenate([bd_t(bt[u]), bd_t(kt[u])], axis=1)) for u in us]
    tick()
    l_ab = [jnp.where(strict, g[u][:CHUNK, :PAIR], 0.0) for u in us]
    m_rb = [jnp.where(incl, g[u][CHUNK:, :PAIR], 0.0).astype(BF16) for u in us]
    km = [jnp.concatenate([jnp.where(strict, g[u][:CHUNK, PAIR:], 0.0),
                           jnp.where(incl, g[u][CHUNK:, PAIR:], 0.0)], axis=0).astype(BF16) for u in us]
    uy = [_dot(km[u], bd(vp[u])) for u in us]
    tick()
    uv0 = [uy[u][:CHUNK].astype(BF16) for u in us]
    t_acc = [eye_pair + l_ab[u] for u in us]
    lp = [_dot(l_ab[u].astype(BF16), bd(l_ab[u])).astype(BF16) for u in us]
    tick()
    for _ in range(4):
        prod = [_dot(lp[u], jnp.concatenate([bd(lp[u]), bd(t_acc[u])], axis=1)) for u in us]
        lp = [prod[u][:, :PAIR].astype(BF16) for u in us]
        t_acc = [t_acc[u] + prod[u][:, PAIR:] for u in us]
        tick()
    t_acc = [t_acc[u] + _dot(lp[u], bd(t_acc[u])) for u in us]
    tick()
    au = [_dot(t_acc[u].astype(BF16), jnp.concatenate([bd(at[u]), bd(uv0[u])], axis=1)) for u in us]
    tick()
    a_hat = [au[u][:, :PAIR] for u in us]
    u_v = [au[u][:, PAIR:] for u in us]
    ry = [_dot(m_rb[u], jnp.concatenate([bd(a_hat[u]), bd(u_v[u])], axis=1)) for u in us]
    tick()
    r_hat = [(rt[u] + ry[u][:, :PAIR]).astype(BF16) for u in us]
    y_v = [ry[u][:, PAIR:] + uy[u][CHUNK:] for u in us]
    phi = [_dot(a_hat[u].T.astype(BF16), be[u]) for u in us]
    tick()
    f = [_dot(jnp.concatenate([u_v[u], vp[u]], axis=0).T.astype(BF16),
              jnp.concatenate([be[u], ke[u]], axis=0)) for u in us]
    tick()
    phi = [(jnp.where(same_head, phi[u], 0.0)
            + jnp.where(m["eye_full"], ch[units[u][0]]["w_end"][:, sls[units[u][1]]], 0.0)).astype(BF16)
           for u in us]
    z = [jnp.where(m["first_head"], f[u][:HEAD_DIM, :], f[u][HEAD_DIM:, :]) for u in us]
    return r_hat, y_v, phi, z


def _wkv_recurrence_steps(slot, rh_ref, yv_ref, phi_ref, z_ref, s_ref, y_ref, m):
    pairs = range(N_PAIRS)
    state = [s_ref[p] for p in pairs]
    for c in range(TILE_CHUNKS):
        us = [c * N_PAIRS + p for p in pairs]
        ys = [_dot(rh_ref[slot, us[p]], _block_diag_t(state[p], m["same_head"])) + yv_ref[slot, us[p]]
              for p in pairs]
        state = [_dot(state[p].astype(BF16), phi_ref[slot, us[p]]) + z_ref[slot, us[p]] for p in pairs]
        y_ref[c * CHUNK:(c + 1) * CHUNK, :] = jnp.concatenate(ys, axis=1)
        yield
    for p in pairs:
        s_ref[p] = state[p]


def _wkv_prompt_kernel(nt, r_ref, k_ref, v_ref, a_ref, b_ref, lw_ref, szr_ref, bon_ref, gng_ref, gnb_ref, ones_ref,
                       o_ref, s_out, s_ref, y_ref, rh_ref, yv_ref, phi_ref, z_ref):
    s = pl.program_id(0)
    prev = jnp.maximum(s - 1, 0)
    cur_slot = s % 2

    @pl.when(s == 0)
    def _():
        for ref in (rh_ref, yv_ref, phi_ref, z_ref):
            ref[1] = jnp.zeros_like(ref[1])

    @pl.when(prev % nt == 0)
    def _():
        s_ref[...] = jnp.zeros_like(s_ref)

    m = _wkv_masks()

    def previous_tile():
        yield from _wkv_recurrence_steps(1 - cur_slot, rh_ref, yv_ref, phi_ref, z_ref, s_ref, y_ref, m)
        o_ref[0] = _group_norm_gate(y_ref[...], bon_ref[0], szr_ref[0], gng_ref[...], gnb_ref[...], ones_ref[...])

    prev_steps = previous_tile()
    for first_chunk in range(0, TILE_CHUNKS, WKV_GROUP):
        ops = _wkv_chunk_operators(r_ref, k_ref, v_ref, a_ref, b_ref, lw_ref, first_chunk, m,
                                   tick=lambda: next(prev_steps, None))
        for i in range(WKV_GROUP * N_PAIRS):
            u = first_chunk * N_PAIRS + i
            for ref, val in zip((rh_ref, yv_ref, phi_ref, z_ref), ops):
                ref[cur_slot, u] = val[i]
    _drain(prev_steps)

    @pl.when((s > 0) & (prev % nt == nt - 1))
    def _():
        for hd in range(N_HEADS):
            lo = (hd % 2) * HEAD_DIM
            s_out[0, hd] = s_ref[hd // 2][:, lo:lo + HEAD_DIM]


def _wkv_prompt(pre, p, tm):
    r, k, v, a, b, lw, szr, bon = pre
    B, T, _ = r.shape
    assert T % tm == 0 and tm == TILE_CHUNKS * CHUNK and TILE_CHUNKS % WKV_GROUP == 0, (T, tm)
    nt = T // tm
    n_tiles = B * nt
    n_units = TILE_CHUNKS * N_PAIRS
    cur = pl.BlockSpec((1, tm, D_RWKV),
                       lambda s: (jnp.minimum(s, n_tiles - 1) // nt, jnp.minimum(s, n_tiles - 1) % nt, 0))
    prv = pl.BlockSpec((1, tm, D_RWKV), lambda s: (jnp.maximum(s - 1, 0) // nt, jnp.maximum(s - 1, 0) % nt, 0))
    state = pl.BlockSpec((1, N_HEADS, HEAD_DIM, HEAD_DIM), lambda s: (jnp.maximum(s - 1, 0) // nt, 0, 0, 0))
    return pl.pallas_call(
        functools.partial(_wkv_prompt_kernel, nt),
        grid=(n_tiles + 1,),
        in_specs=[cur] * 6 + [prv] * 2 + [_const_spec((1, D_RWKV)), _const_spec((1, D_RWKV)),
                                          _const_spec((MXU_DIM, MXU_DIM))],
        out_specs=[prv, state],
        out_shape=[jax.ShapeDtypeStruct((B, T, D_RWKV), BF16),
                   jax.ShapeDtypeStruct((B, N_HEADS, HEAD_DIM, HEAD_DIM), F32)],
        scratch_shapes=[pltpu.VMEM((N_PAIRS, HEAD_DIM, PAIR), F32), pltpu.VMEM((tm, D_RWKV), F32),
                        pltpu.VMEM((2, n_units, CHUNK, PAIR), BF16), pltpu.VMEM((2, n_units, CHUNK, PAIR), F32),
                        pltpu.VMEM((2, n_units, PAIR, PAIR), BF16), pltpu.VMEM((2, n_units, CHUNK, PAIR), F32)],
        compiler_params=pltpu.CompilerParams(
            dimension_semantics=("arbitrary",), vmem_limit_bytes=VMEM_LIMIT),
        name="wkv_prompt",
    )(r, k, v, a, b, lw, szr, bon, p["gn_g"], p["gn_b"], p["ones_bd"])


def _wkv_step_kernel(r_ref, k_ref, v_ref, a_ref, b_ref, lw_ref, szr_ref, bon_ref, gng_ref, gnb_ref,
                     ones_ref, s_in, o_ref, s_out, vec_t, y_t):
    hd = pl.program_id(0)
    ns = s_in.shape[-1]

    @pl.when(hd == 0)
    def _():
        for n, ref in enumerate((a_ref, b_ref, k_ref, r_ref)):
            vec_t[n] = ref[0].T
        vec_t[4] = jnp.exp(lw_ref[0]).T
        vec_t[5] = v_ref[0].T

    base = pl.multiple_of(hd * HEAD_DIM, HEAD_DIM)
    hrows = pl.ds(base, HEAD_DIM)
    a_h, b_h, k_h, r_h, w_h = (vec_t[n, hrows, :] for n in range(5))
    sub = lax.broadcasted_iota(jnp.int32, (SUBLANES, ns), 0)

    def value_group(g, carry):
        off = pl.multiple_of(g * SUBLANES, SUBLANES)
        v_g = vec_t[5, pl.ds(base + off, SUBLANES), :]
        y_g = jnp.zeros((SUBLANES, ns), F32)
        for ii in range(SUBLANES):
            s0 = s_in[0, off + ii]
            sa = jnp.sum(s0 * a_h, axis=0, keepdims=True)
            s1 = s0 * w_h + sa * b_h + v_g[ii:ii + 1, :] * k_h
            s_out[0, off + ii] = s1
            y_g = jnp.where(sub == ii, jnp.sum(s1 * r_h, axis=0, keepdims=True), y_g)
        y_t[pl.ds(base + off, SUBLANES), :] = y_g
        return carry

    lax.fori_loop(0, HEAD_DIM // SUBLANES, value_group, 0)

    @pl.when(hd == pl.num_programs(0) - 1)
    def _():
        o_ref[0] = _group_norm_gate(y_t[...].T, bon_ref[0], szr_ref[0], gng_ref[...], gnb_ref[...], ones_ref[...])


def _wkv_sample(pre, state_t, p):
    ns = state_t.shape[-1]
    vec = pl.BlockSpec((1, ns, D_RWKV), lambda h: (0, 0, 0))
    st = pl.BlockSpec((1, HEAD_DIM, HEAD_DIM, ns), lambda h: (h, 0, 0, 0))
    return pl.pallas_call(
        _wkv_step_kernel,
        grid=(N_HEADS,),
        in_specs=[vec] * 8 + [_const_spec((1, D_RWKV)), _const_spec((1, D_RWKV)),
                              _const_spec((MXU_DIM, MXU_DIM)), st],
        out_specs=[vec, st],
        out_shape=[jax.ShapeDtypeStruct((1, ns, D_RWKV), BF16),
                   jax.ShapeDtypeStruct(state_t.shape, F32)],
        scratch_shapes=[pltpu.VMEM((6, D_RWKV, ns), F32), pltpu.VMEM((D_RWKV, ns), F32)],
        compiler_params=pltpu.CompilerParams(
            dimension_semantics=("arbitrary",), vmem_limit_bytes=VMEM_LIMIT),
        name="wkv_sample",
    )(*pre, p["gn_g"], p["gn_b"], p["ones_bd"], state_t)


def _conv_tile(u, cw_ref, cb_ref, ubuf, conv_ref):
    tm = u.shape[0]
    ubuf[CONV_PAD:CONV_PAD + tm, :] = u
    base = CONV_PAD - (CONV_W - 1)
    span = CONV_ROWS + CONV_PAD

    def row_block(i, carry):
        t0 = pl.multiple_of(i * CONV_ROWS, CONV_ROWS)
        win = ubuf[pl.ds(t0, span), :]
        acc = jnp.broadcast_to(cb_ref[...], (CONV_ROWS, D_CONV))
        for s in range(SUBLANES):
            shifted = win if s == 0 else pltpu.roll(win, shift=span - s, axis=0)
            for kk in range(CONV_W):
                if (base + kk) % SUBLANES == s:
                    q = base + kk - s
                    acc = acc + cw_ref[kk:kk + 1, :] * shifted[q:q + CONV_ROWS, :]
        conv_ref[pl.ds(t0, CONV_ROWS), :] = acc
        return carry

    lax.fori_loop(0, tm // CONV_ROWS, row_block, 0)
    ubuf[0:CONV_PAD, :] = ubuf[tm:tm + CONV_PAD, :]


def _post_tail(conv, x, o_r, szc, sgr, sgc, lng_ref, lnb_ref, wbr_ref, wbc_ref, wout_ref, gpost_ref):
    cm = jnp.mean(conv, axis=-1, keepdims=True)
    d = conv - cm
    cv = jnp.mean(d * d, axis=-1, keepdims=True)
    cn = d * lax.rsqrt(cv + LN_EPS) * lng_ref[...] + lnb_ref[...]
    o_c = (cn * _sigmoid(cn)) * szc
    branch_c = _dot(o_c.astype(BF16), wbc_ref[...])
    branch_r = _dot(o_r, wbr_ref[...])
    merged = sgr * branch_r + sgc * branch_c
    out = _dot(merged.astype(BF16), wout_ref[...])
    ms = jnp.mean(out * out, axis=-1, keepdims=True)
    return x + out * lax.rsqrt(ms + RMS_EPS) * gpost_ref[...]


def _post_prompt_kernel(x_ref, or_ref, u_ref, szc_ref, sgr_ref, sgc_ref, cw_ref, cb_ref, lng_ref, lnb_ref,
                        wbr_ref, wbc_ref, wout_ref, gpost_ref, y_ref, ubuf, conv_ref):
    @pl.when(pl.program_id(1) == 0)
    def _():
        ubuf[0:CONV_PAD, :] = jnp.zeros((CONV_PAD, D_CONV), F32)

    _conv_tile(u_ref[0], cw_ref, cb_ref, ubuf, conv_ref)
    y_ref[0] = _post_tail(conv_ref[...], x_ref[0], or_ref[0], szc_ref[0], sgr_ref[0], sgc_ref[0],
                          lng_ref, lnb_ref, wbr_ref, wbc_ref, wout_ref, gpost_ref)


def _post_sample_kernel(x_ref, or_ref, u_ref, szc_ref, sgr_ref, sgc_ref, cw_ref, cb_ref, lng_ref, lnb_ref,
                        wbr_ref, wbc_ref, wout_ref, gpost_ref, st_ref, y_ref, st_out):
    u = u_ref[0]
    acc = jnp.broadcast_to(cb_ref[...], u.shape)
    for kk in range(CONV_W - 1):
        acc = acc + cw_ref[kk:kk + 1, :] * st_ref[kk]
    acc = acc + cw_ref[CONV_W - 1:CONV_W, :] * u
    st_out[0:CONV_W - 2] = st_ref[1:CONV_W - 1]
    st_out[CONV_W - 2] = u
    y_ref[0] = _post_tail(acc, x_ref[0], or_ref[0], szc_ref[0], sgr_ref[0], sgc_ref[0],
                          lng_ref, lnb_ref, wbr_ref, wbc_ref, wout_ref, gpost_ref)


def _post(x, o_r, u, szc, sgr, sgc, conv_state, p, tm):
    B, T, _ = x.shape
    sample = conv_state is not None
    tile = lambda w: pl.BlockSpec((1, tm, w), lambda b, j: (b, j, 0))
    in_specs = [tile(D_MODEL), tile(D_RWKV), tile(D_CONV), tile(D_CONV), tile(D_MODEL), tile(D_MODEL),
                _const_spec((CONV_W, D_CONV)), _const_spec((1, D_CONV)), _const_spec((1, D_CONV)),
                _const_spec((1, D_CONV)), _const_spec((D_RWKV, D_MODEL)), _const_spec((D_CONV, D_MODEL)),
                _const_spec((D_MODEL, D_MODEL)), _const_spec((1, D_MODEL))]
    args = [x, o_r, u, szc, sgr, sgc, p["conv_w"], p["conv_b"], p["ln_g"], p["ln_b"],
            p["w_br"], p["w_bc"], p["w_out"], p["gpost"]]
    out_specs = [tile(D_MODEL)]
    out_shape = [jax.ShapeDtypeStruct((B, T, D_MODEL), F32)]
    scratch = []
    if sample:
        st = pl.BlockSpec((CONV_W - 1, tm, D_CONV), lambda b, j: (0, j, 0))
        in_specs.append(st)
        args.append(conv_state)
        out_specs.append(st)
        out_shape.append(jax.ShapeDtypeStruct(conv_state.shape, F32))
    else:
        assert tm % CONV_ROWS == 0, tm
        scratch = [pltpu.VMEM((tm + CONV_PAD, D_CONV), F32), pltpu.VMEM((tm, D_CONV), F32)]
    return pl.pallas_call(
        _post_sample_kernel if sample else _post_prompt_kernel,
        grid=(B, T // tm),
        in_specs=in_specs,
        out_specs=out_specs,
        out_shape=out_shape,
        scratch_shapes=scratch,
        compiler_params=pltpu.CompilerParams(
            dimension_semantics=("arbitrary", "arbitrary"), vmem_limit_bytes=VMEM_LIMIT),
        name="post_sample" if sample else "post_prompt",
    )(*args)


def _layer_params(l, norm_pre_g, w_in, mu_shift, decay_w0, decay_w2, iclr_a0, iclr_a2, k_k, k_a, r_k, gn_g, gn_b,
                  conv_glu_b, conv_w, conv_b, ln_c_g, ln_c_b, w_branch_r, w_branch_c, w_out, norm_post_g):
    row = lambda t: t[l].reshape(1, -1).astype(F32)
    zeros = jnp.zeros((LORA, D_RWKV), F32)
    w2c = jnp.concatenate([jnp.concatenate([decay_w2[l], zeros], axis=1),
                           jnp.concatenate([zeros, iclr_a2[l]], axis=1)], axis=0)
    blk = jnp.arange(MXU_DIM) // HEAD_DIM
    return dict(
        gpre=row(norm_pre_g), w_in=w_in[l].astype(BF16), mu=row(mu_shift), w0=row(decay_w0), a0=row(iclr_a0),
        w2c=w2c.astype(BF16), k_k=row(k_k), k_a=row(k_a), r_k=row(r_k), glu_b=row(conv_glu_b),
        ones_bd=(blk[:, None] == blk[None, :]).astype(BF16),
        gn_g=row(gn_g), gn_b=row(gn_b), conv_w=conv_w[l].astype(F32), conv_b=row(conv_b),
        ln_g=row(ln_c_g), ln_b=row(ln_c_b), w_br=w_branch_r[l].astype(BF16), w_bc=w_branch_c[l].astype(BF16),
        w_out=w_out[l].astype(BF16), gpost=row(norm_post_g))


def _prompt_layer(x, p, tm_prep, tm_wkv, tm_post):
    T = x.shape[1]
    *pre, u, szc, sgr, sgc, h_last = _prep(x, None, p, tm_prep)
    o_r, wkv = _wkv_prompt(pre, p, tm_wkv)
    (y,) = _post(x, o_r, u, szc, sgr, sgc, None, p, tm_post)
    return y, h_last[:, SUBLANES - 1], wkv, u[:, T - (CONV_W - 1):]


def _sample_layer(x, shift, wkv, conv, p, tm_post):
    ns = x.shape[0]
    xs = x.reshape(1, ns, D_MODEL)
    *pre, u, szc, sgr, sgc, h = _prep(xs, shift.reshape(1, ns, D_MODEL), p, ns)
    o_r, wkv_new = _wkv_sample(pre, jnp.transpose(wkv, (1, 2, 3, 0)), p)
    y, conv_new = _post(xs, o_r, u, szc, sgr, sgc, jnp.transpose(conv, (1, 0, 2)), p, tm_post)
    return (y.reshape(ns, 1, D_MODEL), h.reshape(ns, D_MODEL), jnp.transpose(wkv_new, (3, 0, 1, 2)),
            jnp.transpose(conv_new, (1, 0, 2)))


def kernel(x_prompt, x_sample, state_shift, state_wkv, state_conv, norm_pre_g, w_in, mu_shift, decay_w0, decay_w2,
           iclr_a0, iclr_a2, k_k, k_a, r_k, gn_g, gn_b, conv_glu_b, conv_w, conv_b, ln_c_g, ln_c_b, w_branch_r,
           w_branch_c, w_out, norm_post_g):
    depth = w_in.shape[0]
    xp, xs = x_prompt, x_sample
    outs = [[] for _ in range(6)]
    for l in range(depth):
        p = _layer_params(l, norm_pre_g, w_in, mu_shift, decay_w0, decay_w2, iclr_a0, iclr_a2, k_k, k_a, r_k, gn_g,
                          gn_b, conv_glu_b, conv_w, conv_b, ln_c_g, ln_c_b, w_branch_r, w_branch_c, w_out,
                          norm_post_g)
        xp, a1, a2, a3 = _prompt_layer(xp, p, tm_prep=256, tm_wkv=256, tm_post=512)
        xs, b1, b2, b3 = _sample_layer(xs, state_shift[l], state_wkv[l], state_conv[l], p, tm_post=64)
        for lst, val in zip(outs, (a1, a2, a3, b1, b2, b3)):
            lst.append(val)
    return (xp, xs) + tuple(jnp.stack(o) for o in outs)
```

```python
import functools
import math

import jax
import jax.numpy as jnp
from jax import lax
from jax.experimental import pallas as pl
from jax.experimental.pallas import tpu as pltpu

F32 = jnp.float32
BF16 = jnp.bfloat16

D_MODEL = 1024
HEAD_DIM = 64
N_HEADS = D_MODEL // HEAD_DIM
D_RWKV = N_HEADS * HEAD_DIM
D_CONV = D_MODEL // 2
CONV_W = 31
LORA = 64
RMS_EPS = 1e-6
GN_EPS = 64e-5
LN_EPS = 1e-5
N_SHIFT = 4 * D_RWKV + 2 * LORA
N_IN = N_SHIFT + 2 * D_CONV + D_CONV + 2 * D_MODEL

SUBLANES = 8
PAIR = 2 * HEAD_DIM
N_PAIRS = N_HEADS // 2
MXU_DIM = 256
CHUNK = 64
TILE_CHUNKS = 4
WKV_GROUP = 2
CONV_PAD = 32
VMEM_LIMIT = 56 * 1024 * 1024

_O_R, _O_K, _O_V, _O_ZR = 0, D_RWKV, 2 * D_RWKV, 3 * D_RWKV
_O_LORA = 4 * D_RWKV
_O_U = N_SHIFT
_O_ZC = _O_U + 2 * D_CONV
_O_GR = _O_ZC + D_CONV
_O_GC = _O_GR + D_MODEL


def _dot(a, b):
    return jnp.dot(a, b, preferred_element_type=F32)


def _sigmoid(x):
    return 1.0 / (1.0 + jnp.exp(-x))


def _head_sum(x, ones_bd):
    xb = x.astype(BF16)
    return jnp.concatenate([_dot(xb[:, c * MXU_DIM:(c + 1) * MXU_DIM], ones_bd)
                            for c in range(x.shape[1] // MXU_DIM)], axis=1)


def _group_norm_gate(y, bonus, szr, gng, gnb, ones_bd):
    mean = _head_sum(y, ones_bd) * (1.0 / HEAD_DIM)
    d = y - mean
    var = _head_sum(d * d, ones_bd) * (1.0 / HEAD_DIM)
    y_gn = d * lax.rsqrt(var + GN_EPS) * gng + gnb
    return ((y_gn + bonus) * szr).astype(BF16)


def _prep_kernel(sample, *refs):
    if sample:
        (x_ref, prev_ref, gpre_ref, win_ref, mu_ref, w0_ref, a0_ref, w2c_ref, kk_ref, ka_ref, rk_ref,
         glub_ref, ones_ref,
         r_out, k_out, v_out, a_out, b_out, lw_out, szr_out, bon_out, u_out, szc_out, sgr_out, sgc_out,
         h_out) = refs
    else:
        (x_ref, gpre_ref, win_ref, mu_ref, w0_ref, a0_ref, w2c_ref, kk_ref, ka_ref, rk_ref,
         glub_ref, ones_ref,
         r_out, k_out, v_out, a_out, b_out, lw_out, szr_out, bon_out, u_out, szc_out, sgr_out, sgc_out,
         h_out, carry_ref) = refs

    x = x_ref[0]
    tm = x.shape[0]
    ms = jnp.mean(x * x, axis=-1, keepdims=True)
    h = x * lax.rsqrt(ms + RMS_EPS) * gpre_ref[...]
    hb = h.astype(BF16)

    def proj(off, width):
        return _dot(hb, win_ref[:, off:off + width])

    if sample:
        h_out[0] = h
        pb = prev_ref[0].astype(BF16)
    else:
        h_out[0] = h[tm - SUBLANES:, :]

        @pl.when(pl.program_id(1) == 0)
        def _():
            carry_ref[...] = jnp.zeros_like(carry_ref)
        row0 = lax.broadcasted_iota(jnp.int32, (tm, 1), 0) == 0

    def shifted(off, width):
        p = proj(off, width)
        if sample:
            pp = _dot(pb, win_ref[:, off:off + width])
        else:
            rolled = pltpu.roll(p, shift=1, axis=0)
            pp = jnp.where(row0, carry_ref[0:1, off:off + width], rolled)
            carry_ref[0:1, off:off + width] = p[tm - 1:tm, :]
        return p + (pp - p) * mu_ref[:, off:off + width]

    ones_bd = ones_ref[...]

    lo = shifted(_O_LORA, 2 * LORA)
    k = shifted(_O_K, D_RWKV)

    lane = lax.broadcasted_iota(jnp.int32, lo.shape, 1)
    lo = jnp.where(lane < LORA, jnp.tanh(lo), lo)
    lor = _dot(lo.astype(BF16), w2c_ref[...])

    zr = shifted(_O_ZR, D_RWKV)
    szr_out[0] = zr * _sigmoid(zr)

    kk = k * kk_ref[...]
    kk_ss = _head_sum(kk * kk, ones_bd)

    u_in = proj(_O_U, 2 * D_CONV) + glub_ref[...]
    u_out[0] = u_in[:, :D_CONV] * _sigmoid(u_in[:, D_CONV:])

    w_raw = w0_ref[...] + lor[:, :D_RWKV]
    lw_out[0] = (-math.exp(-0.5)) * _sigmoid(w_raw)
    iclr = _sigmoid(a0_ref[...] + lor[:, D_RWKV:])
    kk = kk * lax.rsqrt(kk_ss + 1e-12)
    k2 = k * (1.0 + (iclr - 1.0) * ka_ref[...])
    k_out[0] = k2
    a_out[0] = -kk
    b_out[0] = kk * iclr

    zc = proj(_O_ZC, D_CONV)
    szc_out[0] = zc * _sigmoid(zc)
    sgr_out[0] = _sigmoid(proj(_O_GR, D_MODEL))
    sgc_out[0] = _sigmoid(proj(_O_GC, D_MODEL))

    r = shifted(_O_R, D_RWKV)
    r_out[0] = r
    v = shifted(_O_V, D_RWKV)
    v_out[0] = v
    bon_out[0] = _head_sum(r * k2 * rk_ref[...], ones_bd) * v


def _const_spec(shape):
    nd = len(shape)
    return pl.BlockSpec(shape, lambda *_: (0,) * nd, pipeline_mode=pl.Buffered(1))


def _prep(x, prev, p, tm):
    B, T, _ = x.shape
    sample = prev is not None
    tile = lambda w: pl.BlockSpec((1, tm, w), lambda b, j: (b, j, 0))
    in_specs = [tile(D_MODEL)] + ([tile(D_MODEL)] if sample else []) + [
        _const_spec((1, D_MODEL)), _const_spec((D_MODEL, N_IN)), _const_spec((1, N_SHIFT)),
        _const_spec((1, D_RWKV)), _const_spec((1, D_RWKV)), _const_spec((2 * LORA, 2 * D_RWKV)),
        _const_spec((1, D_RWKV)), _const_spec((1, D_RWKV)), _const_spec((1, D_RWKV)),
        _const_spec((1, 2 * D_CONV)), _const_spec((MXU_DIM, MXU_DIM))]
    widths = [D_RWKV] * 8 + [D_CONV, D_CONV, D_MODEL, D_MODEL]
    h_spec = tile(D_MODEL) if sample else pl.BlockSpec((1, SUBLANES, D_MODEL), lambda b, j: (b, 0, 0))
    out_specs = [tile(w) for w in widths] + [h_spec]
    out_shape = [jax.ShapeDtypeStruct((B, T, w), F32) for w in widths] + [
        jax.ShapeDtypeStruct((B, T if sample else SUBLANES, D_MODEL), F32)]
    args = [x] + ([prev] if sample else []) + [
        p["gpre"], p["w_in"], p["mu"], p["w0"], p["a0"], p["w2c"], p["k_k"], p["k_a"], p["r_k"],
        p["glu_b"], p["ones_bd"]]
    return pl.pallas_call(
        functools.partial(_prep_kernel, sample),
        grid=(B, T // tm),
        in_specs=in_specs,
        out_specs=out_specs,
        out_shape=out_shape,
        scratch_shapes=[] if sample else [pltpu.VMEM((SUBLANES, N_SHIFT), F32)],
        compiler_params=pltpu.CompilerParams(
            dimension_semantics=("arbitrary", "arbitrary"), vmem_limit_bytes=VMEM_LIMIT),
        name="prep_sample" if sample else "prep_prompt",
    )(*args)


def _block_diag(x, same_head):
    return jnp.where(same_head, jnp.concatenate([x, x], axis=0), jnp.zeros((), x.dtype))


def _drain(steps):
    for _ in steps:
        pass


def _wkv_masks():
    row = lax.broadcasted_iota(jnp.int32, (CHUNK, PAIR), 0)
    col = lax.broadcasted_iota(jnp.int32, (CHUNK, PAIR), 1) % HEAD_DIM
    brow = lax.broadcasted_iota(jnp.int32, (PAIR, PAIR), 0)
    bcol = lax.broadcasted_iota(jnp.int32, (PAIR, PAIR), 1)
    trow = lax.broadcasted_iota(jnp.int32, (CHUNK, CHUNK), 0)
    tcol = lax.broadcasted_iota(jnp.int32, (CHUNK, CHUNK), 1)
    return dict(strict=row > col, incl=row >= col, eye_pair=(row == col).astype(F32),
                same_head=(brow // HEAD_DIM) == (bcol // HEAD_DIM), eye_full=brow == bcol,
                first_head=lax.broadcasted_iota(jnp.int32, (CHUNK, PAIR), 1) < HEAD_DIM,
                tri=(trow >= tcol).astype(BF16))


def _block_diag_t(x, same_head):
    return jnp.where(same_head, jnp.concatenate([x, x], axis=0).T, 0.0).astype(BF16)


def _wkv_chunk_operators(r_ref, k_ref, v_ref, a_ref, b_ref, lw_ref, first_chunk, m, tick):
    strict, incl, eye_pair, same_head = m["strict"], m["incl"], m["eye_pair"], m["same_head"]
    bd = lambda x: _block_diag(x.astype(BF16), same_head)
    bd_t = lambda x: _block_diag_t(x, same_head)

    def chunk_inputs(row0):
        rows = pl.ds(row0, CHUNK)
        lw = lw_ref[0, rows, :]
        l1 = lw.astype(BF16)
        rem = lw - l1.astype(F32)
        l2 = rem.astype(BF16)
        l3 = (rem - l2.astype(F32)).astype(BF16)
        cl = _dot(m["tri"], l1) + _dot(m["tri"], l2) + _dot(m["tri"], l3)
        e_neg = jnp.exp(-cl)
        w_end = jnp.exp(cl[CHUNK - 1:CHUNK, :])
        b_t = b_ref[0, rows, :] * e_neg
        k_t = k_ref[0, rows, :] * e_neg
        return dict(r=r_ref[0, rows, :] * jnp.exp(cl), a=(a_ref[0, rows, :] * jnp.exp(cl - lw)).astype(BF16),
                    b=b_t, k=k_t, be=(b_t * w_end).astype(BF16), ke=(k_t * w_end).astype(BF16),
                    v=v_ref[0, rows, :], w_end=w_end)

    ch = []
    for c in range(WKV_GROUP):
        ch.append(chunk_inputs((first_chunk + c) * CHUNK))
        tick()
    pairs = range(N_PAIRS)
    sls = [slice(p * PAIR, (p + 1) * PAIR) for p in pairs]
    units = [(c, p) for c in range(WKV_GROUP) for p in pairs]
    us = range(len(units))
    pick = lambda name: [ch[c][name][:, sls[p]] for c, p in units]
    at, rt, bt, kt, be, ke, vp = (pick(n) for n in ("a", "r", "b", "k", "be", "ke", "v"))
    g = [_dot(jnp.concatenate([at[u], rt[u].astype(BF16)], axis=0),
              jnp.concatenate([bd_t(bt[u]), bd_t(kt[u])], axis=1)) for u in us]
    tick()
    l_ab = [jnp.where(strict, g[u][:CHUNK, :PAIR], 0.0) for u in us]
    m_rb = [jnp.where(incl, g[u][CHUNK:, :PAIR], 0.0).astype(BF16) for u in us]
    km = [jnp.concatenate([jnp.where(strict, g[u][:CHUNK, PAIR:], 0.0),
                           jnp.where(incl, g[u][CHUNK:, PAIR:], 0.0)], axis=0).astype(BF16) for u in us]
    uy = [_dot(km[u], bd(vp[u])) for u in us]
    tick()
    uv0 = [uy[u][:CHUNK].astype(BF16) for u in us]
    t_acc = [eye_pair + l_ab[u] for u in us]
    lp = [_dot(l_ab[u].astype(BF16), bd(l_ab[u])).astype(BF16) for u in us]
    tick()
    for _ in range(4):
        prod = [_dot(lp[u], jnp.concatenate([bd(lp[u]), bd(t_acc[u])], axis=1)) for u in us]
        lp = [prod[u][:, :PAIR].astype(BF16) for u in us]
        t_acc = [t_acc[u] + prod[u][:, PAIR:] for u in us]
        tick()
    t_acc = [t_acc[u] + _dot(lp[u], bd(t_acc[u])) for u in us]
    tick()
    au = [_dot(t_acc[u].astype(BF16), jnp.concatenate([bd(at[u]), bd(uv0[u])], axis=1)) for u in us]
    tick()
    a_hat = [au[u][:, :PAIR] for u in us]
    u_v = [au[u][:, PAIR:] for u in us]
    ry = [_dot(m_rb[u], jnp.concatenate([bd(a_hat[u]), bd(u_v[u])], axis=1)) for u in us]
    tick()
    r_hat = [(rt[u] + ry[u][:, :PAIR]).astype(BF16) for u in us]
    y_v = [ry[u][:, PAIR:] + uy[u][CHUNK:] for u in us]
    phi = [_dot(a_hat[u].T.astype(BF16), be[u]) for u in us]
    tick()
    f = [_dot(jnp.concatenate([u_v[u], vp[u]], axis=0).T.astype(BF16),
              jnp.concatenate([be[u], ke[u]], axis=0)) for u in us]
    tick()
    phi = [(jnp.where(same_head, phi[u], 0.0)
            + jnp.where(m["eye_full"], ch[units[u][0]]["w_end"][:, sls[units[u][1]]], 0.0)).astype(BF16)
           for u in us]
    z = [jnp.where(m["first_head"], f[u][:HEAD_DIM, :], f[u][HEAD_DIM:, :]) for u in us]
    return r_hat, y_v, phi, z


def _wkv_recurrence_steps(slot, rh_ref, yv_ref, phi_ref, z_ref, s_ref, y_ref, m):
    pairs = range(N_PAIRS)
    state = [s_ref[p] for p in pairs]
    for c in range(TILE_CHUNKS):
        us = [c * N_PAIRS + p for p in pairs]
        ys = [_dot(rh_ref[slot, us[p]], _block_diag_t(state[p], m["same_head"])) + yv_ref[slot, us[p]]
              for p in pairs]
        state = [_dot(state[p].astype(BF16), phi_ref[slot, us[p]]) + z_ref[slot, us[p]] for p in pairs]
        y_ref[c * CHUNK:(c + 1) * CHUNK, :] = jnp.concatenate(ys, axis=1)
        yield
    for p in pairs:
        s_ref[p] = state[p]


def _wkv_prompt_kernel(nt, r_ref, k_ref, v_ref, a_ref, b_ref, lw_ref, szr_ref, bon_ref, gng_ref, gnb_ref, ones_ref,
                       o_ref, s_out, s_ref, y_ref, rh_ref, yv_ref, phi_ref, z_ref):
    s = pl.program_id(0)
    prev = jnp.maximum(s - 1, 0)
    cur_slot = s % 2

    @pl.when(s == 0)
    def _():
        for ref in (rh_ref, yv_ref, phi_ref, z_ref):
            ref[1] = jnp.zeros_like(ref[1])

    @pl.when(prev % nt == 0)
    def _():
        s_ref[...] = jnp.zeros_like(s_ref)

    m = _wkv_masks()

    def previous_tile():
        yield from _wkv_recurrence_steps(1 - cur_slot, rh_ref, yv_ref, phi_ref, z_ref, s_ref, y_ref, m)
        o_ref[0] = _group_norm_gate(y_ref[...], bon_ref[0], szr_ref[0], gng_ref[...], gnb_ref[...], ones_ref[...])

    prev_steps = previous_tile()
    for first_chunk in range(0, TILE_CHUNKS, WKV_GROUP):
        ops = _wkv_chunk_operators(r_ref, k_ref, v_ref, a_ref, b_ref, lw_ref, first_chunk, m,
                                   tick=lambda: next(prev_steps, None))
        for i in range(WKV_GROUP * N_PAIRS):
            u = first_chunk * N_PAIRS + i
            for ref, val in zip((rh_ref, yv_ref, phi_ref, z_ref), ops):
                ref[cur_slot, u] = val[i]
    _drain(prev_steps)

    @pl.when((s > 0) & (prev % nt == nt - 1))
    def _():
        for hd in range(N_HEADS):
            lo = (hd % 2) * HEAD_DIM
            s_out[0, hd] = s_ref[hd // 2][:, lo:lo + HEAD_DIM]


def _wkv_prompt(pre, p, tm):
    r, k, v, a, b, lw, szr, bon = pre
    B, T, _ = r.shape
    assert T % tm == 0 and tm == TILE_CHUNKS * CHUNK and TILE_CHUNKS % WKV_GROUP == 0, (T, tm)
    nt = T // tm
    n_tiles = B * nt
    n_units = TILE_CHUNKS * N_PAIRS
    cur = pl.BlockSpec((1, tm, D_RWKV),
                       lambda s: (jnp.minimum(s, n_tiles - 1) // nt, jnp.minimum(s, n_tiles - 1) % nt, 0))
    prv = pl.BlockSpec((1, tm, D_RWKV), lambda s: (jnp.maximum(s - 1, 0) // nt, jnp.maximum(s - 1, 0) % nt, 0))
    state = pl.BlockSpec((1, N_HEADS, HEAD_DIM, HEAD_DIM), lambda s: (jnp.maximum(s - 1, 0) // nt, 0, 0, 0))
    return pl.pallas_call(
        functools.partial(_wkv_prompt_kernel, nt),
        grid=(n_tiles + 1,),
        in_specs=[cur] * 6 + [prv] * 2 + [_const_spec((1, D_RWKV)), _const_spec((1, D_RWKV)),
                                          _const_spec((MXU_DIM, MXU_DIM))],
        out_specs=[prv, state],
        out_shape=[jax.ShapeDtypeStruct((B, T, D_RWKV), BF16),
                   jax.ShapeDtypeStruct((B, N_HEADS, HEAD_DIM, HEAD_DIM), F32)],
        scratch_shapes=[pltpu.VMEM((N_PAIRS, HEAD_DIM, PAIR), F32), pltpu.VMEM((tm, D_RWKV), F32),
                        pltpu.VMEM((2, n_units, CHUNK, PAIR), BF16), pltpu.VMEM((2, n_units, CHUNK, PAIR), F32),
                        pltpu.VMEM((2, n_units, PAIR, PAIR), BF16), pltpu.VMEM((2, n_units, CHUNK, PAIR), F32)],
        compiler_params=pltpu.CompilerParams(
            dimension_semantics=("arbitrary",), vmem_limit_bytes=VMEM_LIMIT),
        name="wkv_prompt",
    )(r, k, v, a, b, lw, szr, bon, p["gn_g"], p["gn_b"], p["ones_bd"])


def _wkv_step_kernel(r_ref, k_ref, v_ref, a_ref, b_ref, lw_ref, szr_ref, bon_ref, gng_ref, gnb_ref,
                     ones_ref, s_in, o_ref, s_out, vec_t, y_t):
    hd = pl.program_id(0)
    ns = s_in.shape[-1]

    @pl.when(hd == 0)
    def _():
        for n, ref in enumerate((a_ref, b_ref, k_ref, r_ref)):
            vec_t[n] = ref[0].T
        vec_t[4] = jnp.exp(lw_ref[0]).T
        vec_t[5] = v_ref[0].T

    base = pl.multiple_of(hd * HEAD_DIM, HEAD_DIM)
    hrows = pl.ds(base, HEAD_DIM)
    a_h, b_h, k_h, r_h, w_h = (vec_t[n, hrows, :] for n in range(5))
    sub = lax.broadcasted_iota(jnp.int32, (SUBLANES, ns), 0)

    def value_group(g, carry):
        off = pl.multiple_of(g * SUBLANES, SUBLANES)
        v_g = vec_t[5, pl.ds(base + off, SUBLANES), :]
        y_g = jnp.zeros((SUBLANES, ns), F32)
        for ii in range(SUBLANES):
            s0 = s_in[0, off + ii]
            sa = jnp.sum(s0 * a_h, axis=0, keepdims=True)
            s1 = s0 * w_h + sa * b_h + v_g[ii:ii + 1, :] * k_h
            s_out[0, off + ii] = s1
            y_g = jnp.where(sub == ii, jnp.sum(s1 * r_h, axis=0, keepdims=True), y_g)
        y_t[pl.ds(base + off, SUBLANES), :] = y_g
        return carry

    lax.fori_loop(0, HEAD_DIM // SUBLANES, value_group, 0)

    @pl.when(hd == pl.num_programs(0) - 1)
    def _():
        o_ref[0] = _group_norm_gate(y_t[...].T, bon_ref[0], szr_ref[0], gng_ref[...], gnb_ref[...], ones_ref[...])


def _wkv_sample(pre, state_t, p):
    ns = state_t.shape[-1]
    vec = pl.BlockSpec((1, ns, D_RWKV), lambda h: (0, 0, 0))
    st = pl.BlockSpec((1, HEAD_DIM, HEAD_DIM, ns), lambda h: (h, 0, 0, 0))
    return pl.pallas_call(
        _wkv_step_kernel,
        grid=(N_HEADS,),
        in_specs=[vec] * 8 + [_const_spec((1, D_RWKV)), _const_spec((1, D_RWKV)),
                              _const_spec((MXU_DIM, MXU_DIM)), st],
        out_specs=[vec, st],
        out_shape=[jax.ShapeDtypeStruct((1, ns, D_RWKV), BF16),
                   jax.ShapeDtypeStruct(state_t.shape, F32)],
        scratch_shapes=[pltpu.VMEM((6, D_RWKV, ns), F32), pltpu.VMEM((D_RWKV, ns), F32)],
        compiler_params=pltpu.CompilerParams(
            dimension_semantics=("arbitrary",), vmem_limit_bytes=VMEM_LIMIT),
        name="wkv_sample",
    )(*pre, p["gn_g"], p["gn_b"], p["ones_bd"], state_t)


def _conv_tile(u, cw_ref, cb_ref, ubuf):
    tm = u.shape[0]
    ubuf[CONV_PAD:CONV_PAD + tm, :] = u
    win = ubuf[...]
    rows = tm + CONV_PAD
    base = CONV_PAD - (CONV_W - 1)
    acc = jnp.broadcast_to(cb_ref[...], u.shape)
    for s in range(SUBLANES):
        shifted = win if s == 0 else pltpu.roll(win, shift=rows - s, axis=0)
        for kk in range(CONV_W):
            if (base + kk) % SUBLANES == s:
                q = base + kk - s
                acc = acc + cw_ref[kk:kk + 1, :] * shifted[q:q + tm, :]
    ubuf[0:CONV_PAD, :] = win[tm:tm + CONV_PAD, :]
    return acc


def _post_tail(conv, x, o_r, szc, sgr, sgc, lng_ref, lnb_ref, wbr_ref, wbc_ref, wout_ref, gpost_ref):
    branch_r = _dot(o_r, wbr_ref[...])
    cm = jnp.mean(conv, axis=-1, keepdims=True)
    d = conv - cm
    cv = jnp.mean(d * d, axis=-1, keepdims=True)
    cn = d * lax.rsqrt(cv + LN_EPS) * lng_ref[...] + lnb_ref[...]
    o_c = (cn * _sigmoid(cn)) * szc
    branch_c = _dot(o_c.astype(BF16), wbc_ref[...])
    merged = sgr * branch_r + sgc * branch_c
    out = _dot(merged.astype(BF16), wout_ref[...])
    ms = jnp.mean(out * out, axis=-1, keepdims=True)
    return x + out * lax.rsqrt(ms + RMS_EPS) * gpost_ref[...]


def _post_prompt_kernel(x_ref, or_ref, u_ref, szc_ref, sgr_ref, sgc_ref, cw_ref, cb_ref, lng_ref, lnb_ref,
                        wbr_ref, wbc_ref, wout_ref, gpost_ref, y_ref, ubuf):
    @pl.when(pl.program_id(1) == 0)
    def _():
        ubuf[0:CONV_PAD, :] = jnp.zeros((CONV_PAD, D_CONV), F32)

    conv = _conv_tile(u_ref[0], cw_ref, cb_ref, ubuf)
    y_ref[0] = _post_tail(conv, x_ref[0], or_ref[0], szc_ref[0], sgr_ref[0], sgc_ref[0],
                          lng_ref, lnb_ref, wbr_ref, wbc_ref, wout_ref, gpost_ref)


def _post_sample_kernel(x_ref, or_ref, u_ref, szc_ref, sgr_ref, sgc_ref, cw_ref, cb_ref, lng_ref, lnb_ref,
                        wbr_ref, wbc_ref, wout_ref, gpost_ref, st_ref, y_ref, st_out):
    u = u_ref[0]
    acc = jnp.broadcast_to(cb_ref[...], u.shape)
    for kk in range(CONV_W - 1):
        acc = acc + cw_ref[kk:kk + 1, :] * st_ref[kk]
    acc = acc + cw_ref[CONV_W - 1:CONV_W, :] * u
    st_out[0:CONV_W - 2] = st_ref[1:CONV_W - 1]
    st_out[CONV_W - 2] = u
    y_ref[0] = _post_tail(acc, x_ref[0], or_ref[0], szc_ref[0], sgr_ref[0], sgc_ref[0],
                          lng_ref, lnb_ref, wbr_ref, wbc_ref, wout_ref, gpost_ref)


def _post(x, o_r, u, szc, sgr, sgc, conv_state, p, tm):
    B, T, _ = x.shape
    sample = conv_state is not None
    tile = lambda w: pl.BlockSpec((1, tm, w), lambda b, j: (b, j, 0))
    in_specs = [tile(D_MODEL), tile(D_RWKV), tile(D_CONV), tile(D_CONV), tile(D_MODEL), tile(D_MODEL),
                _const_spec((CONV_W, D_CONV)), _const_spec((1, D_CONV)), _const_spec((1, D_CONV)),
                _const_spec((1, D_CONV)), _const_spec((D_RWKV, D_MODEL)), _const_spec((D_CONV, D_MODEL)),
                _const_spec((D_MODEL, D_MODEL)), _const_spec((1, D_MODEL))]
    args = [x, o_r, u, szc, sgr, sgc, p["conv_w"], p["conv_b"], p["ln_g"], p["ln_b"],
            p["w_br"], p["w_bc"], p["w_out"], p["gpost"]]
    out_specs = [tile(D_MODEL)]
    out_shape = [jax.ShapeDtypeStruct((B, T, D_MODEL), F32)]
    scratch = []
    if sample:
        st = pl.BlockSpec((CONV_W - 1, tm, D_CONV), lambda b, j: (0, j, 0))
        in_specs.append(st)
        args.append(conv_state)
        out_specs.append(st)
        out_shape.append(jax.ShapeDtypeStruct(conv_state.shape, F32))
    else:
        scratch = [pltpu.VMEM((tm + CONV_PAD, D_CONV), F32)]
    return pl.pallas_call(
        _post_sample_kernel if sample else _post_prompt_kernel,
        grid=(B, T // tm),
        in_specs=in_specs,
        out_specs=out_specs,
        out_shape=out_shape,
        scratch_shapes=scratch,
        compiler_params=pltpu.CompilerParams(
            dimension_semantics=("arbitrary", "arbitrary"), vmem_limit_bytes=VMEM_LIMIT),
        name="post_sample" if sample else "post_prompt",
    )(*args)


def _layer_params(l, norm_pre_g, w_in, mu_shift, decay_w0, decay_w2, iclr_a0, iclr_a2, k_k, k_a, r_k, gn_g, gn_b,
                  conv_glu_b, conv_w, conv_b, ln_c_g, ln_c_b, w_branch_r, w_branch_c, w_out, norm_post_g):
    row = lambda t: t[l].reshape(1, -1).astype(F32)
    zeros = jnp.zeros((LORA, D_RWKV), F32)
    w2c = jnp.concatenate([jnp.concatenate([decay_w2[l], zeros], axis=1),
                           jnp.concatenate([zeros, iclr_a2[l]], axis=1)], axis=0)
    blk = jnp.arange(MXU_DIM) // HEAD_DIM
    return dict(
        gpre=row(norm_pre_g), w_in=w_in[l].astype(BF16), mu=row(mu_shift), w0=row(decay_w0), a0=row(iclr_a0),
        w2c=w2c.astype(BF16), k_k=row(k_k), k_a=row(k_a), r_k=row(r_k), glu_b=row(conv_glu_b),
        ones_bd=(blk[:, None] == blk[None, :]).astype(BF16),
        gn_g=row(gn_g), gn_b=row(gn_b), conv_w=conv_w[l].astype(F32), conv_b=row(conv_b),
        ln_g=row(ln_c_g), ln_b=row(ln_c_b), w_br=w_branch_r[l].astype(BF16), w_bc=w_branch_c[l].astype(BF16),
        w_out=w_out[l].astype(BF16), gpost=row(norm_post_g))


def _prompt_layer(x, p, tm_prep, tm_wkv, tm_post):
    T = x.shape[1]
    *pre, u, szc, sgr, sgc, h_last = _prep(x, None, p, tm_prep)
    o_r, wkv = _wkv_prompt(pre, p, tm_wkv)
    (y,) = _post(x, o_r, u, szc, sgr, sgc, None, p, tm_post)
    return y, h_last[:, SUBLANES - 1], wkv, u[:, T - (CONV_W - 1):]


def _sample_layer(x, shift, wkv, conv, p, tm_post):
    ns = x.shape[0]
    xs = x.reshape(1, ns, D_MODEL)
    *pre, u, szc, sgr, sgc, h = _prep(xs, shift.reshape(1, ns, D_MODEL), p, ns)
    o_r, wkv_new = _wkv_sample(pre, jnp.transpose(wkv, (1, 2, 3, 0)), p)
    y, conv_new = _post(xs, o_r, u, szc, sgr, sgc, jnp.transpose(conv, (1, 0, 2)), p, tm_post)
    return (y.reshape(ns, 1, D_MODEL), h.reshape(ns, D_MODEL), jnp.transpose(wkv_new, (3, 0, 1, 2)),
            jnp.transpose(conv_new, (1, 0, 2)))


def kernel(x_prompt, x_sample, state_shift, state_wkv, state_conv, norm_pre_g, w_in, mu_shift, decay_w0, decay_w2,
           iclr_a0, iclr_a2, k_k, k_a, r_k, gn_g, gn_b, conv_glu_b, conv_w, conv_b, ln_c_g, ln_c_b, w_branch_r,
           w_branch_c, w_out, norm_post_g):
    depth = w_in.shape[0]
    xp, xs = x_prompt, x_sample
    outs = [[] for _ in range(6)]
    for l in range(depth):
        p = _layer_params(l, norm_pre_g, w_in, mu_shift, decay_w0, decay_w2, iclr_a0, iclr_a2, k_k, k_a, r_k, gn_g,
                          gn_b, conv_glu_b, conv_w, conv_b, ln_c_g, ln_c_b, w_branch_r, w_branch_c, w_out,
                          norm_post_g)
        xp, a1, a2, a3 = _prompt_layer(xp, p, tm_prep=256, tm_wkv=256, tm_post=512)
        xs, b1, b2, b3 = _sample_layer(xs, state_shift[l], state_wkv[l], state_conv[l], p, tm_post=64)
        for lst, val in zip(outs, (a1, a2, a3, b1, b2, b3)):
            lst.append(val)
    return (xp, xs) + tuple(jnp.stack(o) for o in outs)
```

```python
import functools
import math

import jax
import jax.numpy as jnp
from jax import lax
from jax.experimental import pallas as pl
from jax.experimental.pallas import tpu as pltpu

F32 = jnp.float32
BF16 = jnp.bfloat16

D_MODEL = 1024
HEAD_DIM = 64
N_HEADS = D_MODEL // HEAD_DIM
D_RWKV = N_HEADS * HEAD_DIM
D_CONV = D_MODEL // 2
CONV_W = 31
LORA = 64
RMS_EPS = 1e-6
GN_EPS = 64e-5
LN_EPS = 1e-5
N_SHIFT = 4 * D_RWKV + 2 * LORA
N_IN = N_SHIFT + 2 * D_CONV + D_CONV + 2 * D_MODEL

SUBLANES = 8
PAIR = 2 * HEAD_DIM
N_PAIRS = N_HEADS // 2
MXU_DIM = 256
CHUNK = 64
TILE_CHUNKS = 4
WKV_GROUP = 2
STATE_SLOTS = 3
CONV_PAD = 32
VMEM_LIMIT = 56 * 1024 * 1024

_O_R, _O_K, _O_V, _O_ZR = 0, D_RWKV, 2 * D_RWKV, 3 * D_RWKV
_O_LORA = 4 * D_RWKV
_O_U = N_SHIFT
_O_ZC = _O_U + 2 * D_CONV
_O_GR = _O_ZC + D_CONV
_O_GC = _O_GR + D_MODEL


def _dot(a, b):
    return jnp.dot(a, b, preferred_element_type=F32)


def _sigmoid(x):
    return 1.0 / (1.0 + jnp.exp(-x))


def _head_sum(x, ones_bd):
    xb = x.astype(BF16)
    return jnp.concatenate([_dot(xb[:, c * MXU_DIM:(c + 1) * MXU_DIM], ones_bd)
                            for c in range(x.shape[1] // MXU_DIM)], axis=1)


def _group_norm_gate(y, bonus, szr, gng, gnb, ones_bd):
    mean = _head_sum(y, ones_bd) * (1.0 / HEAD_DIM)
    d = y - mean
    var = _head_sum(d * d, ones_bd) * (1.0 / HEAD_DIM)
    y_gn = d * lax.rsqrt(var + GN_EPS) * gng + gnb
    return ((y_gn + bonus) * szr).astype(BF16)


def _prep_kernel(sample, *refs):
    if sample:
        (x_ref, prev_ref, gpre_ref, win_ref, mu_ref, w0_ref, a0_ref, w2c_ref, kk_ref, ka_ref, rk_ref,
         glub_ref, ones_ref,
         r_out, k_out, v_out, a_out, b_out, lw_out, szr_out, bon_out, u_out, szc_out, sgr_out, sgc_out,
         h_out) = refs
    else:
        (x_ref, gpre_ref, win_ref, mu_ref, w0_ref, a0_ref, w2c_ref, kk_ref, ka_ref, rk_ref,
         glub_ref, ones_ref,
         r_out, k_out, v_out, a_out, b_out, lw_out, szr_out, bon_out, u_out, szc_out, sgr_out, sgc_out,
         h_out, carry_ref) = refs

    x = x_ref[0]
    tm = x.shape[0]
    ms = jnp.mean(x * x, axis=-1, keepdims=True)
    h = x * lax.rsqrt(ms + RMS_EPS) * gpre_ref[...]
    hb = h.astype(BF16)

    def proj(off, width):
        return _dot(hb, win_ref[:, off:off + width])

    if sample:
        h_out[0] = h
        pb = prev_ref[0].astype(BF16)
    else:
        h_out[0] = h[tm - SUBLANES:, :]

        @pl.when(pl.program_id(1) == 0)
        def _():
            carry_ref[...] = jnp.zeros_like(carry_ref)
        row0 = lax.broadcasted_iota(jnp.int32, (tm, 1), 0) == 0

    def shifted(off, width):
        p = proj(off, width)
        if sample:
            pp = _dot(pb, win_ref[:, off:off + width])
        else:
            rolled = pltpu.roll(p, shift=1, axis=0)
            pp = jnp.where(row0, carry_ref[0:1, off:off + width], rolled)
            carry_ref[0:1, off:off + width] = p[tm - 1:tm, :]
        return p + (pp - p) * mu_ref[:, off:off + width]

    ones_bd = ones_ref[...]

    lo = shifted(_O_LORA, 2 * LORA)
    k = shifted(_O_K, D_RWKV)

    lane = lax.broadcasted_iota(jnp.int32, lo.shape, 1)
    lo = jnp.where(lane < LORA, jnp.tanh(lo), lo)
    lor = _dot(lo.astype(BF16), w2c_ref[...])

    zr = shifted(_O_ZR, D_RWKV)
    szr_out[0] = zr * _sigmoid(zr)

    kk = k * kk_ref[...]
    kk_ss = _head_sum(kk * kk, ones_bd)

    u_in = proj(_O_U, 2 * D_CONV) + glub_ref[...]
    u_out[0] = u_in[:, :D_CONV] * _sigmoid(u_in[:, D_CONV:])

    w_raw = w0_ref[...] + lor[:, :D_RWKV]
    lw_out[0] = (-math.exp(-0.5)) * _sigmoid(w_raw)
    iclr = _sigmoid(a0_ref[...] + lor[:, D_RWKV:])
    kk = kk * lax.rsqrt(kk_ss + 1e-12)
    k2 = k * (1.0 + (iclr - 1.0) * ka_ref[...])
    k_out[0] = k2
    a_out[0] = -kk
    b_out[0] = kk * iclr

    zc = proj(_O_ZC, D_CONV)
    szc_out[0] = zc * _sigmoid(zc)
    sgr_out[0] = _sigmoid(proj(_O_GR, D_MODEL))
    sgc_out[0] = _sigmoid(proj(_O_GC, D_MODEL))

    r = shifted(_O_R, D_RWKV)
    r_out[0] = r
    v = shifted(_O_V, D_RWKV)
    v_out[0] = v
    bon_out[0] = _head_sum(r * k2 * rk_ref[...], ones_bd) * v


def _const_spec(shape):
    nd = len(shape)
    return pl.BlockSpec(shape, lambda *_: (0,) * nd, pipeline_mode=pl.Buffered(1))


def _prep(x, prev, p, tm):
    B, T, _ = x.shape
    sample = prev is not None
    tile = lambda w: pl.BlockSpec((1, tm, w), lambda b, j: (b, j, 0))
    in_specs = [tile(D_MODEL)] + ([tile(D_MODEL)] if sample else []) + [
        _const_spec((1, D_MODEL)), _const_spec((D_MODEL, N_IN)), _const_spec((1, N_SHIFT)),
        _const_spec((1, D_RWKV)), _const_spec((1, D_RWKV)), _const_spec((2 * LORA, 2 * D_RWKV)),
        _const_spec((1, D_RWKV)), _const_spec((1, D_RWKV)), _const_spec((1, D_RWKV)),
        _const_spec((1, 2 * D_CONV)), _const_spec((MXU_DIM, MXU_DIM))]
    widths = [D_RWKV] * 8 + [D_CONV, D_CONV, D_MODEL, D_MODEL]
    h_spec = tile(D_MODEL) if sample else pl.BlockSpec((1, SUBLANES, D_MODEL), lambda b, j: (b, 0, 0))
    out_specs = [tile(w) for w in widths] + [h_spec]
    out_shape = [jax.ShapeDtypeStruct((B, T, w), F32) for w in widths] + [
        jax.ShapeDtypeStruct((B, T if sample else SUBLANES, D_MODEL), F32)]
    args = [x] + ([prev] if sample else []) + [
        p["gpre"], p["w_in"], p["mu"], p["w0"], p["a0"], p["w2c"], p["k_k"], p["k_a"], p["r_k"],
        p["glu_b"], p["ones_bd"]]
    return pl.pallas_call(
        functools.partial(_prep_kernel, sample),
        grid=(B, T // tm),
        in_specs=in_specs,
        out_specs=out_specs,
        out_shape=out_shape,
        scratch_shapes=[] if sample else [pltpu.VMEM((SUBLANES, N_SHIFT), F32)],
        compiler_params=pltpu.CompilerParams(
            dimension_semantics=("arbitrary", "arbitrary"), vmem_limit_bytes=VMEM_LIMIT),
        name="prep_sample" if sample else "prep_prompt",
    )(*args)


def _block_diag(x, same_head):
    return jnp.where(same_head, jnp.concatenate([x, x], axis=0), jnp.zeros((), x.dtype))


def _drain(steps):
    for _ in steps:
        pass


def _wkv_masks():
    row = lax.broadcasted_iota(jnp.int32, (CHUNK, PAIR), 0)
    col = lax.broadcasted_iota(jnp.int32, (CHUNK, PAIR), 1) % HEAD_DIM
    brow = lax.broadcasted_iota(jnp.int32, (PAIR, PAIR), 0)
    bcol = lax.broadcasted_iota(jnp.int32, (PAIR, PAIR), 1)
    trow = lax.broadcasted_iota(jnp.int32, (CHUNK, CHUNK), 0)
    tcol = lax.broadcasted_iota(jnp.int32, (CHUNK, CHUNK), 1)
    return dict(strict=row > col, incl=row >= col, eye_pair=(row == col).astype(F32),
                same_head=(brow // HEAD_DIM) == (bcol // HEAD_DIM), eye_full=brow == bcol,
                first_head=lax.broadcasted_iota(jnp.int32, (CHUNK, PAIR), 1) < HEAD_DIM,
                tri=(trow >= tcol).astype(BF16))


def _block_diag_t(x, same_head):
    return jnp.where(same_head, jnp.concatenate([x, x], axis=0).T, 0.0).astype(BF16)


def _wkv_chunk_operators(r_ref, k_ref, v_ref, a_ref, b_ref, lw_ref, first_chunk, m, tick):
    strict, incl, eye_pair, same_head = m["strict"], m["incl"], m["eye_pair"], m["same_head"]
    bd = lambda x: _block_diag(x.astype(BF16), same_head)
    bd_t = lambda x: _block_diag_t(x, same_head)

    def chunk_inputs(row0):
        rows = pl.ds(row0, CHUNK)
        lw = lw_ref[0, rows, :]
        l1 = lw.astype(BF16)
        rem = lw - l1.astype(F32)
        l2 = rem.astype(BF16)
        l3 = (rem - l2.astype(F32)).astype(BF16)
        cl = _dot(m["tri"], l1) + _dot(m["tri"], l2) + _dot(m["tri"], l3)
        e_neg = jnp.exp(-cl)
        w_end = jnp.exp(cl[CHUNK - 1:CHUNK, :])
        b_t = b_ref[0, rows, :] * e_neg
        k_t = k_ref[0, rows, :] * e_neg
        return dict(r=r_ref[0, rows, :] * jnp.exp(cl), a=(a_ref[0, rows, :] * jnp.exp(cl - lw)).astype(BF16),
                    b=b_t, k=k_t, be=(b_t * w_end).astype(BF16), ke=(k_t * w_end).astype(BF16),
                    v=v_ref[0, rows, :], w_end=w_end)

    ch = []
    for c in range(WKV_GROUP):
        ch.append(chunk_inputs((first_chunk + c) * CHUNK))
        tick()
    pairs = range(N_PAIRS)
    sls = [slice(p * PAIR, (p + 1) * PAIR) for p in pairs]
    units = [(c, p) for c in range(WKV_GROUP) for p in pairs]
    us = range(len(units))
    pick = lambda name: [ch[c][name][:, sls[p]] for c, p in units]
    at, rt, bt, kt, be, ke, vp = (pick(n) for n in ("a", "r", "b", "k", "be", "ke", "v"))
    g = [_dot(jnp.concatenate([at[u], rt[u].astype(BF16)], axis=0),
              jnp.concatenate([bd_t(bt[u]), bd_t(kt[u])], axis=1)) for u in us]
    tick()
    l_ab = [jnp.where(strict, g[u][:CHUNK, :PAIR], 0.0) for u in us]
    m_rb = [jnp.where(incl, g[u][CHUNK:, :PAIR], 0.0).astype(BF16) for u in us]
    km = [jnp.concatenate([jnp.where(strict, g[u][:CHUNK, PAIR:], 0.0),
                           jnp.where(incl, g[u][CHUNK:, PAIR:], 0.0)], axis=0).astype(BF16) for u in us]
    uy = [_dot(km[u], bd(vp[u])) for u in us]
    tick()
    uv0 = [uy[u][:CHUNK].astype(BF16) for u in us]
    t_acc = [eye_pair + l_ab[u] for u in us]
    lp = [_dot(l_ab[u].astype(BF16), bd(l_ab[u])).astype(BF16) for u in us]
    tick()
    for _ in range(4):
        prod = [_dot(lp[u], jnp.concatenate([bd(lp[u]), bd(t_acc[u])], axis=1)) for u in us]
        lp = [prod[u][:, :PAIR].astype(BF16) for u in us]
        t_acc = [t_acc[u] + prod[u][:, PAIR:] for u in us]
        tick()
    t_acc = [t_acc[u] + _dot(lp[u], bd(t_acc[u])) for u in us]
    tick()
    au = [_dot(t_acc[u].astype(BF16), jnp.concatenate([bd(at[u]), bd(uv0[u])], axis=1)) for u in us]
    tick()
    a_hat = [au[u][:, :PAIR] for u in us]
    u_v = [au[u][:, PAIR:] for u in us]
    ry = [_dot(m_rb[u], jnp.concatenate([bd(a_hat[u]), bd(u_v[u])], axis=1)) for u in us]
    tick()
    r_hat = [(rt[u] + ry[u][:, :PAIR]).astype(BF16) for u in us]
    y_v = [ry[u][:, PAIR:] + uy[u][CHUNK:] for u in us]
    phi = [_dot(a_hat[u].T.astype(BF16), be[u]) for u in us]
    tick()
    f = [_dot(jnp.concatenate([u_v[u], vp[u]], axis=0).T.astype(BF16),
              jnp.concatenate([be[u], ke[u]], axis=0)) for u in us]
    tick()
    phi = [(jnp.where(same_head, phi[u], 0.0)
            + jnp.where(m["eye_full"], ch[units[u][0]]["w_end"][:, sls[units[u][1]]], 0.0)).astype(BF16)
           for u in us]
    z = [jnp.where(m["first_head"], f[u][:HEAD_DIM, :], f[u][HEAD_DIM:, :]) for u in us]
    return r_hat, y_v, phi, z


def _wkv_recurrence_steps(slot, rh_ref, yv_ref, phi_ref, z_ref, s_ref, y_ref, m):
    pairs = range(N_PAIRS)
    state = [s_ref[p] for p in pairs]
    for c in range(TILE_CHUNKS):
        us = [c * N_PAIRS + p for p in pairs]
        ys = [_dot(rh_ref[slot, us[p]], _block_diag_t(state[p], m["same_head"])) + yv_ref[slot, us[p]]
              for p in pairs]
        state = [_dot(state[p].astype(BF16), phi_ref[slot, us[p]]) + z_ref[slot, us[p]] for p in pairs]
        y_ref[c * CHUNK:(c + 1) * CHUNK, :] = jnp.concatenate(ys, axis=1)
        yield
    for p in pairs:
        s_ref[p] = state[p]


def _wkv_prompt_kernel(nt, r_ref, k_ref, v_ref, a_ref, b_ref, lw_ref, szr_ref, bon_ref, gng_ref, gnb_ref, ones_ref,
                       o_ref, s_out, s_ref, y_ref, rh_ref, yv_ref, phi_ref, z_ref):
    s = pl.program_id(0)
    prev = jnp.maximum(s - 1, 0)
    cur_slot = s % 2

    @pl.when(s == 0)
    def _():
        for ref in (rh_ref, yv_ref, phi_ref, z_ref):
            ref[1] = jnp.zeros_like(ref[1])

    @pl.when(prev % nt == 0)
    def _():
        s_ref[...] = jnp.zeros_like(s_ref)

    m = _wkv_masks()

    def previous_tile():
        yield from _wkv_recurrence_steps(1 - cur_slot, rh_ref, yv_ref, phi_ref, z_ref, s_ref, y_ref, m)
        o_ref[0] = _group_norm_gate(y_ref[...], bon_ref[0], szr_ref[0], gng_ref[...], gnb_ref[...], ones_ref[...])

    prev_steps = previous_tile()
    for first_chunk in range(0, TILE_CHUNKS, WKV_GROUP):
        ops = _wkv_chunk_operators(r_ref, k_ref, v_ref, a_ref, b_ref, lw_ref, first_chunk, m,
                                   tick=lambda: next(prev_steps, None))
        for i in range(WKV_GROUP * N_PAIRS):
            u = first_chunk * N_PAIRS + i
            for ref, val in zip((rh_ref, yv_ref, phi_ref, z_ref), ops):
                ref[cur_slot, u] = val[i]
    _drain(prev_steps)

    @pl.when((s > 0) & (prev % nt == nt - 1))
    def _():
        for hd in range(N_HEADS):
            lo = (hd % 2) * HEAD_DIM
            s_out[0, hd] = s_ref[hd // 2][:, lo:lo + HEAD_DIM]


def _wkv_prompt(pre, p, tm):
    r, k, v, a, b, lw, szr, bon = pre
    B, T, _ = r.shape
    assert T % tm == 0 and tm == TILE_CHUNKS * CHUNK and TILE_CHUNKS % WKV_GROUP == 0, (T, tm)
    nt = T // tm
    n_tiles = B * nt
    n_units = TILE_CHUNKS * N_PAIRS
    cur = pl.BlockSpec((1, tm, D_RWKV),
                       lambda s: (jnp.minimum(s, n_tiles - 1) // nt, jnp.minimum(s, n_tiles - 1) % nt, 0))
    prv = pl.BlockSpec((1, tm, D_RWKV), lambda s: (jnp.maximum(s - 1, 0) // nt, jnp.maximum(s - 1, 0) % nt, 0))
    state = pl.BlockSpec((1, N_HEADS, HEAD_DIM, HEAD_DIM), lambda s: (jnp.maximum(s - 1, 0) // nt, 0, 0, 0))
    return pl.pallas_call(
        functools.partial(_wkv_prompt_kernel, nt),
        grid=(n_tiles + 1,),
        in_specs=[cur] * 6 + [prv] * 2 + [_const_spec((1, D_RWKV)), _const_spec((1, D_RWKV)),
                                          _const_spec((MXU_DIM, MXU_DIM))],
        out_specs=[prv, state],
        out_shape=[jax.ShapeDtypeStruct((B, T, D_RWKV), BF16),
                   jax.ShapeDtypeStruct((B, N_HEADS, HEAD_DIM, HEAD_DIM), F32)],
        scratch_shapes=[pltpu.VMEM((N_PAIRS, HEAD_DIM, PAIR), F32), pltpu.VMEM((tm, D_RWKV), F32),
                        pltpu.VMEM((2, n_units, CHUNK, PAIR), BF16), pltpu.VMEM((2, n_units, CHUNK, PAIR), F32),
                        pltpu.VMEM((2, n_units, PAIR, PAIR), BF16), pltpu.VMEM((2, n_units, CHUNK, PAIR), F32)],
        compiler_params=pltpu.CompilerParams(
            dimension_semantics=("arbitrary",), vmem_limit_bytes=VMEM_LIMIT),
        name="wkv_prompt",
    )(r, k, v, a, b, lw, szr, bon, p["gn_g"], p["gn_b"], p["ones_bd"])


def _wkv_step_kernel(r_ref, k_ref, v_ref, a_ref, b_ref, lw_ref, szr_ref, bon_ref, gng_ref, gnb_ref,
                     ones_ref, s_hbm, o_ref, s_out, vec_t, y_t, s_buf, s_sem):
    hd = pl.program_id(0)
    ns = s_buf.shape[-1]
    ahead = STATE_SLOTS - 1

    def state_copy(head):
        slot = head % STATE_SLOTS
        return pltpu.make_async_copy(s_hbm.at[head], s_buf.at[slot], s_sem.at[slot])

    @pl.when(hd == 0)
    def _():
        for head in range(ahead):
            state_copy(head).start()

    @pl.when(hd + ahead < pl.num_programs(0))
    def _():
        state_copy(hd + ahead).start()

    @pl.when(hd == 0)
    def _():
        for n, ref in enumerate((a_ref, b_ref, k_ref, r_ref)):
            vec_t[n] = ref[0].T
        vec_t[4] = jnp.exp(lw_ref[0]).T
        vec_t[5] = v_ref[0].T

    base = pl.multiple_of(hd * HEAD_DIM, HEAD_DIM)
    hrows = pl.ds(base, HEAD_DIM)
    a_h, b_h, k_h, r_h, w_h = (vec_t[n, hrows, :] for n in range(5))
    sub = lax.broadcasted_iota(jnp.int32, (SUBLANES, ns), 0)
    state_copy(hd).wait()
    s_in = s_buf.at[hd % STATE_SLOTS]

    def value_group(g, carry):
        off = pl.multiple_of(g * SUBLANES, SUBLANES)
        v_g = vec_t[5, pl.ds(base + off, SUBLANES), :]
        y_g = jnp.zeros((SUBLANES, ns), F32)
        for ii in range(SUBLANES):
            s0 = s_in[off + ii]
            sa = jnp.sum(s0 * a_h, axis=0, keepdims=True)
            s1 = s0 * w_h + sa * b_h + v_g[ii:ii + 1, :] * k_h
            s_out[0, off + ii] = s1
            y_g = jnp.where(sub == ii, jnp.sum(s1 * r_h, axis=0, keepdims=True), y_g)
        y_t[pl.ds(base + off, SUBLANES), :] = y_g
        return carry

    lax.fori_loop(0, HEAD_DIM // SUBLANES, value_group, 0)

    @pl.when(hd == pl.num_programs(0) - 1)
    def _():
        o_ref[0] = _group_norm_gate(y_t[...].T, bon_ref[0], szr_ref[0], gng_ref[...], gnb_ref[...], ones_ref[...])


def _wkv_sample(pre, state_t, p):
    ns = state_t.shape[-1]
    assert N_HEADS >= STATE_SLOTS
    vec = pl.BlockSpec((1, ns, D_RWKV), lambda h: (0, 0, 0))
    st = pl.BlockSpec((1, HEAD_DIM, HEAD_DIM, ns), lambda h: (h, 0, 0, 0))
    return pl.pallas_call(
        _wkv_step_kernel,
        grid=(N_HEADS,),
        in_specs=[vec] * 8 + [_const_spec((1, D_RWKV)), _const_spec((1, D_RWKV)),
                              _const_spec((MXU_DIM, MXU_DIM)), pl.BlockSpec(memory_space=pltpu.HBM)],
        out_specs=[vec, st],
        out_shape=[jax.ShapeDtypeStruct((1, ns, D_RWKV), BF16),
                   jax.ShapeDtypeStruct(state_t.shape, F32)],
        scratch_shapes=[pltpu.VMEM((6, D_RWKV, ns), F32), pltpu.VMEM((D_RWKV, ns), F32),
                        pltpu.VMEM((STATE_SLOTS, HEAD_DIM, HEAD_DIM, ns), F32),
                        pltpu.SemaphoreType.DMA((STATE_SLOTS,))],
        compiler_params=pltpu.CompilerParams(
            dimension_semantics=("arbitrary",), vmem_limit_bytes=VMEM_LIMIT),
        name="wkv_sample",
    )(*pre, p["gn_g"], p["gn_b"], p["ones_bd"], state_t)


def _conv_tile(u, cw_ref, cb_ref, ubuf):
    tm = u.shape[0]
    ubuf[CONV_PAD:CONV_PAD + tm, :] = u
    win = ubuf[...]
    rows = tm + CONV_PAD
    base = CONV_PAD - (CONV_W - 1)
    acc = jnp.broadcast_to(cb_ref[...], u.shape)
    for s in range(SUBLANES):
        shifted = win if s == 0 else pltpu.roll(win, shift=rows - s, axis=0)
        for kk in range(CONV_W):
            if (base + kk) % SUBLANES == s:
                q = base + kk - s
                acc = acc + cw_ref[kk:kk + 1, :] * shifted[q:q + tm, :]
    ubuf[0:CONV_PAD, :] = win[tm:tm + CONV_PAD, :]
    return acc


def _post_tail(conv, x, o_r, szc, sgr, sgc, lng_ref, lnb_ref, wbr_ref, wbc_ref, wout_ref, gpost_ref):
    cm = jnp.mean(conv, axis=-1, keepdims=True)
    d = conv - cm
    cv = jnp.mean(d * d, axis=-1, keepdims=True)
    cn = d * lax.rsqrt(cv + LN_EPS) * lng_ref[...] + lnb_ref[...]
    o_c = (cn * _sigmoid(cn)) * szc
    branch_c = _dot(o_c.astype(BF16), wbc_ref[...])
    branch_r = _dot(o_r, wbr_ref[...])
    merged = sgr * branch_r + sgc * branch_c
    out = _dot(merged.astype(BF16), wout_ref[...])
    ms = jnp.mean(out * out, axis=-1, keepdims=True)
    return x + out * lax.rsqrt(ms + RMS_EPS) * gpost_ref[...]


def _post_prompt_kernel(x_ref, or_ref, u_ref, szc_ref, sgr_ref, sgc_ref, cw_ref, cb_ref, lng_ref, lnb_ref,
                        wbr_ref, wbc_ref, wout_ref, gpost_ref, y_ref, ubuf):
    @pl.when(pl.program_id(1) == 0)
    def _():
        ubuf[0:CONV_PAD, :] = jnp.zeros((CONV_PAD, D_CONV), F32)

    conv = _conv_tile(u_ref[0], cw_ref, cb_ref, ubuf)
    y_ref[0] = _post_tail(conv, x_ref[0], or_ref[0], szc_ref[0], sgr_ref[0], sgc_ref[0],
                          lng_ref, lnb_ref, wbr_ref, wbc_ref, wout_ref, gpost_ref)


def _post_sample_kernel(x_ref, or_ref, u_ref, szc_ref, sgr_ref, sgc_ref, cw_ref, cb_ref, lng_ref, lnb_ref,
                        wbr_ref, wbc_ref, wout_ref, gpost_ref, st_ref, y_ref, st_out):
    u = u_ref[0]
    acc = jnp.broadcast_to(cb_ref[...], u.shape)
    for kk in range(CONV_W - 1):
        acc = acc + cw_ref[kk:kk + 1, :] * st_ref[kk]
    acc = acc + cw_ref[CONV_W - 1:CONV_W, :] * u
    st_out[0:CONV_W - 2] = st_ref[1:CONV_W - 1]
    st_out[CONV_W - 2] = u
    y_ref[0] = _post_tail(acc, x_ref[0], or_ref[0], szc_ref[0], sgr_ref[0], sgc_ref[0],
                          lng_ref, lnb_ref, wbr_ref, wbc_ref, wout_ref, gpost_ref)


def _post(x, o_r, u, szc, sgr, sgc, conv_state, p, tm):
    B, T, _ = x.shape
    sample = conv_state is not None
    tile = lambda w: pl.BlockSpec((1, tm, w), lambda b, j: (b, j, 0))
    in_specs = [tile(D_MODEL), tile(D_RWKV), tile(D_CONV), tile(D_CONV), tile(D_MODEL), tile(D_MODEL),
                _const_spec((CONV_W, D_CONV)), _const_spec((1, D_CONV)), _const_spec((1, D_CONV)),
                _const_spec((1, D_CONV)), _const_spec((D_RWKV, D_MODEL)), _const_spec((D_CONV, D_MODEL)),
                _const_spec((D_MODEL, D_MODEL)), _const_spec((1, D_MODEL))]
    args = [x, o_r, u, szc, sgr, sgc, p["conv_w"], p["conv_b"], p["ln_g"], p["ln_b"],
            p["w_br"], p["w_bc"], p["w_out"], p["gpost"]]
    out_specs = [tile(D_MODEL)]
    out_shape = [jax.ShapeDtypeStruct((B, T, D_MODEL), F32)]
    scratch = []
    if sample:
        st = pl.BlockSpec((CONV_W - 1, tm, D_CONV), lambda b, j: (0, j, 0))
        in_specs.append(st)
        args.append(conv_state)
        out_specs.append(st)
        out_shape.append(jax.ShapeDtypeStruct(conv_state.shape, F32))
    else:
        scratch = [pltpu.VMEM((tm + CONV_PAD, D_CONV), F32)]
    return pl.pallas_call(
        _post_sample_kernel if sample else _post_prompt_kernel,
        grid=(B, T // tm),
        in_specs=in_specs,
        out_specs=out_specs,
        out_shape=out_shape,
        scratch_shapes=scratch,
        compiler_params=pltpu.CompilerParams(
            dimension_semantics=("arbitrary", "arbitrary"), vmem_limit_bytes=VMEM_LIMIT),
        name="post_sample" if sample else "post_prompt",
    )(*args)


def _layer_params(l, norm_pre_g, w_in, mu_shift, decay_w0, decay_w2, iclr_a0, iclr_a2, k_k, k_a, r_k, gn_g, gn_b,
                  conv_glu_b, conv_w, conv_b, ln_c_g, ln_c_b, w_branch_r, w_branch_c, w_out, norm_post_g):
    row = lambda t: t[l].reshape(1, -1).astype(F32)
    zeros = jnp.zeros((LORA, D_RWKV), F32)
    w2c = jnp.concatenate([jnp.concatenate([decay_w2[l], zeros], axis=1),
                           jnp.concatenate([zeros, iclr_a2[l]], axis=1)], axis=0)
    blk = jnp.arange(MXU_DIM) // HEAD_DIM
    return dict(
        gpre=row(norm_pre_g), w_in=w_in[l].astype(BF16), mu=row(mu_shift), w0=row(decay_w0), a0=row(iclr_a0),
        w2c=w2c.astype(BF16), k_k=row(k_k), k_a=row(k_a), r_k=row(r_k), glu_b=row(conv_glu_b),
        ones_bd=(blk[:, None] == blk[None, :]).astype(BF16),
        gn_g=row(gn_g), gn_b=row(gn_b), conv_w=conv_w[l].astype(F32), conv_b=row(conv_b),
        ln_g=row(ln_c_g), ln_b=row(ln_c_b), w_br=w_branch_r[l].astype(BF16), w_bc=w_branch_c[l].astype(BF16),
        w_out=w_out[l].astype(BF16), gpost=row(norm_post_g))


def _prompt_layer(x, p, tm_prep, tm_wkv, tm_post):
    T = x.shape[1]
    *pre, u, szc, sgr, sgc, h_last = _prep(x, None, p, tm_prep)
    o_r, wkv = _wkv_prompt(pre, p, tm_wkv)
    (y,) = _post(x, o_r, u, szc, sgr, sgc, None, p, tm_post)
    return y, h_last[:, SUBLANES - 1], wkv, u[:, T - (CONV_W - 1):]


def _sample_layer(x, shift, wkv, conv, p, tm_post):
    ns = x.shape[0]
    xs = x.reshape(1, ns, D_MODEL)
    *pre, u, szc, sgr, sgc, h = _prep(xs, shift.reshape(1, ns, D_MODEL), p, ns)
    o_r, wkv_new = _wkv_sample(pre, jnp.transpose(wkv, (1, 2, 3, 0)), p)
    y, conv_new = _post(xs, o_r, u, szc, sgr, sgc, jnp.transpose(conv, (1, 0, 2)), p, tm_post)
    return (y.reshape(ns, 1, D_MODEL), h.reshape(ns, D_MODEL), jnp.transpose(wkv_new, (3, 0, 1, 2)),
            jnp.transpose(conv_new, (1, 0, 2)))


def kernel(x_prompt, x_sample, state_shift, state_wkv, state_conv, norm_pre_g, w_in, mu_shift, decay_w0, decay_w2,
           iclr_a0, iclr_a2, k_k, k_a, r_k, gn_g, gn_b, conv_glu_b, conv_w, conv_b, ln_c_g, ln_c_b, w_branch_r,
           w_branch_c, w_out, norm_post_g):
    depth = w_in.shape[0]
    xp, xs = x_prompt, x_sample
    outs = [[] for _ in range(6)]
    for l in range(depth):
        p = _layer_params(l, norm_pre_g, w_in, mu_shift, decay_w0, decay_w2, iclr_a0, iclr_a2, k_k, k_a, r_k, gn_g,
                          gn_b, conv_glu_b, conv_w, conv_b, ln_c_g, ln_c_b, w_branch_r, w_branch_c, w_out,
                          norm_post_g)
        xp, a1, a2, a3 = _prompt_layer(xp, p, tm_prep=256, tm_wkv=256, tm_post=512)
        xs, b1, b2, b3 = _sample_layer(xs, state_shift[l], state_wkv[l], state_conv[l], p, tm_post=64)
        for lst, val in zip(outs, (a1, a2, a3, b1, b2, b3)):
            lst.append(val)
    return (xp, xs) + tuple(jnp.stack(o) for o in outs)
```

```python
import functools
import math

import jax
import jax.numpy as jnp
from jax import lax
from jax.experimental import pallas as pl
from jax.experimental.pallas import tpu as pltpu

F32 = jnp.float32
BF16 = jnp.bfloat16

D_MODEL = 1024
HEAD_DIM = 64
N_HEADS = D_MODEL // HEAD_DIM
D_RWKV = N_HEADS * HEAD_DIM
D_CONV = D_MODEL // 2
CONV_W = 31
LORA = 64
RMS_EPS = 1e-6
GN_EPS = 64e-5
LN_EPS = 1e-5
N_SHIFT = 4 * D_RWKV + 2 * LORA
N_IN = N_SHIFT + 2 * D_CONV + D_CONV + 2 * D_MODEL

SUBLANES = 8
PAIR = 2 * HEAD_DIM
N_PAIRS = N_HEADS // 2
MXU_DIM = 256
CHUNK = 64
TILE_CHUNKS = 4
WKV_GROUP = 2
STATE_SLOTS = 3
STAGE_SLOTS = 3
STAGE_COLS = 512
CONV_PAD = 32
VMEM_LIMIT = 56 * 1024 * 1024

_O_R, _O_K, _O_V, _O_ZR = 0, D_RWKV, 2 * D_RWKV, 3 * D_RWKV
_O_LORA = 4 * D_RWKV
_O_U = N_SHIFT
_O_ZC = _O_U + 2 * D_CONV
_O_GR = _O_ZC + D_CONV
_O_GC = _O_GR + D_MODEL


def _dot(a, b):
    return jnp.dot(a, b, preferred_element_type=F32)


def _sigmoid(x):
    return 1.0 / (1.0 + jnp.exp(-x))


def _head_sum(x, ones_bd):
    xb = x.astype(BF16)
    return jnp.concatenate([_dot(xb[:, c * MXU_DIM:(c + 1) * MXU_DIM], ones_bd)
                            for c in range(x.shape[1] // MXU_DIM)], axis=1)


def _group_norm_gate(y, bonus, szr, gng, gnb, ones_bd):
    mean = _head_sum(y, ones_bd) * (1.0 / HEAD_DIM)
    d = y - mean
    var = _head_sum(d * d, ones_bd) * (1.0 / HEAD_DIM)
    y_gn = d * lax.rsqrt(var + GN_EPS) * gng + gnb
    return ((y_gn + bonus) * szr).astype(BF16)


def _stage_weights(w_hbm, layer, w_bf, stage, sem):
    chunks = [(off, min(STAGE_COLS, N_IN - off)) for off in range(0, N_IN, STAGE_COLS)]
    ahead = STAGE_SLOTS - 1

    def chunk_copy(c):
        off, width = chunks[c]
        slot = c % STAGE_SLOTS
        return pltpu.make_async_copy(w_hbm.at[layer, :, pl.ds(off, width)],
                                     stage.at[slot, :, pl.ds(0, width)], sem.at[slot])

    for c in range(min(ahead, len(chunks))):
        chunk_copy(c).start()
    for c, (off, width) in enumerate(chunks):
        if c + ahead < len(chunks):
            chunk_copy(c + ahead).start()
        chunk_copy(c).wait()
        w_bf[:, off:off + width] = stage[c % STAGE_SLOTS, :, :width].astype(BF16)


def _prep_kernel(sample, layer, *refs):
    if sample:
        (x_ref, prev_ref, gpre_ref, win_ref, mu_ref, w0_ref, a0_ref, w2c_ref, kk_ref, ka_ref, rk_ref,
         glub_ref, ones_ref,
         r_out, k_out, v_out, a_out, b_out, lw_out, szr_out, bon_out, u_out, szc_out, sgr_out, sgc_out,
         h_out) = refs
    else:
        (x_ref, gpre_ref, w_hbm, mu_ref, w0_ref, a0_ref, w2c_ref, kk_ref, ka_ref, rk_ref,
         glub_ref, ones_ref,
         r_out, k_out, v_out, a_out, b_out, lw_out, szr_out, bon_out, u_out, szc_out, sgr_out, sgc_out,
         h_out, wbf_out, carry_ref, win_ref, stage, stage_sem, export_sem) = refs
        step = pl.program_id(0) * pl.num_programs(1) + pl.program_id(1)
        export = pltpu.make_async_copy(win_ref, wbf_out, export_sem.at[0])

        @pl.when(step == 0)
        def _():
            _stage_weights(w_hbm, layer, win_ref, stage, stage_sem)
            export.start()

    x = x_ref[0]
    tm = x.shape[0]
    ms = jnp.mean(x * x, axis=-1, keepdims=True)
    h = x * lax.rsqrt(ms + RMS_EPS) * gpre_ref[...]
    hb = h.astype(BF16)

    def proj(off, width):
        return _dot(hb, win_ref[:, off:off + width])

    if sample:
        h_out[0] = h
        pb = prev_ref[0].astype(BF16)
    else:
        h_out[0] = h[tm - SUBLANES:, :]

        @pl.when(pl.program_id(1) == 0)
        def _():
            carry_ref[...] = jnp.zeros_like(carry_ref)
        row0 = lax.broadcasted_iota(jnp.int32, (tm, 1), 0) == 0

    def shifted(off, width):
        p = proj(off, width)
        if sample:
            pp = _dot(pb, win_ref[:, off:off + width])
        else:
            rolled = pltpu.roll(p, shift=1, axis=0)
            pp = jnp.where(row0, carry_ref[0:1, off:off + width], rolled)
            carry_ref[0:1, off:off + width] = p[tm - 1:tm, :]
        return p + (pp - p) * mu_ref[:, off:off + width]

    ones_bd = ones_ref[...]

    lo = shifted(_O_LORA, 2 * LORA)
    k = shifted(_O_K, D_RWKV)

    lane = lax.broadcasted_iota(jnp.int32, lo.shape, 1)
    lo = jnp.where(lane < LORA, jnp.tanh(lo), lo)
    lor = _dot(lo.astype(BF16), w2c_ref[...])

    zr = shifted(_O_ZR, D_RWKV)
    szr_out[0] = zr * _sigmoid(zr)

    kk = k * kk_ref[...]
    kk_ss = _head_sum(kk * kk, ones_bd)

    u_in = proj(_O_U, 2 * D_CONV) + glub_ref[...]
    u_out[0] = u_in[:, :D_CONV] * _sigmoid(u_in[:, D_CONV:])

    w_raw = w0_ref[...] + lor[:, :D_RWKV]
    lw_out[0] = (-math.exp(-0.5)) * _sigmoid(w_raw)
    iclr = _sigmoid(a0_ref[...] + lor[:, D_RWKV:])
    kk = kk * lax.rsqrt(kk_ss + 1e-12)
    k2 = k * (1.0 + (iclr - 1.0) * ka_ref[...])
    k_out[0] = k2
    a_out[0] = -kk
    b_out[0] = kk * iclr

    zc = proj(_O_ZC, D_CONV)
    szc_out[0] = zc * _sigmoid(zc)
    sgr_out[0] = _sigmoid(proj(_O_GR, D_MODEL))
    sgc_out[0] = _sigmoid(proj(_O_GC, D_MODEL))

    r = shifted(_O_R, D_RWKV)
    r_out[0] = r
    v = shifted(_O_V, D_RWKV)
    v_out[0] = v
    bon_out[0] = _head_sum(r * k2 * rk_ref[...], ones_bd) * v

    if not sample:
        @pl.when(step == pl.num_programs(0) * pl.num_programs(1) - 1)
        def _():
            export.wait()


def _const_spec(shape):
    nd = len(shape)
    return pl.BlockSpec(shape, lambda *_: (0,) * nd, pipeline_mode=pl.Buffered(1))


def _prep(x, prev, p, tm):
    B, T, _ = x.shape
    sample = prev is not None
    tile = lambda w: pl.BlockSpec((1, tm, w), lambda b, j: (b, j, 0))
    hbm = pl.BlockSpec(memory_space=pltpu.HBM)
    in_specs = [tile(D_MODEL)] + ([tile(D_MODEL)] if sample else []) + [
        _const_spec((1, D_MODEL)), _const_spec((D_MODEL, N_IN)) if sample else hbm, _const_spec((1, N_SHIFT)),
        _const_spec((1, D_RWKV)), _const_spec((1, D_RWKV)), _const_spec((2 * LORA, 2 * D_RWKV)),
        _const_spec((1, D_RWKV)), _const_spec((1, D_RWKV)), _const_spec((1, D_RWKV)),
        _const_spec((1, 2 * D_CONV)), _const_spec((MXU_DIM, MXU_DIM))]
    widths = [D_RWKV] * 8 + [D_CONV, D_CONV, D_MODEL, D_MODEL]
    h_spec = tile(D_MODEL) if sample else pl.BlockSpec((1, SUBLANES, D_MODEL), lambda b, j: (b, 0, 0))
    out_specs = [tile(w) for w in widths] + [h_spec]
    out_shape = [jax.ShapeDtypeStruct((B, T, w), F32) for w in widths] + [
        jax.ShapeDtypeStruct((B, T if sample else SUBLANES, D_MODEL), F32)]
    args = [x] + ([prev] if sample else []) + [
        p["gpre"], p["w_in"] if sample else p["w_in_f32"], p["mu"], p["w0"], p["a0"], p["w2c"], p["k_k"], p["k_a"],
        p["r_k"], p["glu_b"], p["ones_bd"]]
    scratch_shapes = []
    if not sample:
        out_specs.append(hbm)
        out_shape.append(jax.ShapeDtypeStruct((D_MODEL, N_IN), BF16))
        scratch_shapes = [pltpu.VMEM((SUBLANES, N_SHIFT), F32), pltpu.VMEM((D_MODEL, N_IN), BF16),
                          pltpu.VMEM((STAGE_SLOTS, D_MODEL, STAGE_COLS), p["w_in_f32"].dtype),
                          pltpu.SemaphoreType.DMA((STAGE_SLOTS,)), pltpu.SemaphoreType.DMA((1,))]
    return pl.pallas_call(
        functools.partial(_prep_kernel, sample, None if sample else p["layer"]),
        grid=(B, T // tm),
        in_specs=in_specs,
        out_specs=out_specs,
        out_shape=out_shape,
        scratch_shapes=scratch_shapes,
        compiler_params=pltpu.CompilerParams(
            dimension_semantics=("arbitrary", "arbitrary"), vmem_limit_bytes=VMEM_LIMIT),
        name="prep_sample" if sample else "prep_prompt",
    )(*args)


def _block_diag(x, same_head):
    return jnp.where(same_head, jnp.concatenate([x, x], axis=0), jnp.zeros((), x.dtype))


def _drain(steps):
    for _ in steps:
        pass


def _wkv_masks():
    row = lax.broadcasted_iota(jnp.int32, (CHUNK, PAIR), 0)
    col = lax.broadcasted_iota(jnp.int32, (CHUNK, PAIR), 1) % HEAD_DIM
    brow = lax.broadcasted_iota(jnp.int32, (PAIR, PAIR), 0)
    bcol = lax.broadcasted_iota(jnp.int32, (PAIR, PAIR), 1)
    trow = lax.broadcasted_iota(jnp.int32, (CHUNK, CHUNK), 0)
    tcol = lax.broadcasted_iota(jnp.int32, (CHUNK, CHUNK), 1)
    return dict(strict=row > col, incl=row >= col, eye_pair=(row == col).astype(F32),
                same_head=(brow // HEAD_DIM) == (bcol // HEAD_DIM), eye_full=brow == bcol,
                first_head=lax.broadcasted_iota(jnp.int32, (CHUNK, PAIR), 1) < HEAD_DIM,
                tri=(trow >= tcol).astype(BF16))


def _block_diag_t(x, same_head):
    return jnp.where(same_head, jnp.concatenate([x, x], axis=0).T, 0.0).astype(BF16)


def _wkv_chunk_operators(r_ref, k_ref, v_ref, a_ref, b_ref, lw_ref, first_chunk, m, tick):
    strict, incl, eye_pair, same_head = m["strict"], m["incl"], m["eye_pair"], m["same_head"]
    bd = lambda x: _block_diag(x.astype(BF16), same_head)
    bd_t = lambda x: _block_diag_t(x, same_head)

    def chunk_inputs(row0):
        rows = pl.ds(row0, CHUNK)
        lw = lw_ref[0, rows, :]
        l1 = lw.astype(BF16)
        rem = lw - l1.astype(F32)
        l2 = rem.astype(BF16)
        l3 = (rem - l2.astype(F32)).astype(BF16)
        cl = _dot(m["tri"], l1) + _dot(m["tri"], l2) + _dot(m["tri"], l3)
        e_neg = jnp.exp(-cl)
        w_end = jnp.exp(cl[CHUNK - 1:CHUNK, :])
        b_t = b_ref[0, rows, :] * e_neg
        k_t = k_ref[0, rows, :] * e_neg
        return dict(r=r_ref[0, rows, :] * jnp.exp(cl), a=(a_ref[0, rows, :] * jnp.exp(cl - lw)).astype(BF16),
                    b=b_t, k=k_t, be=(b_t * w_end).astype(BF16), ke=(k_t * w_end).astype(BF16),
                    v=v_ref[0, rows, :], w_end=w_end)

    ch = []
    for c in range(WKV_GROUP):
        ch.append(chunk_inputs((first_chunk + c) * CHUNK))
        tick()
    pairs = range(N_PAIRS)
    sls = [slice(p * PAIR, (p + 1) * PAIR) for p in pairs]
    units = [(c, p) for c in range(WKV_GROUP) for p in pairs]
    us = range(len(units))
    pick = lambda name: [ch[c][name][:, sls[p]] for c, p in units]
    at, rt, bt, kt, be, ke, vp = (pick(n) for n in ("a", "r", "b", "k", "be", "ke", "v"))
    g = [_dot(jnp.concatenate([at[u], rt[u].astype(BF16)], axis=0),
              jnp.concatenate([bd_t(bt[u]), bd_t(kt[u])], axis=1)) for u in us]
    tick()
    l_ab = [jnp.where(strict, g[u][:CHUNK, :PAIR], 0.0) for u in us]
    m_rb = [jnp.where(incl, g[u][CHUNK:, :PAIR], 0.0).astype(BF16) for u in us]
    km = [jnp.concatenate([jnp.where(strict, g[u][:CHUNK, PAIR:], 0.0),
                           jnp.where(incl, g[u][CHUNK:, PAIR:], 0.0)], axis=0).astype(BF16) for u in us]
    uy = [_dot(km[u], bd(vp[u])) for u in us]
    tick()
    uv0 = [uy[u][:CHUNK].astype(BF16) for u in us]
    t_acc = [eye_pair + l_ab[u] for u in us]
    lp = [_dot(l_ab[u].astype(BF16), bd(l_ab[u])).astype(BF16) for u in us]
    tick()
    for _ in range(4):
        prod = [_dot(lp[u], jnp.concatenate([bd(lp[u]), bd(t_acc[u])], axis=1)) for u in us]
        lp = [prod[u][:, :PAIR].astype(BF16) for u in us]
        t_acc = [t_acc[u] + prod[u][:, PAIR:] for u in us]
        tick()
    t_acc = [t_acc[u] + _dot(lp[u], bd(t_acc[u])) for u in us]
    tick()
    au = [_dot(t_acc[u].astype(BF16), jnp.concatenate([bd(at[u]), bd(uv0[u])], axis=1)) for u in us]
    tick()
    a_hat = [au[u][:, :PAIR] for u in us]
    u_v = [au[u][:, PAIR:] for u in us]
    ry = [_dot(m_rb[u], jnp.concatenate([bd(a_hat[u]), bd(u_v[u])], axis=1)) for u in us]
    tick()
    r_hat = [(rt[u] + ry[u][:, :PAIR]).astype(BF16) for u in us]
    y_v = [ry[u][:, PAIR:] + uy[u][CHUNK:] for u in us]
    phi = [_dot(a_hat[u].T.astype(BF16), be[u]) for u in us]
    tick()
    f = [_dot(jnp.concatenate([u_v[u], vp[u]], axis=0).T.astype(BF16),
              jnp.concatenate([be[u], ke[u]], axis=0)) for u in us]
    tick()
    phi = [(jnp.where(same_head, phi[u], 0.0)
            + jnp.where(m["eye_full"], ch[units[u][0]]["w_end"][:, sls[units[u][1]]], 0.0)).astype(BF16)
           for u in us]
    z = [jnp.where(m["first_head"], f[u][:HEAD_DIM, :], f[u][HEAD_DIM:, :]) for u in us]
    return r_hat, y_v, phi, z


def _wkv_recurrence_steps(slot, rh_ref, yv_ref, phi_ref, z_ref, s_ref, y_ref, m):
    pairs = range(N_PAIRS)
    state = [s_ref[p] for p in pairs]
    for c in range(TILE_CHUNKS):
        us = [c * N_PAIRS + p for p in pairs]
        ys = [_dot(rh_ref[slot, us[p]], _block_diag_t(state[p], m["same_head"])) + yv_ref[slot, us[p]]
              for p in pairs]
        state = [_dot(state[p].astype(BF16), phi_ref[slot, us[p]]) + z_ref[slot, us[p]] for p in pairs]
        y_ref[c * CHUNK:(c + 1) * CHUNK, :] = jnp.concatenate(ys, axis=1)
        yield
    for p in pairs:
        s_ref[p] = state[p]


def _wkv_prompt_kernel(nt, r_ref, k_ref, v_ref, a_ref, b_ref, lw_ref, szr_ref, bon_ref, gng_ref, gnb_ref, ones_ref,
                       o_ref, s_out, s_ref, y_ref, rh_ref, yv_ref, phi_ref, z_ref):
    s = pl.program_id(0)
    prev = jnp.maximum(s - 1, 0)
    cur_slot = s % 2

    @pl.when(s == 0)
    def _():
        for ref in (rh_ref, yv_ref, phi_ref, z_ref):
            ref[1] = jnp.zeros_like(ref[1])

    @pl.when(prev % nt == 0)
    def _():
        s_ref[...] = jnp.zeros_like(s_ref)

    m = _wkv_masks()

    def previous_tile():
        yield from _wkv_recurrence_steps(1 - cur_slot, rh_ref, yv_ref, phi_ref, z_ref, s_ref, y_ref, m)
        o_ref[0] = _group_norm_gate(y_ref[...], bon_ref[0], szr_ref[0], gng_ref[...], gnb_ref[...], ones_ref[...])

    prev_steps = previous_tile()
    for first_chunk in range(0, TILE_CHUNKS, WKV_GROUP):
        ops = _wkv_chunk_operators(r_ref, k_ref, v_ref, a_ref, b_ref, lw_ref, first_chunk, m,
                                   tick=lambda: next(prev_steps, None))
        for i in range(WKV_GROUP * N_PAIRS):
            u = first_chunk * N_PAIRS + i
            for ref, val in zip((rh_ref, yv_ref, phi_ref, z_ref), ops):
                ref[cur_slot, u] = val[i]
    _drain(prev_steps)

    @pl.when((s > 0) & (prev % nt == nt - 1))
    def _():
        for hd in range(N_HEADS):
            lo = (hd % 2) * HEAD_DIM
            s_out[0, hd] = s_ref[hd // 2][:, lo:lo + HEAD_DIM]


def _wkv_prompt(pre, p, tm):
    r, k, v, a, b, lw, szr, bon = pre
    B, T, _ = r.shape
    assert T % tm == 0 and tm == TILE_CHUNKS * CHUNK and TILE_CHUNKS % WKV_GROUP == 0, (T, tm)
    nt = T // tm
    n_tiles = B * nt
    n_units = TILE_CHUNKS * N_PAIRS
    cur = pl.BlockSpec((1, tm, D_RWKV),
                       lambda s: (jnp.minimum(s, n_tiles - 1) // nt, jnp.minimum(s, n_tiles - 1) % nt, 0))
    prv = pl.BlockSpec((1, tm, D_RWKV), lambda s: (jnp.maximum(s - 1, 0) // nt, jnp.maximum(s - 1, 0) % nt, 0))
    state = pl.BlockSpec((1, N_HEADS, HEAD_DIM, HEAD_DIM), lambda s: (jnp.maximum(s - 1, 0) // nt, 0, 0, 0))
    return pl.pallas_call(
        functools.partial(_wkv_prompt_kernel, nt),
        grid=(n_tiles + 1,),
        in_specs=[cur] * 6 + [prv] * 2 + [_const_spec((1, D_RWKV)), _const_spec((1, D_RWKV)),
                                          _const_spec((MXU_DIM, MXU_DIM))],
        out_specs=[prv, state],
        out_shape=[jax.ShapeDtypeStruct((B, T, D_RWKV), BF16),
                   jax.ShapeDtypeStruct((B, N_HEADS, HEAD_DIM, HEAD_DIM), F32)],
        scratch_shapes=[pltpu.VMEM((N_PAIRS, HEAD_DIM, PAIR), F32), pltpu.VMEM((tm, D_RWKV), F32),
                        pltpu.VMEM((2, n_units, CHUNK, PAIR), BF16), pltpu.VMEM((2, n_units, CHUNK, PAIR), F32),
                        pltpu.VMEM((2, n_units, PAIR, PAIR), BF16), pltpu.VMEM((2, n_units, CHUNK, PAIR), F32)],
        compiler_params=pltpu.CompilerParams(
            dimension_semantics=("arbitrary",), vmem_limit_bytes=VMEM_LIMIT),
        name="wkv_prompt",
    )(r, k, v, a, b, lw, szr, bon, p["gn_g"], p["gn_b"], p["ones_bd"])


def _wkv_step_kernel(r_ref, k_ref, v_ref, a_ref, b_ref, lw_ref, szr_ref, bon_ref, gng_ref, gnb_ref,
                     ones_ref, s_hbm, o_ref, s_out, vec_t, y_t, s_buf, s_sem):
    hd = pl.program_id(0)
    ns = s_buf.shape[-1]
    ahead = STATE_SLOTS - 1

    def state_copy(head):
        slot = head % STATE_SLOTS
        return pltpu.make_async_copy(s_hbm.at[head], s_buf.at[slot], s_sem.at[slot])

    @pl.when(hd == 0)
    def _():
        for head in range(ahead):
            state_copy(head).start()

    @pl.when(hd + ahead < pl.num_programs(0))
    def _():
        state_copy(hd + ahead).start()

    @pl.when(hd == 0)
    def _():
        for n, ref in enumerate((a_ref, b_ref, k_ref, r_ref)):
            vec_t[n] = ref[0].T
        vec_t[4] = jnp.exp(lw_ref[0]).T
        vec_t[5] = v_ref[0].T

    base = pl.multiple_of(hd * HEAD_DIM, HEAD_DIM)
    hrows = pl.ds(base, HEAD_DIM)
    a_h, b_h, k_h, r_h, w_h = (vec_t[n, hrows, :] for n in range(5))
    sub = lax.broadcasted_iota(jnp.int32, (SUBLANES, ns), 0)
    state_copy(hd).wait()
    s_in = s_buf.at[hd % STATE_SLOTS]

    def value_group(g, carry):
        off = pl.multiple_of(g * SUBLANES, SUBLANES)
        v_g = vec_t[5, pl.ds(base + off, SUBLANES), :]
        y_g = jnp.zeros((SUBLANES, ns), F32)
        for ii in range(SUBLANES):
            s0 = s_in[off + ii]
            sa = jnp.sum(s0 * a_h, axis=0, keepdims=True)
            s1 = s0 * w_h + sa * b_h + v_g[ii:ii + 1, :] * k_h
            s_out[0, off + ii] = s1
            y_g = jnp.where(sub == ii, jnp.sum(s1 * r_h, axis=0, keepdims=True), y_g)
        y_t[pl.ds(base + off, SUBLANES), :] = y_g
        return carry

    lax.fori_loop(0, HEAD_DIM // SUBLANES, value_group, 0)

    @pl.when(hd == pl.num_programs(0) - 1)
    def _():
        o_ref[0] = _group_norm_gate(y_t[...].T, bon_ref[0], szr_ref[0], gng_ref[...], gnb_ref[...], ones_ref[...])


def _wkv_sample(pre, state_t, p):
    ns = state_t.shape[-1]
    assert N_HEADS >= STATE_SLOTS
    vec = pl.BlockSpec((1, ns, D_RWKV), lambda h: (0, 0, 0))
    st = pl.BlockSpec((1, HEAD_DIM, HEAD_DIM, ns), lambda h: (h, 0, 0, 0))
    return pl.pallas_call(
        _wkv_step_kernel,
        grid=(N_HEADS,),
        in_specs=[vec] * 8 + [_const_spec((1, D_RWKV)), _const_spec((1, D_RWKV)),
                              _const_spec((MXU_DIM, MXU_DIM)), pl.BlockSpec(memory_space=pltpu.HBM)],
        out_specs=[vec, st],
        out_shape=[jax.ShapeDtypeStruct((1, ns, D_RWKV), BF16),
                   jax.ShapeDtypeStruct(state_t.shape, F32)],
        scratch_shapes=[pltpu.VMEM((6, D_RWKV, ns), F32), pltpu.VMEM((D_RWKV, ns), F32),
                        pltpu.VMEM((STATE_SLOTS, HEAD_DIM, HEAD_DIM, ns), F32),
                        pltpu.SemaphoreType.DMA((STATE_SLOTS,))],
        compiler_params=pltpu.CompilerParams(
            dimension_semantics=("arbitrary",), vmem_limit_bytes=VMEM_LIMIT),
        name="wkv_sample",
    )(*pre, p["gn_g"], p["gn_b"], p["ones_bd"], state_t)


def _conv_tile(u, cw_ref, cb_ref, ubuf):
    tm = u.shape[0]
    ubuf[CONV_PAD:CONV_PAD + tm, :] = u
    win = ubuf[...]
    rows = tm + CONV_PAD
    base = CONV_PAD - (CONV_W - 1)
    acc = jnp.broadcast_to(cb_ref[...], u.shape)
    for s in range(SUBLANES):
        shifted = win if s == 0 else pltpu.roll(win, shift=rows - s, axis=0)
        for kk in range(CONV_W):
            if (base + kk) % SUBLANES == s:
                q = base + kk - s
                acc = acc + cw_ref[kk:kk + 1, :] * shifted[q:q + tm, :]
    ubuf[0:CONV_PAD, :] = win[tm:tm + CONV_PAD, :]
    return acc


def _post_tail(conv, x, o_r, szc, sgr, sgc, lng_ref, lnb_ref, wbr_ref, wbc_ref, wout_ref, gpost_ref):
    cm = jnp.mean(conv, axis=-1, keepdims=True)
    d = conv - cm
    cv = jnp.mean(d * d, axis=-1, keepdims=True)
    cn = d * lax.rsqrt(cv + LN_EPS) * lng_ref[...] + lnb_ref[...]
    o_c = (cn * _sigmoid(cn)) * szc
    branch_c = _dot(o_c.astype(BF16), wbc_ref[...])
    branch_r = _dot(o_r, wbr_ref[...])
    merged = sgr * branch_r + sgc * branch_c
    out = _dot(merged.astype(BF16), wout_ref[...])
    ms = jnp.mean(out * out, axis=-1, keepdims=True)
    return x + out * lax.rsqrt(ms + RMS_EPS) * gpost_ref[...]


def _post_prompt_kernel(x_ref, or_ref, u_ref, szc_ref, sgr_ref, sgc_ref, cw_ref, cb_ref, lng_ref, lnb_ref,
                        wbr_ref, wbc_ref, wout_ref, gpost_ref, y_ref, ubuf):
    @pl.when(pl.program_id(1) == 0)
    def _():
        ubuf[0:CONV_PAD, :] = jnp.zeros((CONV_PAD, D_CONV), F32)

    conv = _conv_tile(u_ref[0], cw_ref, cb_ref, ubuf)
    y_ref[0] = _post_tail(conv, x_ref[0], or_ref[0], szc_ref[0], sgr_ref[0], sgc_ref[0],
                          lng_ref, lnb_ref, wbr_ref, wbc_ref, wout_ref, gpost_ref)


def _post_sample_kernel(x_ref, or_ref, u_ref, szc_ref, sgr_ref, sgc_ref, cw_ref, cb_ref, lng_ref, lnb_ref,
                        wbr_ref, wbc_ref, wout_ref, gpost_ref, st_ref, y_ref, st_out):
    u = u_ref[0]
    acc = jnp.broadcast_to(cb_ref[...], u.shape)
    for kk in range(CONV_W - 1):
        acc = acc + cw_ref[kk:kk + 1, :] * st_ref[kk]
    acc = acc + cw_ref[CONV_W - 1:CONV_W, :] * u
    st_out[0:CONV_W - 2] = st_ref[1:CONV_W - 1]
    st_out[CONV_W - 2] = u
    y_ref[0] = _post_tail(acc, x_ref[0], or_ref[0], szc_ref[0], sgr_ref[0], sgc_ref[0],
                          lng_ref, lnb_ref, wbr_ref, wbc_ref, wout_ref, gpost_ref)


def _post(x, o_r, u, szc, sgr, sgc, conv_state, p, tm):
    B, T, _ = x.shape
    sample = conv_state is not None
    tile = lambda w: pl.BlockSpec((1, tm, w), lambda b, j: (b, j, 0))
    in_specs = [tile(D_MODEL), tile(D_RWKV), tile(D_CONV), tile(D_CONV), tile(D_MODEL), tile(D_MODEL),
                _const_spec((CONV_W, D_CONV)), _const_spec((1, D_CONV)), _const_spec((1, D_CONV)),
                _const_spec((1, D_CONV)), _const_spec((D_RWKV, D_MODEL)), _const_spec((D_CONV, D_MODEL)),
                _const_spec((D_MODEL, D_MODEL)), _const_spec((1, D_MODEL))]
    args = [x, o_r, u, szc, sgr, sgc, p["conv_w"], p["conv_b"], p["ln_g"], p["ln_b"],
            p["w_br"], p["w_bc"], p["w_out"], p["gpost"]]
    out_specs = [tile(D_MODEL)]
    out_shape = [jax.ShapeDtypeStruct((B, T, D_MODEL), F32)]
    scratch = []
    if sample:
        st = pl.BlockSpec((CONV_W - 1, tm, D_CONV), lambda b, j: (0, j, 0))
        in_specs.append(st)
        args.append(conv_state)
        out_specs.append(st)
        out_shape.append(jax.ShapeDtypeStruct(conv_state.shape, F32))
    else:
        scratch = [pltpu.VMEM((tm + CONV_PAD, D_CONV), F32)]
    return pl.pallas_call(
        _post_sample_kernel if sample else _post_prompt_kernel,
        grid=(B, T // tm),
        in_specs=in_specs,
        out_specs=out_specs,
        out_shape=out_shape,
        scratch_shapes=scratch,
        compiler_params=pltpu.CompilerParams(
            dimension_semantics=("arbitrary", "arbitrary"), vmem_limit_bytes=VMEM_LIMIT),
        name="post_sample" if sample else "post_prompt",
    )(*args)


def _layer_params(l, norm_pre_g, w_in, mu_shift, decay_w0, decay_w2, iclr_a0, iclr_a2, k_k, k_a, r_k, gn_g, gn_b,
                  conv_glu_b, conv_w, conv_b, ln_c_g, ln_c_b, w_branch_r, w_branch_c, w_out, norm_post_g):
    row = lambda t: t[l].reshape(1, -1).astype(F32)
    zeros = jnp.zeros((LORA, D_RWKV), F32)
    w2c = jnp.concatenate([jnp.concatenate([decay_w2[l], zeros], axis=1),
                           jnp.concatenate([zeros, iclr_a2[l]], axis=1)], axis=0)
    blk = jnp.arange(MXU_DIM) // HEAD_DIM
    return dict(
        gpre=row(norm_pre_g), w_in_f32=w_in, layer=l, mu=row(mu_shift), w0=row(decay_w0), a0=row(iclr_a0),
        w2c=w2c.astype(BF16), k_k=row(k_k), k_a=row(k_a), r_k=row(r_k), glu_b=row(conv_glu_b),
        ones_bd=(blk[:, None] == blk[None, :]).astype(BF16),
        gn_g=row(gn_g), gn_b=row(gn_b), conv_w=conv_w[l].astype(F32), conv_b=row(conv_b),
        ln_g=row(ln_c_g), ln_b=row(ln_c_b), w_br=w_branch_r[l].astype(BF16), w_bc=w_branch_c[l].astype(BF16),
        w_out=w_out[l].astype(BF16), gpost=row(norm_post_g))


def _prompt_layer(x, p, tm_prep, tm_wkv, tm_post):
    T = x.shape[1]
    *pre, u, szc, sgr, sgc, h_last, w_in_bf16 = _prep(x, None, p, tm_prep)
    o_r, wkv = _wkv_prompt(pre, p, tm_wkv)
    (y,) = _post(x, o_r, u, szc, sgr, sgc, None, p, tm_post)
    return y, h_last[:, SUBLANES - 1], wkv, u[:, T - (CONV_W - 1):], w_in_bf16


def _sample_layer(x, shift, wkv, conv, p, tm_post):
    ns = x.shape[0]
    xs = x.reshape(1, ns, D_MODEL)
    *pre, u, szc, sgr, sgc, h = _prep(xs, shift.reshape(1, ns, D_MODEL), p, ns)
    o_r, wkv_new = _wkv_sample(pre, jnp.transpose(wkv, (1, 2, 3, 0)), p)
    y, conv_new = _post(xs, o_r, u, szc, sgr, sgc, jnp.transpose(conv, (1, 0, 2)), p, tm_post)
    return (y.reshape(ns, 1, D_MODEL), h.reshape(ns, D_MODEL), jnp.transpose(wkv_new, (3, 0, 1, 2)),
            jnp.transpose(conv_new, (1, 0, 2)))


def kernel(x_prompt, x_sample, state_shift, state_wkv, state_conv, norm_pre_g, w_in, mu_shift, decay_w0, decay_w2,
           iclr_a0, iclr_a2, k_k, k_a, r_k, gn_g, gn_b, conv_glu_b, conv_w, conv_b, ln_c_g, ln_c_b, w_branch_r,
           w_branch_c, w_out, norm_post_g):
    depth = w_in.shape[0]
    xp, xs = x_prompt, x_sample
    outs = [[] for _ in range(6)]
    for l in range(depth):
        p = _layer_params(l, norm_pre_g, w_in, mu_shift, decay_w0, decay_w2, iclr_a0, iclr_a2, k_k, k_a, r_k, gn_g,
                          gn_b, conv_glu_b, conv_w, conv_b, ln_c_g, ln_c_b, w_branch_r, w_branch_c, w_out,
                          norm_post_g)
        xp, a1, a2, a3, w_in_bf16 = _prompt_layer(xp, p, tm_prep=256, tm_wkv=256, tm_post=512)
        xs, b1, b2, b3 = _sample_layer(xs, state_shift[l], state_wkv[l], state_conv[l], dict(p, w_in=w_in_bf16),
                                       tm_post=64)
        for lst, val in zip(outs, (a1, a2, a3, b1, b2, b3)):
            lst.append(val)
    return (xp, xs) + tuple(jnp.stack(o) for o in outs)
```

```python
import functools
import math

import jax
import jax.numpy as jnp
from jax import lax
from jax.experimental import pallas as pl
from jax.experimental.pallas import tpu as pltpu

F32 = jnp.float32
BF16 = jnp.bfloat16

D_MODEL = 1024
HEAD_DIM = 64
N_HEADS = D_MODEL // HEAD_DIM
D_RWKV = N_HEADS * HEAD_DIM
D_CONV = D_MODEL // 2
CONV_W = 31
LORA = 64
RMS_EPS = 1e-6
GN_EPS = 64e-5
LN_EPS = 1e-5
N_SHIFT = 4 * D_RWKV + 2 * LORA
N_IN = N_SHIFT + 2 * D_CONV + D_CONV + 2 * D_MODEL

SUBLANES = 8
PAIR = 2 * HEAD_DIM
N_PAIRS = N_HEADS // 2
MXU_DIM = 256
CHUNK = 64
TILE_CHUNKS = 4
WKV_GROUP = 2
STATE_SLOTS = 3
STAGE_SLOTS = 3
STAGE_ROWS = 64
CONV_PAD = 32
VMEM_LIMIT = 56 * 1024 * 1024

_O_R, _O_K, _O_V, _O_ZR = 0, D_RWKV, 2 * D_RWKV, 3 * D_RWKV
_O_LORA = 4 * D_RWKV
_O_U = N_SHIFT
_O_ZC = _O_U + 2 * D_CONV
_O_GR = _O_ZC + D_CONV
_O_GC = _O_GR + D_MODEL


def _dot(a, b):
    return jnp.dot(a, b, preferred_element_type=F32)


def _sigmoid(x):
    return 1.0 / (1.0 + jnp.exp(-x))


def _head_sum(x, ones_bd):
    xb = x.astype(BF16)
    return jnp.concatenate([_dot(xb[:, c * MXU_DIM:(c + 1) * MXU_DIM], ones_bd)
                            for c in range(x.shape[1] // MXU_DIM)], axis=1)


def _group_norm_gate(y, bonus, szr, gng, gnb, ones_bd):
    mean = _head_sum(y, ones_bd) * (1.0 / HEAD_DIM)
    d = y - mean
    var = _head_sum(d * d, ones_bd) * (1.0 / HEAD_DIM)
    y_gn = d * lax.rsqrt(var + GN_EPS) * gng + gnb
    return ((y_gn + bonus) * szr).astype(BF16)


def _stage_weights(w_hbm, layer, w_bf, stage, sem):
    n_chunks = D_MODEL // STAGE_ROWS
    ahead = STAGE_SLOTS - 1

    def chunk_copy(c):
        slot = c % STAGE_SLOTS
        return pltpu.make_async_copy(w_hbm.at[layer, pl.ds(c * STAGE_ROWS, STAGE_ROWS), :],
                                     stage.at[slot], sem.at[slot])

    for c in range(min(ahead, n_chunks)):
        chunk_copy(c).start()
    for c in range(n_chunks):
        if c + ahead < n_chunks:
            chunk_copy(c + ahead).start()
        chunk_copy(c).wait()
        w_bf[c * STAGE_ROWS:(c + 1) * STAGE_ROWS, :] = stage[c % STAGE_SLOTS].astype(BF16)


def _prep_kernel(sample, layer, *refs):
    if sample:
        (x_ref, prev_ref, gpre_ref, win_ref, mu_ref, w0_ref, a0_ref, w2c_ref, kk_ref, ka_ref, rk_ref,
         glub_ref, ones_ref,
         r_out, k_out, v_out, a_out, b_out, lw_out, szr_out, bon_out, u_out, szc_out, sgr_out, sgc_out,
         h_out) = refs
    else:
        (x_ref, gpre_ref, w_hbm, mu_ref, w0_ref, a0_ref, w2c_ref, kk_ref, ka_ref, rk_ref,
         glub_ref, ones_ref,
         r_out, k_out, v_out, a_out, b_out, lw_out, szr_out, bon_out, u_out, szc_out, sgr_out, sgc_out,
         h_out, wbf_out, carry_ref, win_ref, stage, stage_sem, export_sem) = refs
        step = pl.program_id(0) * pl.num_programs(1) + pl.program_id(1)
        export = pltpu.make_async_copy(win_ref, wbf_out, export_sem.at[0])

        @pl.when(step == 0)
        def _():
            _stage_weights(w_hbm, layer, win_ref, stage, stage_sem)
            export.start()

    x = x_ref[0]
    tm = x.shape[0]
    ms = jnp.mean(x * x, axis=-1, keepdims=True)
    h = x * lax.rsqrt(ms + RMS_EPS) * gpre_ref[...]
    hb = h.astype(BF16)

    def proj(off, width):
        return _dot(hb, win_ref[:, off:off + width])

    if sample:
        h_out[0] = h
        pb = prev_ref[0].astype(BF16)
    else:
        h_out[0] = h[tm - SUBLANES:, :]

        @pl.when(pl.program_id(1) == 0)
        def _():
            carry_ref[...] = jnp.zeros_like(carry_ref)
        row0 = lax.broadcasted_iota(jnp.int32, (tm, 1), 0) == 0

    def shifted(off, width):
        p = proj(off, width)
        if sample:
            pp = _dot(pb, win_ref[:, off:off + width])
        else:
            rolled = pltpu.roll(p, shift=1, axis=0)
            pp = jnp.where(row0, carry_ref[0:1, off:off + width], rolled)
            carry_ref[0:1, off:off + width] = p[tm - 1:tm, :]
        return p + (pp - p) * mu_ref[:, off:off + width]

    ones_bd = ones_ref[...]

    lo = shifted(_O_LORA, 2 * LORA)
    k = shifted(_O_K, D_RWKV)

    lane = lax.broadcasted_iota(jnp.int32, lo.shape, 1)
    lo = jnp.where(lane < LORA, jnp.tanh(lo), lo)
    lor = _dot(lo.astype(BF16), w2c_ref[...])

    zr = shifted(_O_ZR, D_RWKV)
    szr_out[0] = zr * _sigmoid(zr)

    kk = k * kk_ref[...]
    kk_ss = _head_sum(kk * kk, ones_bd)

    u_in = proj(_O_U, 2 * D_CONV) + glub_ref[...]
    u_out[0] = u_in[:, :D_CONV] * _sigmoid(u_in[:, D_CONV:])

    w_raw = w0_ref[...] + lor[:, :D_RWKV]
    lw_out[0] = (-math.exp(-0.5)) * _sigmoid(w_raw)
    iclr = _sigmoid(a0_ref[...] + lor[:, D_RWKV:])
    kk = kk * lax.rsqrt(kk_ss + 1e-12)
    k2 = k * (1.0 + (iclr - 1.0) * ka_ref[...])
    k_out[0] = k2
    a_out[0] = -kk
    b_out[0] = kk * iclr

    zc = proj(_O_ZC, D_CONV)
    szc_out[0] = zc * _sigmoid(zc)
    sgr_out[0] = _sigmoid(proj(_O_GR, D_MODEL))
    sgc_out[0] = _sigmoid(proj(_O_GC, D_MODEL))

    r = shifted(_O_R, D_RWKV)
    r_out[0] = r
    v = shifted(_O_V, D_RWKV)
    v_out[0] = v
    bon_out[0] = _head_sum(r * k2 * rk_ref[...], ones_bd) * v

    if not sample:
        @pl.when(step == pl.num_programs(0) * pl.num_programs(1) - 1)
        def _():
            export.wait()


def _const_spec(shape):
    nd = len(shape)
    return pl.BlockSpec(shape, lambda *_: (0,) * nd, pipeline_mode=pl.Buffered(1))


def _prep(x, prev, p, tm):
    B, T, _ = x.shape
    sample = prev is not None
    tile = lambda w: pl.BlockSpec((1, tm, w), lambda b, j: (b, j, 0))
    hbm = pl.BlockSpec(memory_space=pltpu.HBM)
    in_specs = [tile(D_MODEL)] + ([tile(D_MODEL)] if sample else []) + [
        _const_spec((1, D_MODEL)), _const_spec((D_MODEL, N_IN)) if sample else hbm, _const_spec((1, N_SHIFT)),
        _const_spec((1, D_RWKV)), _const_spec((1, D_RWKV)), _const_spec((2 * LORA, 2 * D_RWKV)),
        _const_spec((1, D_RWKV)), _const_spec((1, D_RWKV)), _const_spec((1, D_RWKV)),
        _const_spec((1, 2 * D_CONV)), _const_spec((MXU_DIM, MXU_DIM))]
    widths = [D_RWKV] * 8 + [D_CONV, D_CONV, D_MODEL, D_MODEL]
    h_spec = tile(D_MODEL) if sample else pl.BlockSpec((1, SUBLANES, D_MODEL), lambda b, j: (b, 0, 0))
    out_specs = [tile(w) for w in widths] + [h_spec]
    out_shape = [jax.ShapeDtypeStruct((B, T, w), F32) for w in widths] + [
        jax.ShapeDtypeStruct((B, T if sample else SUBLANES, D_MODEL), F32)]
    args = [x] + ([prev] if sample else []) + [
        p["gpre"], p["w_in"] if sample else p["w_in_f32"], p["mu"], p["w0"], p["a0"], p["w2c"], p["k_k"], p["k_a"],
        p["r_k"], p["glu_b"], p["ones_bd"]]
    scratch_shapes = []
    if not sample:
        assert D_MODEL % STAGE_ROWS == 0
        out_specs.append(hbm)
        out_shape.append(jax.ShapeDtypeStruct((D_MODEL, N_IN), BF16))
        scratch_shapes = [pltpu.VMEM((SUBLANES, N_SHIFT), F32), pltpu.VMEM((D_MODEL, N_IN), BF16),
                          pltpu.VMEM((STAGE_SLOTS, STAGE_ROWS, N_IN), p["w_in_f32"].dtype),
                          pltpu.SemaphoreType.DMA((STAGE_SLOTS,)), pltpu.SemaphoreType.DMA((1,))]
    return pl.pallas_call(
        functools.partial(_prep_kernel, sample, None if sample else p["layer"]),
        grid=(B, T // tm),
        in_specs=in_specs,
        out_specs=out_specs,
        out_shape=out_shape,
        scratch_shapes=scratch_shapes,
        compiler_params=pltpu.CompilerParams(
            dimension_semantics=("arbitrary", "arbitrary"), vmem_limit_bytes=VMEM_LIMIT),
        name="prep_sample" if sample else "prep_prompt",
    )(*args)


def _block_diag(x, same_head):
    return jnp.where(same_head, jnp.concatenate([x, x], axis=0), jnp.zeros((), x.dtype))


def _drain(steps):
    for _ in steps:
        pass


def _wkv_masks():
    row = lax.broadcasted_iota(jnp.int32, (CHUNK, PAIR), 0)
    col = lax.broadcasted_iota(jnp.int32, (CHUNK, PAIR), 1) % HEAD_DIM
    brow = lax.broadcasted_iota(jnp.int32, (PAIR, PAIR), 0)
    bcol = lax.broadcasted_iota(jnp.int32, (PAIR, PAIR), 1)
    trow = lax.broadcasted_iota(jnp.int32, (CHUNK, CHUNK), 0)
    tcol = lax.broadcasted_iota(jnp.int32, (CHUNK, CHUNK), 1)
    return dict(strict=row > col, incl=row >= col, eye_pair=(row == col).astype(F32),
                same_head=(brow // HEAD_DIM) == (bcol // HEAD_DIM), eye_full=brow == bcol,
                first_head=lax.broadcasted_iota(jnp.int32, (CHUNK, PAIR), 1) < HEAD_DIM,
                tri=(trow >= tcol).astype(BF16))


def _block_diag_t(x, same_head):
    return jnp.where(same_head, jnp.concatenate([x, x], axis=0).T, 0.0).astype(BF16)


def _wkv_chunk_operators(r_ref, k_ref, v_ref, a_ref, b_ref, lw_ref, first_chunk, m, tick):
    strict, incl, eye_pair, same_head = m["strict"], m["incl"], m["eye_pair"], m["same_head"]
    bd = lambda x: _block_diag(x.astype(BF16), same_head)
    bd_t = lambda x: _block_diag_t(x, same_head)

    def chunk_inputs(row0):
        rows = pl.ds(row0, CHUNK)
        lw = lw_ref[0, rows, :]
        l1 = lw.astype(BF16)
        rem = lw - l1.astype(F32)
        l2 = rem.astype(BF16)
        l3 = (rem - l2.astype(F32)).astype(BF16)
        cl = _dot(m["tri"], l1) + _dot(m["tri"], l2) + _dot(m["tri"], l3)
        e_neg = jnp.exp(-cl)
        w_end = jnp.exp(cl[CHUNK - 1:CHUNK, :])
        b_t = b_ref[0, rows, :] * e_neg
        k_t = k_ref[0, rows, :] * e_neg
        return dict(r=r_ref[0, rows, :] * jnp.exp(cl), a=(a_ref[0, rows, :] * jnp.exp(cl - lw)).astype(BF16),
                    b=b_t, k=k_t, be=(b_t * w_end).astype(BF16), ke=(k_t * w_end).astype(BF16),
                    v=v_ref[0, rows, :], w_end=w_end)

    ch = []
    for c in range(WKV_GROUP):
        ch.append(chunk_inputs((first_chunk + c) * CHUNK))
        tick()
    pairs = range(N_PAIRS)
    sls = [slice(p * PAIR, (p + 1) * PAIR) for p in pairs]
    units = [(c, p) for c in range(WKV_GROUP) for p in pairs]
    us = range(len(units))
    pick = lambda name: [ch[c][name][:, sls[p]] for c, p in units]
    at, rt, bt, kt, be, ke, vp = (pick(n) for n in ("a", "r", "b", "k", "be", "ke", "v"))
    g = [_dot(jnp.concatenate([at[u], rt[u].astype(BF16)], axis=0),
              jnp.concatenate([bd_t(bt[u]), bd_t(kt[u])], axis=1)) for u in us]
    tick()
    l_ab = [jnp.where(strict, g[u][:CHUNK, :PAIR], 0.0) for u in us]
    m_rb = [jnp.where(incl, g[u][CHUNK:, :PAIR], 0.0).astype(BF16) for u in us]
    km = [jnp.concatenate([jnp.where(strict, g[u][:CHUNK, PAIR:], 0.0),
                           jnp.where(incl, g[u][CHUNK:, PAIR:], 0.0)], axis=0).astype(BF16) for u in us]
    uy = [_dot(km[u], bd(vp[u])) for u in us]
    tick()
    uv0 = [uy[u][:CHUNK].astype(BF16) for u in us]
    t_acc = [eye_pair + l_ab[u] for u in us]
    lp = [_dot(l_ab[u].astype(BF16), bd(l_ab[u])).astype(BF16) for u in us]
    tick()
    for _ in range(4):
        prod = [_dot(lp[u], jnp.concatenate([bd(lp[u]), bd(t_acc[u])], axis=1)) for u in us]
        lp = [prod[u][:, :PAIR].astype(BF16) for u in us]
        t_acc = [t_acc[u] + prod[u][:, PAIR:] for u in us]
        tick()
    t_acc = [t_acc[u] + _dot(lp[u], bd(t_acc[u])) for u in us]
    tick()
    au = [_dot(t_acc[u].astype(BF16), jnp.concatenate([bd(at[u]), bd(uv0[u])], axis=1)) for u in us]
    tick()
    a_hat = [au[u][:, :PAIR] for u in us]
    u_v = [au[u][:, PAIR:] for u in us]
    ry = [_dot(m_rb[u], jnp.concatenate([bd(a_hat[u]), bd(u_v[u])], axis=1)) for u in us]
    tick()
    r_hat = [(rt[u] + ry[u][:, :PAIR]).astype(BF16) for u in us]
    y_v = [ry[u][:, PAIR:] + uy[u][CHUNK:] for u in us]
    phi = [_dot(a_hat[u].T.astype(BF16), be[u]) for u in us]
    tick()
    f = [_dot(jnp.concatenate([u_v[u], vp[u]], axis=0).T.astype(BF16),
              jnp.concatenate([be[u], ke[u]], axis=0)) for u in us]
    tick()
    phi = [(jnp.where(same_head, phi[u], 0.0)
            + jnp.where(m["eye_full"], ch[units[u][0]]["w_end"][:, sls[units[u][1]]], 0.0)).astype(BF16)
           for u in us]
    z = [jnp.where(m["first_head"], f[u][:HEAD_DIM, :], f[u][HEAD_DIM:, :]) for u in us]
    return r_hat, y_v, phi, z


def _wkv_recurrence_steps(slot, rh_ref, yv_ref, phi_ref, z_ref, s_ref, y_ref, m):
    pairs = range(N_PAIRS)
    state = [s_ref[p] for p in pairs]
    for c in range(TILE_CHUNKS):
        us = [c * N_PAIRS + p for p in pairs]
        ys = [_dot(rh_ref[slot, us[p]], _block_diag_t(state[p], m["same_head"])) + yv_ref[slot, us[p]]
              for p in pairs]
        state = [_dot(state[p].astype(BF16), phi_ref[slot, us[p]]) + z_ref[slot, us[p]] for p in pairs]
        y_ref[c * CHUNK:(c + 1) * CHUNK, :] = jnp.concatenate(ys, axis=1)
        yield
    for p in pairs:
        s_ref[p] = state[p]


def _wkv_prompt_kernel(nt, r_ref, k_ref, v_ref, a_ref, b_ref, lw_ref, szr_ref, bon_ref, gng_ref, gnb_ref, ones_ref,
                       o_ref, s_out, s_ref, y_ref, rh_ref, yv_ref, phi_ref, z_ref):
    s = pl.program_id(0)
    prev = jnp.maximum(s - 1, 0)
    cur_slot = s % 2

    @pl.when(s == 0)
    def _():
        for ref in (rh_ref, yv_ref, phi_ref, z_ref):
            ref[1] = jnp.zeros_like(ref[1])

    @pl.when(prev % nt == 0)
    def _():
        s_ref[...] = jnp.zeros_like(s_ref)

    m = _wkv_masks()

    def previous_tile():
        yield from _wkv_recurrence_steps(1 - cur_slot, rh_ref, yv_ref, phi_ref, z_ref, s_ref, y_ref, m)
        o_ref[0] = _group_norm_gate(y_ref[...], bon_ref[0], szr_ref[0], gng_ref[...], gnb_ref[...], ones_ref[...])

    prev_steps = previous_tile()
    for first_chunk in range(0, TILE_CHUNKS, WKV_GROUP):
        ops = _wkv_chunk_operators(r_ref, k_ref, v_ref, a_ref, b_ref, lw_ref, first_chunk, m,
                                   tick=lambda: next(prev_steps, None))
        for i in range(WKV_GROUP * N_PAIRS):
            u = first_chunk * N_PAIRS + i
            for ref, val in zip((rh_ref, yv_ref, phi_ref, z_ref), ops):
                ref[cur_slot, u] = val[i]
    _drain(prev_steps)

    @pl.when((s > 0) & (prev % nt == nt - 1))
    def _():
        for hd in range(N_HEADS):
            lo = (hd % 2) * HEAD_DIM
            s_out[0, hd] = s_ref[hd // 2][:, lo:lo + HEAD_DIM]


def _wkv_prompt(pre, p, tm):
    r, k, v, a, b, lw, szr, bon = pre
    B, T, _ = r.shape
    assert T % tm == 0 and tm == TILE_CHUNKS * CHUNK and TILE_CHUNKS % WKV_GROUP == 0, (T, tm)
    nt = T // tm
    n_tiles = B * nt
    n_units = TILE_CHUNKS * N_PAIRS
    cur = pl.BlockSpec((1, tm, D_RWKV),
                       lambda s: (jnp.minimum(s, n_tiles - 1) // nt, jnp.minimum(s, n_tiles - 1) % nt, 0))
    prv = pl.BlockSpec((1, tm, D_RWKV), lambda s: (jnp.maximum(s - 1, 0) // nt, jnp.maximum(s - 1, 0) % nt, 0))
    state = pl.BlockSpec((1, N_HEADS, HEAD_DIM, HEAD_DIM), lambda s: (jnp.maximum(s - 1, 0) // nt, 0, 0, 0))
    return pl.pallas_call(
        functools.partial(_wkv_prompt_kernel, nt),
        grid=(n_tiles + 1,),
        in_specs=[cur] * 6 + [prv] * 2 + [_const_spec((1, D_RWKV)), _const_spec((1, D_RWKV)),
                                          _const_spec((MXU_DIM, MXU_DIM))],
        out_specs=[prv, state],
        out_shape=[jax.ShapeDtypeStruct((B, T, D_RWKV), BF16),
                   jax.ShapeDtypeStruct((B, N_HEADS, HEAD_DIM, HEAD_DIM), F32)],
        scratch_shapes=[pltpu.VMEM((N_PAIRS, HEAD_DIM, PAIR), F32), pltpu.VMEM((tm, D_RWKV), F32),
                        pltpu.VMEM((2, n_units, CHUNK, PAIR), BF16), pltpu.VMEM((2, n_units, CHUNK, PAIR), F32),
                        pltpu.VMEM((2, n_units, PAIR, PAIR), BF16), pltpu.VMEM((2, n_units, CHUNK, PAIR), F32)],
        compiler_params=pltpu.CompilerParams(
            dimension_semantics=("arbitrary",), vmem_limit_bytes=VMEM_LIMIT),
        name="wkv_prompt",
    )(r, k, v, a, b, lw, szr, bon, p["gn_g"], p["gn_b"], p["ones_bd"])


def _wkv_step_kernel(r_ref, k_ref, v_ref, a_ref, b_ref, lw_ref, szr_ref, bon_ref, gng_ref, gnb_ref,
                     ones_ref, s_hbm, o_ref, s_out, vec_t, y_t, s_buf, s_sem):
    hd = pl.program_id(0)
    ns = s_buf.shape[-1]
    ahead = STATE_SLOTS - 1

    def state_copy(head):
        slot = head % STATE_SLOTS
        return pltpu.make_async_copy(s_hbm.at[head], s_buf.at[slot], s_sem.at[slot])

    @pl.when(hd == 0)
    def _():
        for head in range(ahead):
            state_copy(head).start()

    @pl.when(hd + ahead < pl.num_programs(0))
    def _():
        state_copy(hd + ahead).start()

    @pl.when(hd == 0)
    def _():
        for n, ref in enumerate((a_ref, b_ref, k_ref, r_ref)):
            vec_t[n] = ref[0].T
        vec_t[4] = jnp.exp(lw_ref[0]).T
        vec_t[5] = v_ref[0].T

    base = pl.multiple_of(hd * HEAD_DIM, HEAD_DIM)
    hrows = pl.ds(base, HEAD_DIM)
    a_h, b_h, k_h, r_h, w_h = (vec_t[n, hrows, :] for n in range(5))
    sub = lax.broadcasted_iota(jnp.int32, (SUBLANES, ns), 0)
    state_copy(hd).wait()
    s_in = s_buf.at[hd % STATE_SLOTS]

    def value_group(g, carry):
        off = pl.multiple_of(g * SUBLANES, SUBLANES)
        v_g = vec_t[5, pl.ds(base + off, SUBLANES), :]
        y_g = jnp.zeros((SUBLANES, ns), F32)
        for ii in range(SUBLANES):
            s0 = s_in[off + ii]
            sa = jnp.sum(s0 * a_h, axis=0, keepdims=True)
            s1 = s0 * w_h + sa * b_h + v_g[ii:ii + 1, :] * k_h
            s_out[0, off + ii] = s1
            y_g = jnp.where(sub == ii, jnp.sum(s1 * r_h, axis=0, keepdims=True), y_g)
        y_t[pl.ds(base + off, SUBLANES), :] = y_g
        return carry

    lax.fori_loop(0, HEAD_DIM // SUBLANES, value_group, 0)

    @pl.when(hd == pl.num_programs(0) - 1)
    def _():
        o_ref[0] = _group_norm_gate(y_t[...].T, bon_ref[0], szr_ref[0], gng_ref[...], gnb_ref[...], ones_ref[...])


def _wkv_sample(pre, state_t, p):
    ns = state_t.shape[-1]
    assert N_HEADS >= STATE_SLOTS
    vec = pl.BlockSpec((1, ns, D_RWKV), lambda h: (0, 0, 0))
    st = pl.BlockSpec((1, HEAD_DIM, HEAD_DIM, ns), lambda h: (h, 0, 0, 0))
    return pl.pallas_call(
        _wkv_step_kernel,
        grid=(N_HEADS,),
        in_specs=[vec] * 8 + [_const_spec((1, D_RWKV)), _const_spec((1, D_RWKV)),
                              _const_spec((MXU_DIM, MXU_DIM)), pl.BlockSpec(memory_space=pltpu.HBM)],
        out_specs=[vec, st],
        out_shape=[jax.ShapeDtypeStruct((1, ns, D_RWKV), BF16),
                   jax.ShapeDtypeStruct(state_t.shape, F32)],
        scratch_shapes=[pltpu.VMEM((6, D_RWKV, ns), F32), pltpu.VMEM((D_RWKV, ns), F32),
                        pltpu.VMEM((STATE_SLOTS, HEAD_DIM, HEAD_DIM, ns), F32),
                        pltpu.SemaphoreType.DMA((STATE_SLOTS,))],
        compiler_params=pltpu.CompilerParams(
            dimension_semantics=("arbitrary",), vmem_limit_bytes=VMEM_LIMIT),
        name="wkv_sample",
    )(*pre, p["gn_g"], p["gn_b"], p["ones_bd"], state_t)


def _conv_tile(u, cw_ref, cb_ref, ubuf):
    tm = u.shape[0]
    ubuf[CONV_PAD:CONV_PAD + tm, :] = u
    win = ubuf[...]
    rows = tm + CONV_PAD
    base = CONV_PAD - (CONV_W - 1)
    acc = jnp.broadcast_to(cb_ref[...], u.shape)
    for s in range(SUBLANES):
        shifted = win if s == 0 else pltpu.roll(win, shift=rows - s, axis=0)
        for kk in range(CONV_W):
            if (base + kk) % SUBLANES == s:
                q = base + kk - s
                acc = acc + cw_ref[kk:kk + 1, :] * shifted[q:q + tm, :]
    ubuf[0:CONV_PAD, :] = win[tm:tm + CONV_PAD, :]
    return acc


def _post_tail(conv, x, o_r, szc, sgr, sgc, lng_ref, lnb_ref, wbr_ref, wbc_ref, wout_ref, gpost_ref):
    cm = jnp.mean(conv, axis=-1, keepdims=True)
    d = conv - cm
    cv = jnp.mean(d * d, axis=-1, keepdims=True)
    cn = d * lax.rsqrt(cv + LN_EPS) * lng_ref[...] + lnb_ref[...]
    o_c = (cn * _sigmoid(cn)) * szc
    branch_c = _dot(o_c.astype(BF16), wbc_ref[...])
    branch_r = _dot(o_r, wbr_ref[...])
    merged = sgr * branch_r + sgc * branch_c
    out = _dot(merged.astype(BF16), wout_ref[...])
    ms = jnp.mean(out * out, axis=-1, keepdims=True)
    return x + out * lax.rsqrt(ms + RMS_EPS) * gpost_ref[...]


def _post_prompt_kernel(x_ref, or_ref, u_ref, szc_ref, sgr_ref, sgc_ref, cw_ref, cb_ref, lng_ref, lnb_ref,
                        wbr_ref, wbc_ref, wout_ref, gpost_ref, y_ref, ubuf):
    @pl.when(pl.program_id(1) == 0)
    def _():
        ubuf[0:CONV_PAD, :] = jnp.zeros((CONV_PAD, D_CONV), F32)

    conv = _conv_tile(u_ref[0], cw_ref, cb_ref, ubuf)
    y_ref[0] = _post_tail(conv, x_ref[0], or_ref[0], szc_ref[0], sgr_ref[0], sgc_ref[0],
                          lng_ref, lnb_ref, wbr_ref, wbc_ref, wout_ref, gpost_ref)


def _post_sample_kernel(x_ref, or_ref, u_ref, szc_ref, sgr_ref, sgc_ref, cw_ref, cb_ref, lng_ref, lnb_ref,
                        wbr_ref, wbc_ref, wout_ref, gpost_ref, st_ref, y_ref, st_out):
    u = u_ref[0]
    acc = jnp.broadcast_to(cb_ref[...], u.shape)
    for kk in range(CONV_W - 1):
        acc = acc + cw_ref[kk:kk + 1, :] * st_ref[kk]
    acc = acc + cw_ref[CONV_W - 1:CONV_W, :] * u
    st_out[0:CONV_W - 2] = st_ref[1:CONV_W - 1]
    st_out[CONV_W - 2] = u
    y_ref[0] = _post_tail(acc, x_ref[0], or_ref[0], szc_ref[0], sgr_ref[0], sgc_ref[0],
                          lng_ref, lnb_ref, wbr_ref, wbc_ref, wout_ref, gpost_ref)


def _post(x, o_r, u, szc, sgr, sgc, conv_state, p, tm):
    B, T, _ = x.shape
    sample = conv_state is not None
    tile = lambda w: pl.BlockSpec((1, tm, w), lambda b, j: (b, j, 0))
    in_specs = [tile(D_MODEL), tile(D_RWKV), tile(D_CONV), tile(D_CONV), tile(D_MODEL), tile(D_MODEL),
                _const_spec((CONV_W, D_CONV)), _const_spec((1, D_CONV)), _const_spec((1, D_CONV)),
                _const_spec((1, D_CONV)), _const_spec((D_RWKV, D_MODEL)), _const_spec((D_CONV, D_MODEL)),
                _const_spec((D_MODEL, D_MODEL)), _const_spec((1, D_MODEL))]
    args = [x, o_r, u, szc, sgr, sgc, p["conv_w"], p["conv_b"], p["ln_g"], p["ln_b"],
            p["w_br"], p["w_bc"], p["w_out"], p["gpost"]]
    out_specs = [tile(D_MODEL)]
    out_shape = [jax.ShapeDtypeStruct((B, T, D_MODEL), F32)]
    scratch = []
    if sample:
        st = pl.BlockSpec((CONV_W - 1, tm, D_CONV), lambda b, j: (0, j, 0))
        in_specs.append(st)
        args.append(conv_state)
        out_specs.append(st)
        out_shape.append(jax.ShapeDtypeStruct(conv_state.shape, F32))
    else:
        scratch = [pltpu.VMEM((tm + CONV_PAD, D_CONV), F32)]
    return pl.pallas_call(
        _post_sample_kernel if sample else _post_prompt_kernel,
        grid=(B, T // tm),
        in_specs=in_specs,
        out_specs=out_specs,
        out_shape=out_shape,
        scratch_shapes=scratch,
        compiler_params=pltpu.CompilerParams(
            dimension_semantics=("arbitrary", "arbitrary"), vmem_limit_bytes=VMEM_LIMIT),
        name="post_sample" if sample else "post_prompt",
    )(*args)


def _layer_params(l, norm_pre_g, w_in, mu_shift, decay_w0, decay_w2, iclr_a0, iclr_a2, k_k, k_a, r_k, gn_g, gn_b,
                  conv_glu_b, conv_w, conv_b, ln_c_g, ln_c_b, w_branch_r, w_branch_c, w_out, norm_post_g):
    row = lambda t: t[l].reshape(1, -1).astype(F32)
    zeros = jnp.zeros((LORA, D_RWKV), F32)
    w2c = jnp.concatenate([jnp.concatenate([decay_w2[l], zeros], axis=1),
                           jnp.concatenate([zeros, iclr_a2[l]], axis=1)], axis=0)
    blk = jnp.arange(MXU_DIM) // HEAD_DIM
    return dict(
        gpre=row(norm_pre_g), w_in_f32=w_in, layer=l, mu=row(mu_shift), w0=row(decay_w0), a0=row(iclr_a0),
        w2c=w2c.astype(BF16), k_k=row(k_k), k_a=row(k_a), r_k=row(r_k), glu_b=row(conv_glu_b),
        ones_bd=(blk[:, None] == blk[None, :]).astype(BF16),
        gn_g=row(gn_g), gn_b=row(gn_b), conv_w=conv_w[l].astype(F32), conv_b=row(conv_b),
        ln_g=row(ln_c_g), ln_b=row(ln_c_b), w_br=w_branch_r[l].astype(BF16), w_bc=w_branch_c[l].astype(BF16),
        w_out=w_out[l].astype(BF16), gpost=row(norm_post_g))


def _prompt_layer(x, p, tm_prep, tm_wkv, tm_post):
    T = x.shape[1]
    *pre, u, szc, sgr, sgc, h_last, w_in_bf16 = _prep(x, None, p, tm_prep)
    o_r, wkv = _wkv_prompt(pre, p, tm_wkv)
    (y,) = _post(x, o_r, u, szc, sgr, sgc, None, p, tm_post)
    return y, h_last[:, SUBLANES - 1], wkv, u[:, T - (CONV_W - 1):], w_in_bf16


def _sample_layer(x, shift, wkv, conv, p, tm_post):
    ns = x.shape[0]
    xs = x.reshape(1, ns, D_MODEL)
    *pre, u, szc, sgr, sgc, h = _prep(xs, shift.reshape(1, ns, D_MODEL), p, ns)
    o_r, wkv_new = _wkv_sample(pre, jnp.transpose(wkv, (1, 2, 3, 0)), p)
    y, conv_new = _post(xs, o_r, u, szc, sgr, sgc, jnp.transpose(conv, (1, 0, 2)), p, tm_post)
    return (y.reshape(ns, 1, D_MODEL), h.reshape(ns, D_MODEL), jnp.transpose(wkv_new, (3, 0, 1, 2)),
            jnp.transpose(conv_new, (1, 0, 2)))


def kernel(x_prompt, x_sample, state_shift, state_wkv, state_conv, norm_pre_g, w_in, mu_shift, decay_w0, decay_w2,
           iclr_a0, iclr_a2, k_k, k_a, r_k, gn_g, gn_b, conv_glu_b, conv_w, conv_b, ln_c_g, ln_c_b, w_branch_r,
           w_branch_c, w_out, norm_post_g):
    depth = w_in.shape[0]
    xp, xs = x_prompt, x_sample
    outs = [[] for _ in range(6)]
    for l in range(depth):
        p = _layer_params(l, norm_pre_g, w_in, mu_shift, decay_w0, decay_w2, iclr_a0, iclr_a2, k_k, k_a, r_k, gn_g,
                          gn_b, conv_glu_b, conv_w, conv_b, ln_c_g, ln_c_b, w_branch_r, w_branch_c, w_out,
                          norm_post_g)
        xp, a1, a2, a3, w_in_bf16 = _prompt_layer(xp, p, tm_prep=256, tm_wkv=256, tm_post=512)
        xs, b1, b2, b3 = _sample_layer(xs, state_shift[l], state_wkv[l], state_conv[l], dict(p, w_in=w_in_bf16),
                                       tm_post=64)
        for lst, val in zip(outs, (a1, a2, a3, b1, b2, b3)):
            lst.append(val)
    return (xp, xs) + tuple(jnp.stack(o) for o in outs)
```
